```python
import jax, jax.numpy as jnp
from jax import lax
import numpy as np

D_MODEL = 1024
BATCH = 2
SEQ = 8192
DEPTH = 2
DEC_BATCH = 32
DEC_SEQ = 16
PAST_LEN = 1024

CHUNK = 64
LEFT_CHUNKS = 8
ATTN_REACH = LEFT_CHUNKS * CHUNK
N_HEADS_A = 8
HEAD_DIM_A = 64
D_A = N_HEADS_A * HEAD_DIM_A
REL_CLIP = 128
N_REL = 2 * REL_CLIP + 1
MLP_CHUNK = 128
N_GROUPS_B = 4
D_B = 512
GROUP_DIM_B = D_B // N_GROUPS_B
D_FF = ((8 * D_MODEL // 3 + 255) // 256) * 256
CONV_W = 3
N_BRANCH = 2
D_IN = 3 * D_A + 2 * D_B + N_BRANCH * D_MODEL
EPS = 1e-6

kernel_name = 'streaming_band_attn_spatial_gating_convffn'


def rms_norm(x, g):
    xf = x.astype(jnp.float32)
    y = xf * lax.rsqrt(jnp.mean(xf * xf, axis=-1, keepdims=True) + EPS)
    return (y * g.astype(jnp.float32)).astype(x.dtype)


def ada_params(c, w_ada, b_ada):
    mod = jax.nn.silu(c) @ w_ada + b_ada
    return jnp.split(mod[:, None, :], 6, axis=-1)


def modulate(h, shift, scale):
    return h * (1 + scale) + shift


def mixer_inputs(h, w_in, q_norm_g, k_norm_g, v_norm_g):
    B, T, _ = h.shape
    z = h @ w_in
    q, k, v, zb, gates = jnp.split(z, [D_A, 2 * D_A, 3 * D_A, 3 * D_A + 2 * D_B], axis=-1)
    q = rms_norm(q.reshape(B, T, N_HEADS_A, HEAD_DIM_A), q_norm_g)
    k = rms_norm(k.reshape(B, T, N_HEADS_A, HEAD_DIM_A), k_norm_g)
    v = v.reshape(B, T, N_HEADS_A, HEAD_DIM_A)
    ub, vb = jnp.split(jax.nn.gelu(zb), 2, axis=-1)
    vb = rms_norm(vb, v_norm_g).reshape(B, T, N_GROUPS_B, GROUP_DIM_B)
    g_a, g_b = jnp.split(jax.nn.sigmoid(gates), 2, axis=-1)
    return q, k, v, ub, vb, g_a, g_b


def band_attention_prompt(q, k, v, rel_bias):
    B, S, H, dh = q.shape
    nC = S // CHUNK
    band = (LEFT_CHUNKS + 1) * CHUNK
    qc = q.reshape(B, nC, CHUNK, H, dh)
    pad = jnp.zeros((B, ATTN_REACH, H, dh), k.dtype)
    kp = jnp.concatenate([pad, k], axis=1).reshape(B, nC + LEFT_CHUNKS, CHUNK, H, dh)
    vp = jnp.concatenate([pad.astype(v.dtype), v], axis=1).reshape(B, nC + LEFT_CHUNKS, CHUNK, H, dh)
    kb = jnp.concatenate([kp[:, j:j + nC] for j in range(LEFT_CHUNKS + 1)], axis=2)
    vb = jnp.concatenate([vp[:, j:j + nC] for j in range(LEFT_CHUNKS + 1)], axis=2)
    s = jnp.einsum('bcqhd,bckhd->bchqk', qc, kb, preferred_element_type=jnp.float32) * (dh ** -0.5)
    i = jnp.arange(CHUNK)[:, None]
    kk = jnp.arange(band)[None, :]
    idx = jnp.clip(ATTN_REACH + i - kk, -REL_CLIP, REL_CLIP) + REL_CLIP
    s = s + rel_bias.astype(jnp.float32)[:, idx][None, None]
    key_chunk = jnp.arange(nC)[:, None] - LEFT_CHUNKS + jnp.arange(LEFT_CHUNKS + 1)[None, :]
    valid = jnp.repeat(key_chunk >= 0, CHUNK, axis=1)
    s = jnp.where(valid[None, :, None, None, :], s, -jnp.inf)
    p = jax.nn.softmax(s, axis=-1)
    o = jnp.einsum('bchqk,bckhd->bcqhd', p.astype(v.dtype), vb)
    return o.reshape(B, S, H * dh)


def band_attention_sample(q, k, v, k_cache, v_cache, rel_bias):
    B, T, H, dh = q.shape
    L = k_cache.shape[1]
    kk_all = jnp.concatenate([k_cache.astype(k.dtype), k], axis=1)
    vv_all = jnp.concatenate([v_cache.astype(v.dtype), v], axis=1)
    s = jnp.einsum('bqhd,bkhd->bhqk', q, kk_all, preferred_element_type=jnp.float32) * (dh ** -0.5)
    i = jnp.arange(T)[:, None]
    kk = jnp.arange(L + T)[None, :]
    idx = jnp.clip(L + i - kk, -REL_CLIP, REL_CLIP) + REL_CLIP
    s = s + rel_bias.astype(jnp.float32)[:, idx][None]
    p = jax.nn.softmax(s, axis=-1)
    o = jnp.einsum('bhqk,bkhd->bqhd', p.astype(v.dtype), vv_all)
    return o.reshape(B, T, H * dh)


def causal_spatial(w_spatial):
    return w_spatial * jnp.tril(jnp.ones((MLP_CHUNK, MLP_CHUNK), w_spatial.dtype))


def spatial_gating_prompt(ub, vb, w_spatial, b_spatial):
    B, S, _ = ub.shape
    vc = vb.reshape(B, S // MLP_CHUNK, MLP_CHUNK, N_GROUPS_B, GROUP_DIM_B)
    mix = jnp.einsum('gts,bcsgd->bctgd', causal_spatial(w_spatial), vc)
    mix = mix + b_spatial.T[None, None, :, :, None]
    return ub * mix.reshape(B, S, D_B)


def spatial_gating_sample(ub, vb, w_spatial, b_spatial):
    B, T, _ = ub.shape
    ws = causal_spatial(w_spatial)[:, :T, :T]
    mix = jnp.einsum('gts,bsgd->btgd', ws, vb) + b_spatial[:, :T].T[None, :, :, None]
    return ub * mix.reshape(B, T, D_B)


def conv_ffn(h, conv_cache, w_ffn_in, ffn_conv_w, ffn_conv_b, w_ffn_out):
    B, T, _ = h.shape
    g, u = jnp.split(h @ w_ffn_in, 2, axis=-1)
    if conv_cache is None:
        conv_cache = jnp.zeros((B, CONV_W - 1, D_FF), g.dtype)
    gp = jnp.concatenate([conv_cache.astype(g.dtype), g], axis=1)
    gc = ffn_conv_b
    for j in range(CONV_W):
        gc = gc + ffn_conv_w[j] * gp[:, j:j + T]
    out = (jax.nn.gelu(gc) * u) @ w_ffn_out
    return out, gp[:, -(CONV_W - 1):]


def run_layer(x, c, k_cache, v_cache, conv_cache, norm1_g, norm2_g, w_ada, b_ada, w_in,
              q_norm_g, k_norm_g, rel_bias, v_norm_g, w_spatial, b_spatial,
              w_out_a, w_out_b, w_out, w_ffn_in, ffn_conv_w, ffn_conv_b, w_ffn_out):
    sh1, sc1, gt1, sh2, sc2, gt2 = ada_params(c, w_ada, b_ada)
    h = modulate(rms_norm(x, norm1_g), sh1, sc1)
    q, k, v, ub, vb, g_a, g_b = mixer_inputs(h, w_in, q_norm_g, k_norm_g, v_norm_g)
    if k_cache is None:
        o_a = band_attention_prompt(q, k, v, rel_bias)
        o_b = spatial_gating_prompt(ub, vb, w_spatial, b_spatial)
        keep = min(ATTN_REACH, x.shape[1])
        k_state, v_state, vb_state = k[:, -keep:], v[:, -keep:], None
    else:
        o_a = band_attention_sample(q, k, v, k_cache, v_cache, rel_bias)
        o_b = spatial_gating_sample(ub, vb, w_spatial, b_spatial)
        k_state, v_state, vb_state = k, v, vb
    mixed = (g_a * (o_a @ w_out_a) + g_b * (o_b @ w_out_b)) @ w_out
    x = x + gt1 * mixed
    h2 = modulate(rms_norm(x, norm2_g), sh2, sc2)
    f, conv_state = conv_ffn(h2, conv_cache, w_ffn_in, ffn_conv_w, ffn_conv_b, w_ffn_out)
    x = x + gt2 * f
    return x, k_state, v_state, vb_state, conv_state


def setup_inputs(seed: int = 0) -> dict:
    key = jax.random.key(seed)
    ks = jax.random.split(key, 32)
    f32 = jnp.float32

    def nrm(k, shape, scale):
        return jax.random.normal(k, shape, f32) * scale

    cache_len = min(ATTN_REACH, PAST_LEN)
    return {
        'x_prompt': nrm(ks[0], (BATCH, SEQ, D_MODEL), 1.0),
        'x_sample': nrm(ks[1], (DEC_BATCH, DEC_SEQ, D_MODEL), 1.0),
        'cache_attn_k': nrm(ks[2], (DEPTH, DEC_BATCH, cache_len, N_HEADS_A, HEAD_DIM_A), 1.0),
        'cache_attn_v': nrm(ks[3], (DEPTH, DEC_BATCH, cache_len, N_HEADS_A, HEAD_DIM_A), 1.0),
        'cache_ffn_conv': nrm(ks[4], (DEPTH, DEC_BATCH, CONV_W - 1, D_FF), 1.0),
        'c_prompt': nrm(ks[5], (BATCH, D_MODEL), 1.0),
        'c_sample': nrm(ks[6], (DEC_BATCH, D_MODEL), 1.0),
        'norm1_g': 1.0 + nrm(ks[7], (DEPTH, D_MODEL), 0.02),
        'norm2_g': 1.0 + nrm(ks[8], (DEPTH, D_MODEL), 0.02),
        'w_ada': nrm(ks[9], (DEPTH, D_MODEL, 6 * D_MODEL), 0.5 * D_MODEL ** -0.5),
        'b_ada': nrm(ks[10], (DEPTH, 6 * D_MODEL), 0.02),
        'w_in': nrm(ks[11], (DEPTH, D_MODEL, D_IN), D_MODEL ** -0.5),
        'q_norm_g': 1.0 + nrm(ks[12], (DEPTH, HEAD_DIM_A), 0.02),
        'k_norm_g': 1.0 + nrm(ks[13], (DEPTH, HEAD_DIM_A), 0.02),
        'rel_bias': nrm(ks[14], (DEPTH, N_HEADS_A, N_REL), 0.1),
        'v_norm_g': 1.0 + nrm(ks[15], (DEPTH, D_B), 0.02),
        'w_spatial': nrm(ks[16], (DEPTH, N_GROUPS_B, MLP_CHUNK, MLP_CHUNK), MLP_CHUNK ** -0.5),
        'b_spatial': 1.0 + nrm(ks[17], (DEPTH, N_GROUPS_B, MLP_CHUNK), 0.02),
        'w_out_a': nrm(ks[18], (DEPTH, D_A, D_MODEL), D_A ** -0.5),
        'w_out_b': nrm(ks[19], (DEPTH, D_B, D_MODEL), D_B ** -0.5),
        'w_out': nrm(ks[20], (DEPTH, D_MODEL, D_MODEL), D_MODEL ** -0.5),
        'w_ffn_in': nrm(ks[21], (DEPTH, D_MODEL, 2 * D_FF), D_MODEL ** -0.5),
        'ffn_conv_w': nrm(ks[22], (DEPTH, CONV_W, D_FF), CONV_W ** -0.5),
        'ffn_conv_b': nrm(ks[23], (DEPTH, D_FF), 0.02),
        'w_ffn_out': nrm(ks[24], (DEPTH, D_FF, D_MODEL), D_FF ** -0.5),
    }


def reference(x_prompt, x_sample, cache_attn_k, cache_attn_v, cache_ffn_conv, c_prompt, c_sample,
              norm1_g, norm2_g, w_ada, b_ada, w_in, q_norm_g, k_norm_g, rel_bias, v_norm_g,
              w_spatial, b_spatial, w_out_a, w_out_b, w_out, w_ffn_in, ffn_conv_w, ffn_conv_b,
              w_ffn_out):
    xp, xs = x_prompt, x_sample
    kp_l, vp_l, cp_l, ks_l, vs_l, bs_l, cs_l = [], [], [], [], [], [], []
    for l in range(DEPTH):
        lw = (norm1_g[l], norm2_g[l], w_ada[l], b_ada[l], w_in[l], q_norm_g[l], k_norm_g[l],
              rel_bias[l], v_norm_g[l], w_spatial[l], b_spatial[l], w_out_a[l], w_out_b[l],
              w_out[l], w_ffn_in[l], ffn_conv_w[l], ffn_conv_b[l], w_ffn_out[l])
        xp, kp, vp, _, cp = run_layer(xp, c_prompt, None, None, None, *lw)
        xs, ks, vs, bs, cs = run_layer(xs, c_sample, cache_attn_k[l], cache_attn_v[l],
                                       cache_ffn_conv[l], *lw)
        kp_l.append(kp); vp_l.append(vp); cp_l.append(cp)
        ks_l.append(ks); vs_l.append(vs); bs_l.append(bs); cs_l.append(cs)
    new_k_prompt = jnp.stack(kp_l)
    new_v_prompt = jnp.stack(vp_l)
    new_conv_prompt = jnp.stack(cp_l)
    new_k_sample = jnp.stack(ks_l)
    new_v_sample = jnp.stack(vs_l)
    new_spatial_v_sample = jnp.stack(bs_l)
    new_conv_sample = jnp.stack(cs_l)
    return (xp, xs, new_k_prompt, new_v_prompt, new_conv_prompt, new_k_sample, new_v_sample, new_spatial_v_sample, new_conv_sample)
```

```python
import functools

import jax
import jax.numpy as jnp
from jax import lax
from jax.experimental import pallas as pl
from jax.experimental.pallas import tpu as pltpu

F32 = jnp.float32
BF16 = jnp.bfloat16

D_MODEL = 1024
CHUNK = 64
ATTN_REACH = 512
N_HEADS = 8
HEAD_DIM = 64
D_A = N_HEADS * HEAD_DIM
REL_CLIP = 128
MLP_CHUNK = 128
N_GROUPS = 4
D_B = 512
GROUP_DIM = D_B // N_GROUPS
D_FF = 2816
CONV_W = 3
EPS = 1e-6

LANES = 128
SUBLANES = 8
N_PAIRS = N_HEADS // 2
Q_BLOCK = 2 * CHUNK
KV_WINDOW = ATTN_REACH + Q_BLOCK
TILE_T = 256
HIST = ATTN_REACH
FF_CHUNKS = ((0, 1024), (1024, 2048), (2048, D_FF))
SAMPLE_BG = 4
FF_BLOCK = 256
VMEM_LIMIT = 56 * 1024 * 1024

_Q0, _K0, _V0, _B0, _G0 = 0, D_A, 2 * D_A, 3 * D_A, 3 * D_A + 2 * D_B
D_IN = _G0 + 2 * D_MODEL


def _dot(a, b):
    return jnp.dot(a, b, preferred_element_type=F32)


def _dot_nt(a, b):
    return lax.dot_general(a, b, (((1,), (1,)), ((), ())), preferred_element_type=F32)


def _rms(x, g):
    return (x * lax.rsqrt(jnp.mean(x * x, axis=-1, keepdims=True) + EPS)) * g


def _head_rms(a, g, e_ref):
    sq = a * a
    hi = sq.astype(BF16)
    lo = (sq - hi.astype(F32)).astype(BF16)
    ms = _dot(hi, e_ref[...]) + _dot(lo, e_ref[...])
    return (a * lax.rsqrt(ms + EPS)) * g


def _const_spec(shape):
    n = len(shape)
    return pl.BlockSpec(shape, lambda *_: (0,) * n, pipeline_mode=pl.Buffered(1))


def _ada_kernel(c_ref, w_ref, b_ref, o_ref):
    s = jax.nn.silu(c_ref[...]).astype(BF16)
    o_ref[...] = _dot(s, w_ref[...].astype(BF16)) + b_ref[...]


def _ada_call(c_all, w_ada, b_ada):
    depth = w_ada.shape[0]
    rows = c_all.shape[0]
    n_out = w_ada.shape[2]
    bn = 1536
    return pl.pallas_call(
        _ada_kernel,
        grid=(depth, n_out // bn),
        in_specs=[
            pl.BlockSpec((rows, D_MODEL), lambda l, n: (0, 0)),
            pl.BlockSpec((None, D_MODEL, bn), lambda l, n: (l, 0, n)),
            pl.BlockSpec((None, 1, bn), lambda l, n: (l, 0, n)),
        ],
        out_specs=pl.BlockSpec((None, rows, bn), lambda l, n: (l, 0, n)),
        out_shape=jax.ShapeDtypeStruct((depth, rows, n_out), F32),
        compiler_params=pltpu.CompilerParams(
            dimension_semantics=("arbitrary", "arbitrary"), vmem_limit_bytes=VMEM_LIMIT),
        name="ada_mod",
    )(c_all, w_ada, b_ada.reshape(depth, 1, n_out))


def _mixer_prompt_kernel(x_ref, mod_ref, n1g_ref, win_ref, qg_ref, kg_ref, e_ref, vg_ref, bias_ref,
                         wsp_ref, bsp_ref, woa_ref, wob_ref, wo_ref,
                         xo_ref, ks_ref, vs_ref,
                         kbuf, vbuf, oa_scr, ob_scr):
    t = pl.program_id(1)
    T = TILE_T

    @pl.when(t == 0)
    def _():
        kbuf[0:HIST, :] = jnp.zeros((HIST, D_A), BF16)
        vbuf[0:HIST, :] = jnp.zeros((HIST, D_A), BF16)

    @pl.when(t > 0)
    def _():
        for buf in (kbuf, vbuf):
            for r in range(0, HIST, T):
                buf[r:r + T, :] = buf[r + T:r + 2 * T, :]

    x = x_ref[...]
    mod = mod_ref[...]
    sh1, sc1, gt1 = mod[0:1], mod[1:2], mod[2:3]
    h = (_rms(x, n1g_ref[...]) * (1.0 + sc1) + sh1).astype(BF16)

    q = _head_rms(_dot(h, win_ref[:, _Q0:_Q0 + D_A]), qg_ref[...], e_ref)
    k = _head_rms(_dot(h, win_ref[:, _K0:_K0 + D_A]), kg_ref[...], e_ref)
    v = _dot(h, win_ref[:, _V0:_V0 + D_A])
    ks_ref[...] = k
    vs_ref[...] = v
    kbuf[HIST:HIST + T, :] = k.astype(BF16)
    vbuf[HIST:HIST + T, :] = v.astype(BF16)
    qb = (q * (HEAD_DIM ** -0.5)).astype(BF16)

    lane = lax.broadcasted_iota(jnp.int32, (1, LANES), 1)
    low_half = lane < HEAD_DIM
    col = lax.broadcasted_iota(jnp.int32, (1, KV_WINDOW), 1)
    for j in range(T // Q_BLOCK):
        r0 = j * Q_BLOCK
        first_valid = HIST - t * T - r0
        pad_mask = jnp.where(col < first_valid, -jnp.inf, 0.0).astype(F32)
        for p in range(N_PAIRS):
            c0 = p * LANES
            qp = qb[r0:r0 + Q_BLOCK, c0:c0 + LANES]
            q2 = jnp.concatenate([jnp.where(low_half, qp, jnp.zeros_like(qp)),
                                  jnp.where(low_half, jnp.zeros_like(qp), qp)], axis=0)
            kw = kbuf[r0:r0 + KV_WINDOW, c0:c0 + LANES]
            vw = vbuf[r0:r0 + KV_WINDOW, c0:c0 + LANES]
            s = _dot_nt(q2, kw) + bias_ref[p] + pad_mask
            m = jnp.max(s, axis=-1, keepdims=True)
            e = jnp.exp(s - m)
            l = jnp.sum(e, axis=-1, keepdims=True)
            o2 = _dot(e.astype(BF16), vw) / l
            oa_scr[r0:r0 + Q_BLOCK, c0:c0 + LANES] = jnp.where(low_half, o2[0:Q_BLOCK], o2[Q_BLOCK:])

    zb = jax.nn.gelu(_dot(h, win_ref[:, _B0:_B0 + 2 * D_B]))
    ub = zb[:, 0:D_B]
    vbn = _rms(zb[:, D_B:2 * D_B], vg_ref[...]).astype(BF16)
    row_i = lax.broadcasted_iota(jnp.int32, (MLP_CHUNK, MLP_CHUNK), 0)
    col_i = lax.broadcasted_iota(jnp.int32, (MLP_CHUNK, MLP_CHUNK), 1)
    for g in range(N_GROUPS):
        wc = jnp.where(row_i >= col_i, wsp_ref[g], 0.0).astype(BF16)
        g0 = g * GROUP_DIM
        for c in range(T // MLP_CHUNK):
            r0 = c * MLP_CHUNK
            mix = _dot(wc, vbn[r0:r0 + MLP_CHUNK, g0:g0 + GROUP_DIM]) + bsp_ref[g]
            ob_scr[r0:r0 + MLP_CHUNK, g0:g0 + GROUP_DIM] = ub[r0:r0 + MLP_CHUNK, g0:g0 + GROUP_DIM] * mix

    ga = jax.nn.sigmoid(_dot(h, win_ref[:, _G0:_G0 + D_MODEL]))
    gb = jax.nn.sigmoid(_dot(h, win_ref[:, _G0 + D_MODEL:_G0 + 2 * D_MODEL]))
    merged = ga * _dot(oa_scr[...].astype(BF16), woa_ref[...]) + gb * _dot(ob_scr[...].astype(BF16), wob_ref[...])
    xo_ref[...] = x + gt1 * _dot(merged.astype(BF16), wo_ref[...])


def _mixer_prompt_call(x, mod, n1g, win, qg, kg, e_mat, vg, bias_tab, wsp, bsp_b, woa, wob, wo):
    B, S, _ = x.shape
    T = TILE_T
    n_t = S // T
    keep_t = ATTN_REACH // T
    tok = pl.BlockSpec((None, T, D_MODEL), lambda b, t: (b, t, 0))
    state = pl.BlockSpec((None, T, D_A), lambda b, t: (b, jnp.maximum(t - (n_t - keep_t), 0), 0))
    return pl.pallas_call(
        _mixer_prompt_kernel,
        grid=(B, n_t),
        in_specs=[
            tok,
            pl.BlockSpec((None, 6, D_MODEL), lambda b, t: (b, 0, 0)),
            _const_spec((1, D_MODEL)),
            _const_spec((D_MODEL, D_IN)),
            _const_spec((1, D_A)),
            _const_spec((1, D_A)),
            _const_spec((D_A, D_A)),
            _const_spec((1, D_B)),
            _const_spec((N_PAIRS, 2 * Q_BLOCK, KV_WINDOW)),
            _const_spec((N_GROUPS, MLP_CHUNK, MLP_CHUNK)),
            _const_spec((N_GROUPS, MLP_CHUNK, GROUP_DIM)),
            _const_spec((D_A, D_MODEL)),
            _const_spec((D_B, D_MODEL)),
            _const_spec((D_MODEL, D_MODEL)),
        ],
        out_specs=[tok, state, state],
        out_shape=[
            jax.ShapeDtypeStruct((B, S, D_MODEL), F32),
            jax.ShapeDtypeStruct((B, ATTN_REACH, D_A), F32),
            jax.ShapeDtypeStruct((B, ATTN_REACH, D_A), F32),
        ],
        scratch_shapes=[
            pltpu.VMEM((HIST + T, D_A), BF16),
            pltpu.VMEM((HIST + T, D_A), BF16),
            pltpu.VMEM((T, D_A), F32),
            pltpu.VMEM((T, D_B), F32),
        ],
        compiler_params=pltpu.CompilerParams(
            dimension_semantics=("arbitrary", "arbitrary"), vmem_limit_bytes=VMEM_LIMIT),
        name="mixer_prompt",
    )(x, mod, n1g, win, qg, kg, e_mat, vg, bias_tab, wsp, bsp_b, woa, wob, wo)


def _conv_gate(g, u, prev1, prev2, row, cw_ref, cb_ref, c0, c1):
    gm1 = jnp.where(row == 0, prev1, pltpu.roll(g, 1, axis=0))
    gm2 = jnp.where(row == 0, prev2, jnp.where(row == 1, prev1, pltpu.roll(g, 2, axis=0)))
    gc = cb_ref[:, c0:c1] + cw_ref[0:1, c0:c1] * gm2
    gc = gc + cw_ref[1:2, c0:c1] * gm1
    gc = gc + cw_ref[2:3, c0:c1] * g
    return jax.nn.gelu(gc) * u


def _ffn_prompt_kernel(x_ref, mod_ref, n2g_ref, wfi_ref, cw_ref, cb_ref, wfo_ref,
                       xo_ref, cs_ref, carry):
    t = pl.program_id(1)
    T = TILE_T

    @pl.when(t == 0)
    def _():
        carry[...] = jnp.zeros(carry.shape, F32)

    x = x_ref[...]
    mod = mod_ref[...]
    sh2, sc2, gt2 = mod[3:4], mod[4:5], mod[5:6]
    h2 = (_rms(x, n2g_ref[...]) * (1.0 + sc2) + sh2).astype(BF16)
    row = lax.broadcasted_iota(jnp.int32, (T, 1), 0)
    f = jnp.zeros((T, D_MODEL), F32)
    for c0, c1 in FF_CHUNKS:
        g = _dot(h2, wfi_ref[:, c0:c1])
        u = _dot(h2, wfi_ref[:, D_FF + c0:D_FF + c1])
        prev2 = carry[SUBLANES - 2:SUBLANES - 1, c0:c1]
        prev1 = carry[SUBLANES - 1:SUBLANES, c0:c1]
        act = _conv_gate(g, u, prev1, prev2, row, cw_ref, cb_ref, c0, c1)
        f = f + _dot(act.astype(BF16), wfo_ref[c0:c1, :])
        tail = g[T - SUBLANES:T, :]
        carry[:, c0:c1] = tail
        cs_ref[:, c0:c1] = tail
    xo_ref[...] = x + gt2 * f


def _ffn_prompt_call(x, mod, n2g, wfi, cw, cb, wfo):
    B, S, _ = x.shape
    T = TILE_T
    tok = pl.BlockSpec((None, T, D_MODEL), lambda b, t: (b, t, 0))
    return pl.pallas_call(
        _ffn_prompt_kernel,
        grid=(B, S // T),
        in_specs=[
            tok,
            pl.BlockSpec((None, 6, D_MODEL), lambda b, t: (b, 0, 0)),
            _const_spec((1, D_MODEL)),
            _const_spec((D_MODEL, 2 * D_FF)),
            _const_spec((CONV_W, D_FF)),
            _const_spec((1, D_FF)),
            _const_spec((D_FF, D_MODEL)),
        ],
        out_specs=[tok, pl.BlockSpec((None, SUBLANES, D_FF), lambda b, t: (b, 0, 0))],
        out_shape=[
            jax.ShapeDtypeStruct((B, S, D_MODEL), F32),
            jax.ShapeDtypeStruct((B, SUBLANES, D_FF), F32),
        ],
        scratch_shapes=[pltpu.VMEM((SUBLANES, D_FF), F32)],
        compiler_params=pltpu.CompilerParams(
            dimension_semantics=("arbitrary", "arbitrary"), vmem_limit_bytes=VMEM_LIMIT),
        name="ffn_prompt",
    )(x, mod, n2g, wfi, cw, cb, wfo)


def _mixer_sample_kernel(x_ref, mod_ref, n1g_ref, win_ref, qg_ref, kg_ref, e_ref, vg_ref,
                         bias_c_ref, bias_n_ref, kc_ref, vc_ref, wsp_ref, bsp_ref,
                         woa_ref, wob_ref, wo_ref,
                         xo_ref, ks_ref, vs_ref, sv_ref,
                         q_scr, kn_scr, vn_scr, oa_scr, *, n_batch, n_tok):
    step = pl.program_id(0)
    n_rows = n_batch * n_tok

    def per_token(m):
        return jnp.broadcast_to(m, (n_batch, n_tok, m.shape[-1])).reshape(n_rows, m.shape[-1])

    def normed_input():
        mod = mod_ref[...]
        sh1, sc1 = per_token(mod[:, 0:1, :]), per_token(mod[:, 1:2, :])
        return (_rms(x_ref[...], n1g_ref[...]) * (1.0 + sc1) + sh1).astype(BF16)

    @pl.when(step == 0)
    def _():
        h = normed_input()
        q = _head_rms(_dot(h, win_ref[:, _Q0:_Q0 + D_A]), qg_ref[...], e_ref)
        k = _head_rms(_dot(h, win_ref[:, _K0:_K0 + D_A]), kg_ref[...], e_ref)
        v = _dot(h, win_ref[:, _V0:_V0 + D_A])
        ks_ref[...] = k
        vs_ref[...] = v
        kn_scr[...] = k.astype(BF16)
        vn_scr[...] = v.astype(BF16)
        q_scr[...] = (q * (HEAD_DIM ** -0.5)).astype(BF16)

    lane = lax.broadcasted_iota(jnp.int32, (1, LANES), 1)
    low_half = lane < HEAD_DIM
    for i in range(SAMPLE_BG):
        r0 = pl.multiple_of((step * SAMPLE_BG + i) * n_tok, n_tok)
        for p in range(N_PAIRS):
            c0 = p * LANES
            qp = q_scr[pl.ds(r0, n_tok), c0:c0 + LANES]
            q2 = jnp.concatenate([jnp.where(low_half, qp, jnp.zeros_like(qp)),
                                  jnp.where(low_half, jnp.zeros_like(qp), qp)], axis=0)
            kc = kc_ref[i, :, c0:c0 + LANES].astype(BF16)
            vc = vc_ref[i, :, c0:c0 + LANES].astype(BF16)
            kn = kn_scr[pl.ds(r0, n_tok), c0:c0 + LANES]
            vn = vn_scr[pl.ds(r0, n_tok), c0:c0 + LANES]
            s_c = _dot_nt(q2, kc) + bias_c_ref[p]
            s_n = _dot_nt(q2, kn) + bias_n_ref[p]
            m = jnp.maximum(jnp.max(s_c, axis=-1, keepdims=True), jnp.max(s_n, axis=-1, keepdims=True))
            e_c = jnp.exp(s_c - m)
            e_n = jnp.exp(s_n - m)
            l = jnp.sum(e_c, axis=-1, keepdims=True) + jnp.sum(e_n, axis=-1, keepdims=True)
            o2 = (_dot(e_c.astype(BF16), vc) + _dot(e_n.astype(BF16), vn)) / l
            oa_scr[pl.ds(r0, n_tok), c0:c0 + LANES] = jnp.where(low_half, o2[0:n_tok], o2[n_tok:])

    @pl.when(step == pl.num_programs(0) - 1)
    def _():
        h = normed_input()
        zb = jax.nn.gelu(_dot(h, win_ref[:, _B0:_B0 + 2 * D_B]))
        ub = zb[:, 0:D_B]
        vbn = _rms(zb[:, D_B:2 * D_B], vg_ref[...])
        sv_ref[...] = vbn
        vbn16 = vbn.astype(BF16)
        mix = jnp.concatenate(
            [_dot(wsp_ref[g], vbn16[:, g * GROUP_DIM:(g + 1) * GROUP_DIM]) for g in range(N_GROUPS)], axis=1)
        ob = ub * (mix + per_token(bsp_ref[...][None]))
        ga = jax.nn.sigmoid(_dot(h, win_ref[:, _G0:_G0 + D_MODEL]))
        gb = jax.nn.sigmoid(_dot(h, win_ref[:, _G0 + D_MODEL:_G0 + 2 * D_MODEL]))
        merged = ga * _dot(oa_scr[...].astype(BF16), woa_ref[...]) + gb * _dot(ob.astype(BF16), wob_ref[...])
        gt1 = per_token(mod_ref[...][:, 2:3, :])
        xo_ref[...] = x_ref[...] + gt1 * _dot(merged.astype(BF16), wo_ref[...])


def _mixer_sample_call(x2, mod, n1g, win, qg, kg, e_mat, vg, bias_c, bias_n, kc, vc, wsp_big, bsp_s,
                       woa, wob, wo, n_batch, n_tok):
    n_rows = n_batch * n_tok
    cache_len = kc.shape[1]
    cache = pl.BlockSpec((SAMPLE_BG, cache_len, D_A), lambda s: (s, 0, 0))
    full = lambda shape: pl.BlockSpec(shape, lambda s: (0,) * len(shape))
    return pl.pallas_call(
        functools.partial(_mixer_sample_kernel, n_batch=n_batch, n_tok=n_tok),
        grid=(n_batch // SAMPLE_BG,),
        in_specs=[
            _const_spec((n_rows, D_MODEL)),
            _const_spec((n_batch, 6, D_MODEL)),
            _const_spec((1, D_MODEL)),
            _const_spec((D_MODEL, D_IN)),
            _const_spec((1, D_A)),
            _const_spec((1, D_A)),
            _const_spec((D_A, D_A)),
            _const_spec((1, D_B)),
            _const_spec((N_PAIRS, 2 * n_tok, cache_len)),
            _const_spec((N_PAIRS, 2 * n_tok, n_tok)),
            cache,
            cache,
            _const_spec((N_GROUPS, n_rows, n_rows)),
            _const_spec((n_tok, D_B)),
            _const_spec((D_A, D_MODEL)),
            _const_spec((D_B, D_MODEL)),
            _const_spec((D_MODEL, D_MODEL)),
        ],
        out_specs=[full((n_rows, D_MODEL)), full((n_rows, D_A)), full((n_rows, D_A)), full((n_rows, D_B))],
        out_shape=[
            jax.ShapeDtypeStruct((n_rows, D_MODEL), F32),
            jax.ShapeDtypeStruct((n_rows, D_A), F32),
            jax.ShapeDtypeStruct((n_rows, D_A), F32),
            jax.ShapeDtypeStruct((n_rows, D_B), F32),
        ],
        scratch_shapes=[
            pltpu.VMEM((n_rows, D_A), BF16),
            pltpu.VMEM((n_rows, D_A), BF16),
            pltpu.VMEM((n_rows, D_A), BF16),
            pltpu.VMEM((n_rows, D_A), F32),
        ],
        compiler_params=pltpu.CompilerParams(
            dimension_semantics=("arbitrary",), vmem_limit_bytes=VMEM_LIMIT),
        name="mixer_sample",
    )(x2, mod, n1g, win, qg, kg, e_mat, vg, bias_c, bias_n, kc, vc, wsp_big, bsp_s, woa, wob, wo)


def _ffn_sample_kernel(x_ref, mod_ref, n2g_ref, wg_ref, wu_ref, cw_ref, cb_ref, cc_ref, wfo_ref,
                       xo_ref, cs_ref, h_scr, acc, *, n_batch, n_tok):
    step = pl.program_id(0)
    n_rows = n_batch * n_tok

    def per_token(m):
        return jnp.broadcast_to(m, (n_batch, n_tok, m.shape[-1])).reshape(n_rows, m.shape[-1])

    @pl.when(step == 0)
    def _():
        mod = mod_ref[...]
        sh2, sc2 = per_token(mod[:, 3:4, :]), per_token(mod[:, 4:5, :])
        h_scr[...] = (_rms(x_ref[...], n2g_ref[...]) * (1.0 + sc2) + sh2).astype(BF16)
        acc[...] = jnp.zeros(acc.shape, F32)

    h2 = h_scr[...]
    g = _dot(h2, wg_ref[...])
    u = _dot(h2, wu_ref[...])
    cc = cc_ref[...]
    prev2, prev1 = per_token(cc[:, 0:1, :]), per_token(cc[:, 1:2, :])
    row = lax.broadcasted_iota(jnp.int32, (n_rows, 1), 0) % n_tok
    act = _conv_gate(g, u, prev1, prev2, row, cw_ref, cb_ref, 0, FF_BLOCK)
    acc[...] += _dot(act.astype(BF16), wfo_ref[...])
    cs_ref[...] = g.reshape(n_batch, n_tok, FF_BLOCK)[:, n_tok - SUBLANES:, :]

    @pl.when(step == pl.num_programs(0) - 1)
    def _():
        gt2 = per_token(mod_ref[...][:, 5:6, :])
        xo_ref[...] = x_ref[...] + gt2 * acc[...]


def _ffn_sample_call(x2, mod, n2g, wfi, cw, cb, cc, wfo, n_batch, n_tok):
    n_rows = n_batch * n_tok
    n_blk = D_FF // FF_BLOCK
    return pl.pallas_call(
        functools.partial(_ffn_sample_kernel, n_batch=n_batch, n_tok=n_tok),
        grid=(n_blk,),
        in_specs=[
            _const_spec((n_rows, D_MODEL)),
            _const_spec((n_batch, 6, D_MODEL)),
            _const_spec((1, D_MODEL)),
            pl.BlockSpec((D_MODEL, FF_BLOCK), lambda c: (0, c)),
            pl.BlockSpec((D_MODEL, FF_BLOCK), lambda c: (0, n_blk + c)),
            pl.BlockSpec((CONV_W, FF_BLOCK), lambda c: (0, c)),
            pl.BlockSpec((1, FF_BLOCK), lambda c: (0, c)),
            pl.BlockSpec((n_batch, CONV_W - 1, FF_BLOCK), lambda c: (0, 0, c)),
            pl.BlockSpec((FF_BLOCK, D_MODEL), lambda c: (c, 0)),
        ],
        out_specs=[
            pl.BlockSpec((n_rows, D_MODEL), lambda c: (0, 0)),
            pl.BlockSpec((n_batch, SUBLANES, FF_BLOCK), lambda c: (0, 0, c)),
        ],
        out_shape=[
            jax.ShapeDtypeStruct((n_rows, D_MODEL), F32),
            jax.ShapeDtypeStruct((n_batch, SUBLANES, D_FF), F32),
        ],
        scratch_shapes=[pltpu.VMEM((n_rows, D_MODEL), BF16), pltpu.VMEM((n_rows, D_MODEL), F32)],
        compiler_params=pltpu.CompilerParams(
            dimension_semantics=("arbitrary",), vmem_limit_bytes=VMEM_LIMIT),
        name="ffn_sample",
    )(x2, mod, n2g, wfi, wfi, cw, cb, cc, wfo)


def _pair_rows(tab):
    h, r, c = tab.shape
    return tab.reshape(h // 2, 2 * r, c)


def _prompt_bias_table(rel_bias):
    i = jnp.arange(Q_BLOCK)[:, None]
    kk = jnp.arange(KV_WINDOW)[None, :]
    idx = jnp.clip(ATTN_REACH + i - kk, -REL_CLIP, REL_CLIP) + REL_CLIP
    band = (kk // CHUNK >= i // CHUNK) & (kk // CHUNK <= i // CHUNK + ATTN_REACH // CHUNK)
    return _pair_rows(jnp.where(band[None], rel_bias[:, idx], -jnp.inf))


def _sample_bias_tables(rel_bias, cache_len, n_tok):
    i = jnp.arange(n_tok)[:, None]
    kk = jnp.arange(cache_len + n_tok)[None, :]
    idx = jnp.clip(cache_len + i - kk, -REL_CLIP, REL_CLIP) + REL_CLIP
    tab = rel_bias[:, idx]
    return _pair_rows(tab[:, :, :cache_len]), _pair_rows(tab[:, :, cache_len:])


def kernel(x_prompt, x_sample, cache_attn_k, cache_attn_v, cache_ffn_conv, c_prompt, c_sample, norm1_g, norm2_g, w_ada, b_ada, w_in, q_norm_g, k_norm_g, rel_bias, v_norm_g, w_spatial, b_spatial, w_out_a, w_out_b, w_out, w_ffn_in, ffn_conv_w, ffn_conv_b, w_ffn_out):
    depth = w_in.shape[0]
    B, S, _ = x_prompt.shape
    NB, NT, _ = x_sample.shape
    cache_len = cache_attn_k.shape[2]
    assert S % TILE_T == 0 and TILE_T % Q_BLOCK == 0 and HIST % TILE_T == 0 and S >= ATTN_REACH
    assert NB % SAMPLE_BG == 0 and NT % SUBLANES == 0 and NT >= SUBLANES and NT <= MLP_CHUNK

    rows = B + NB
    rows_pad = -(-rows // SUBLANES) * SUBLANES
    c_all = jnp.concatenate([c_prompt, c_sample, jnp.zeros((rows_pad - rows, D_MODEL), F32)], axis=0)
    mod_all = _ada_call(c_all, w_ada, b_ada).reshape(depth, rows_pad, 6, D_MODEL)

    win16, woa16, wob16, wo16 = (w.astype(BF16) for w in (w_in, w_out_a, w_out_b, w_out))
    wfi16, wfo16 = w_ffn_in.astype(BF16), w_ffn_out.astype(BF16)
    head_of = jnp.arange(D_A) // HEAD_DIM
    e_mat = jnp.where(head_of[:, None] == head_of[None, :], 1.0 / HEAD_DIM, 0.0).astype(BF16)
    tril = jnp.tril(jnp.ones((MLP_CHUNK, MLP_CHUNK), F32))
    eye_b = jnp.eye(NB, dtype=F32)

    xp = x_prompt
    xs = x_sample.reshape(NB * NT, D_MODEL)
    outs = [[] for _ in range(7)]
    for l in range(depth):
        mod_p, mod_s = mod_all[l, :B], mod_all[l, B:rows]
        n1g, n2g = norm1_g[l][None], norm2_g[l][None]
        qg = jnp.tile(q_norm_g[l], N_HEADS)[None]
        kg = jnp.tile(k_norm_g[l], N_HEADS)[None]
        vg = v_norm_g[l][None]
        bsp_b = jnp.broadcast_to(b_spatial[l][:, :, None], (N_GROUPS, MLP_CHUNK, GROUP_DIM))
        cw, cb = ffn_conv_w[l], ffn_conv_b[l][None]

        xp, kp, vp = _mixer_prompt_call(xp, mod_p, n1g, win16[l], qg, kg, e_mat, vg,
                                        _prompt_bias_table(rel_bias[l]), w_spatial[l], bsp_b,
                                        woa16[l], wob16[l], wo16[l])
        xp, cp = _ffn_prompt_call(xp, mod_p, n2g, wfi16[l], cw, cb, wfo16[l])

        bias_c, bias_n = _sample_bias_tables(rel_bias[l], cache_len, NT)
        ws_small = (w_spatial[l] * tril)[:, :NT, :NT]
        wsp_big = jnp.einsum('ab,gts->gatbs', eye_b, ws_small).reshape(N_GROUPS, NB * NT, NB * NT).astype(BF16)
        bsp_s = jnp.broadcast_to(b_spatial[l][:, :NT].T[:, :, None], (NT, N_GROUPS, GROUP_DIM)).reshape(NT, D_B)
        xs, ks, vs, sv = _mixer_sample_call(
            xs, mod_s, n1g, win16[l], qg, kg, e_mat, vg, bias_c, bias_n,
            cache_attn_k[l].reshape(NB, cache_len, D_A), cache_attn_v[l].reshape(NB, cache_len, D_A),
            wsp_big, bsp_s, woa16[l], wob16[l], wo16[l], NB, NT)
        xs, cs = _ffn_sample_call(xs, mod_s, n2g, wfi16[l], cw, cb, cache_ffn_conv[l], wfo16[l], NB, NT)

        outs[0].append(kp.reshape(B, ATTN_REACH, N_HEADS, HEAD_DIM))
        outs[1].append(vp.reshape(B, ATTN_REACH, N_HEADS, HEAD_DIM))
        outs[2].append(cp[:, SUBLANES - (CONV_W - 1):])
        outs[3].append(ks.reshape(NB, NT, N_HEADS, HEAD_DIM))
        outs[4].append(vs.reshape(NB, NT, N_HEADS, HEAD_DIM))
        outs[5].append(sv.reshape(NB, NT, N_GROUPS, GROUP_DIM))
        outs[6].append(cs[:, SUBLANES - (CONV_W - 1):])
    stacked = [jnp.stack(o) for o in outs]
    return (xp, xs.reshape(NB, NT, D_MODEL), *stacked)
```

```python
import functools

import jax
import jax.numpy as jnp
from jax import lax
from jax.experimental import pallas as pl
from jax.experimental.pallas import tpu as pltpu

F32 = jnp.float32
BF16 = jnp.bfloat16

D_MODEL = 1024
CHUNK = 64
ATTN_REACH = 512
N_HEADS = 8
HEAD_DIM = 64
D_A = N_HEADS * HEAD_DIM
REL_CLIP = 128
MLP_CHUNK = 128
N_GROUPS = 4
D_B = 512
GROUP_DIM = D_B // N_GROUPS
D_FF = 2816
CONV_W = 3
EPS = 1e-6

LANES = 128
SUBLANES = 8
N_PAIRS = N_HEADS // 2
Q_BLOCK = 2 * CHUNK
KV_WINDOW = ATTN_REACH + Q_BLOCK
REL_SPAN = KV_WINDOW + Q_BLOCK
TILE_T = 256
HIST = ATTN_REACH
FF_CHUNKS = ((0, 1024), (1024, 2048), (2048, D_FF))
SAMPLE_BG = 4
FF_BLOCK = 256
VMEM_LIMIT = 56 * 1024 * 1024

_Q0, _K0, _V0, _B0, _G0 = 0, D_A, 2 * D_A, 3 * D_A, 3 * D_A + 2 * D_B
D_IN = _G0 + 2 * D_MODEL


def _dot(a, b):
    return jnp.dot(a, b, preferred_element_type=F32)


def _dot_nt(a, b):
    return lax.dot_general(a, b, (((1,), (1,)), ((), ())), preferred_element_type=F32)


def _rms(x, g):
    return (x * lax.rsqrt(jnp.mean(x * x, axis=-1, keepdims=True) + EPS)) * g


def _head_rms(a, g, e_ref):
    sq = a * a
    hi = sq.astype(BF16)
    lo = (sq - hi.astype(F32)).astype(BF16)
    ms = _dot(hi, e_ref[...]) + _dot(lo, e_ref[...])
    return (a * lax.rsqrt(ms + EPS)) * g


def _rel_rows(rel_ref, h, n_rows):
    wb = jnp.broadcast_to(rel_ref[h:h + 1, :], (n_rows, REL_SPAN))
    return pltpu.roll(wb, 0, axis=1, stride=1, stride_axis=0)


def _const_spec(shape):
    n = len(shape)
    return pl.BlockSpec(shape, lambda *_: (0,) * n, pipeline_mode=pl.Buffered(1))


def _layer_spec(l, shape):
    n = len(shape)
    return pl.BlockSpec((None, *shape), lambda *_: (l,) + (0,) * n, pipeline_mode=pl.Buffered(1))


def _ada_kernel(c_ref, w_ref, b_ref, o_ref):
    s = jax.nn.silu(c_ref[...]).astype(BF16)
    o_ref[...] = _dot(s, w_ref[...].astype(BF16)) + b_ref[...]


def _ada_call(c_all, w_ada, b_ada):
    depth = w_ada.shape[0]
    rows = c_all.shape[0]
    n_out = w_ada.shape[2]
    bn = 1536
    return pl.pallas_call(
        _ada_kernel,
        grid=(depth, n_out // bn),
        in_specs=[
            pl.BlockSpec((rows, D_MODEL), lambda l, n: (0, 0)),
            pl.BlockSpec((None, D_MODEL, bn), lambda l, n: (l, 0, n)),
            pl.BlockSpec((None, 1, bn), lambda l, n: (l, 0, n)),
        ],
        out_specs=pl.BlockSpec((None, rows, bn), lambda l, n: (l, 0, n)),
        out_shape=jax.ShapeDtypeStruct((depth, rows, n_out), F32),
        compiler_params=pltpu.CompilerParams(
            dimension_semantics=("arbitrary", "arbitrary"), vmem_limit_bytes=VMEM_LIMIT),
        name="ada_mod",
    )(c_all, w_ada, b_ada.reshape(depth, 1, n_out))


def _mixer_prompt_kernel(x_ref, mod_ref, n1g_ref, win_ref, qg_ref, kg_ref, e_ref, vg_ref, rel_ref,
                         wsp_ref, bsp_ref, woa_ref, wob_ref, wo_ref,
                         xo_ref, ks_ref, vs_ref,
                         kbuf, vbuf, oa_scr, ob_scr, bias_scr):
    b = pl.program_id(0)
    t = pl.program_id(1)
    T = TILE_T

    @pl.when((b == 0) & (t == 0))
    def _():
        qi = lax.broadcasted_iota(jnp.int32, (Q_BLOCK, KV_WINDOW), 0) // CHUNK
        ki = lax.broadcasted_iota(jnp.int32, (Q_BLOCK, KV_WINDOW), 1) // CHUNK
        band = (ki >= qi) & (ki <= qi + ATTN_REACH // CHUNK)
        for h in range(N_HEADS):
            tab = jnp.where(band, _rel_rows(rel_ref, h, Q_BLOCK)[:, 0:KV_WINDOW], -jnp.inf)
            bias_scr[h // 2, (h % 2) * Q_BLOCK:(h % 2 + 1) * Q_BLOCK, :] = tab

    @pl.when(t == 0)
    def _():
        kbuf[0:HIST, :] = jnp.zeros((HIST, D_A), BF16)
        vbuf[0:HIST, :] = jnp.zeros((HIST, D_A), BF16)

    @pl.when(t > 0)
    def _():
        for buf in (kbuf, vbuf):
            for r in range(0, HIST, T):
                buf[r:r + T, :] = buf[r + T:r + 2 * T, :]

    x = x_ref[...]
    mod = mod_ref[...]
    sh1, sc1, gt1 = mod[0:1], mod[1:2], mod[2:3]
    h = (_rms(x, n1g_ref[...]) * (1.0 + sc1) + sh1).astype(BF16)

    q = _head_rms(_dot(h, win_ref[:, _Q0:_Q0 + D_A]), qg_ref[...], e_ref)
    k = _head_rms(_dot(h, win_ref[:, _K0:_K0 + D_A]), kg_ref[...], e_ref)
    v = _dot(h, win_ref[:, _V0:_V0 + D_A])
    ks_ref[...] = k
    vs_ref[...] = v
    kbuf[HIST:HIST + T, :] = k.astype(BF16)
    vbuf[HIST:HIST + T, :] = v.astype(BF16)
    qb = (q * (HEAD_DIM ** -0.5)).astype(BF16)

    lane = lax.broadcasted_iota(jnp.int32, (1, LANES), 1)
    low_half = lane < HEAD_DIM
    col = lax.broadcasted_iota(jnp.int32, (1, KV_WINDOW), 1)
    for j in range(T // Q_BLOCK):
        r0 = j * Q_BLOCK
        first_valid = HIST - t * T - r0
        pad_mask = jnp.where(col < first_valid, -jnp.inf, 0.0).astype(F32)
        for p in range(N_PAIRS):
            c0 = p * LANES
            qp = qb[r0:r0 + Q_BLOCK, c0:c0 + LANES]
            q2 = jnp.concatenate([jnp.where(low_half, qp, jnp.zeros_like(qp)),
                                  jnp.where(low_half, jnp.zeros_like(qp), qp)], axis=0)
            kw = kbuf[r0:r0 + KV_WINDOW, c0:c0 + LANES]
            vw = vbuf[r0:r0 + KV_WINDOW, c0:c0 + LANES]
            s = _dot_nt(q2, kw) + bias_scr[p] + pad_mask
            m = jnp.max(s, axis=-1, keepdims=True)
            e = jnp.exp(s - m)
            l = jnp.sum(e, axis=-1, keepdims=True)
            o2 = _dot(e.astype(BF16), vw) / l
            oa_scr[r0:r0 + Q_BLOCK, c0:c0 + LANES] = jnp.where(low_half, o2[0:Q_BLOCK], o2[Q_BLOCK:])

    zb = jax.nn.gelu(_dot(h, win_ref[:, _B0:_B0 + 2 * D_B]))
    ub = zb[:, 0:D_B]
    vbn = _rms(zb[:, D_B:2 * D_B], vg_ref[...]).astype(BF16)
    row_i = lax.broadcasted_iota(jnp.int32, (MLP_CHUNK, MLP_CHUNK), 0)
    col_i = lax.broadcasted_iota(jnp.int32, (MLP_CHUNK, MLP_CHUNK), 1)
    for g in range(N_GROUPS):
        wc = jnp.where(row_i >= col_i, wsp_ref[g], 0.0).astype(BF16)
        g0 = g * GROUP_DIM
        for c in range(T // MLP_CHUNK):
            r0 = c * MLP_CHUNK
            mix = _dot(wc, vbn[r0:r0 + MLP_CHUNK, g0:g0 + GROUP_DIM]) + bsp_ref[g]
            ob_scr[r0:r0 + MLP_CHUNK, g0:g0 + GROUP_DIM] = ub[r0:r0 + MLP_CHUNK, g0:g0 + GROUP_DIM] * mix

    ga = jax.nn.sigmoid(_dot(h, win_ref[:, _G0:_G0 + D_MODEL]))
    gb = jax.nn.sigmoid(_dot(h, win_ref[:, _G0 + D_MODEL:_G0 + 2 * D_MODEL]))
    merged = ga * _dot(oa_scr[...].astype(BF16), woa_ref[...]) + gb * _dot(ob_scr[...].astype(BF16), wob_ref[...])
    xo_ref[...] = x + gt1 * _dot(merged.astype(BF16), wo_ref[...])


def _mixer_prompt_call(l, x, mod_all, mod_row0, p):
    B, S, _ = x.shape
    T = TILE_T
    n_t = S // T
    keep_t = ATTN_REACH // T
    tok = pl.BlockSpec((None, T, D_MODEL), lambda b, t: (b, t, 0))
    state = pl.BlockSpec((None, T, D_A), lambda b, t: (b, jnp.maximum(t - (n_t - keep_t), 0), 0))
    return pl.pallas_call(
        _mixer_prompt_kernel,
        grid=(B, n_t),
        in_specs=[
            tok,
            pl.BlockSpec((None, None, 6, D_MODEL), lambda b, t: (l, mod_row0 + b, 0, 0)),
            _layer_spec(l, (1, D_MODEL)),
            _layer_spec(l, (D_MODEL, D_IN)),
            _layer_spec(l, (1, D_A)),
            _layer_spec(l, (1, D_A)),
            _const_spec((D_A, D_A)),
            _layer_spec(l, (1, D_B)),
            _layer_spec(l, (N_HEADS, REL_SPAN)),
            _layer_spec(l, (N_GROUPS, MLP_CHUNK, MLP_CHUNK)),
            _layer_spec(l, (N_GROUPS, MLP_CHUNK, GROUP_DIM)),
            _layer_spec(l, (D_A, D_MODEL)),
            _layer_spec(l, (D_B, D_MODEL)),
            _layer_spec(l, (D_MODEL, D_MODEL)),
        ],
        out_specs=[tok, state, state],
        out_shape=[
            jax.ShapeDtypeStruct((B, S, D_MODEL), F32),
            jax.ShapeDtypeStruct((B, ATTN_REACH, D_A), F32),
            jax.ShapeDtypeStruct((B, ATTN_REACH, D_A), F32),
        ],
        scratch_shapes=[
            pltpu.VMEM((HIST + T, D_A), BF16),
            pltpu.VMEM((HIST + T, D_A), BF16),
            pltpu.VMEM((T, D_A), F32),
            pltpu.VMEM((T, D_B), F32),
            pltpu.VMEM((N_PAIRS, 2 * Q_BLOCK, KV_WINDOW), F32),
        ],
        compiler_params=pltpu.CompilerParams(
            dimension_semantics=("arbitrary", "arbitrary"), vmem_limit_bytes=VMEM_LIMIT),
        name="mixer_prompt",
    )(x, mod_all, p["n1g"], p["win"], p["qg"], p["kg"], p["e_mat"], p["vg"], p["rel"],
      p["wsp"], p["bsp_b"], p["woa"], p["wob"], p["wo"])


def _conv_gate(g, u, prev1, prev2, row, cw_ref, cb_ref, c0, c1):
    gm1 = jnp.where(row == 0, prev1, pltpu.roll(g, 1, axis=0))
    gm2 = jnp.where(row == 0, prev2, jnp.where(row == 1, prev1, pltpu.roll(g, 2, axis=0)))
    gc = cb_ref[:, c0:c1] + cw_ref[0:1, c0:c1] * gm2
    gc = gc + cw_ref[1:2, c0:c1] * gm1
    gc = gc + cw_ref[2:3, c0:c1] * g
    return jax.nn.gelu(gc) * u


def _ffn_prompt_kernel(x_ref, mod_ref, n2g_ref, wfi_ref, cw_ref, cb_ref, wfo_ref,
                       xo_ref, cs_ref, carry):
    t = pl.program_id(1)
    T = TILE_T

    @pl.when(t == 0)
    def _():
        carry[...] = jnp.zeros(carry.shape, F32)

    x = x_ref[...]
    mod = mod_ref[...]
    sh2, sc2, gt2 = mod[3:4], mod[4:5], mod[5:6]
    h2 = (_rms(x, n2g_ref[...]) * (1.0 + sc2) + sh2).astype(BF16)
    row = lax.broadcasted_iota(jnp.int32, (T, 1), 0)
    f = jnp.zeros((T, D_MODEL), F32)
    for c0, c1 in FF_CHUNKS:
        g = _dot(h2, wfi_ref[:, c0:c1])
        u = _dot(h2, wfi_ref[:, D_FF + c0:D_FF + c1])
        prev2 = carry[SUBLANES - 2:SUBLANES - 1, c0:c1]
        prev1 = carry[SUBLANES - 1:SUBLANES, c0:c1]
        act = _conv_gate(g, u, prev1, prev2, row, cw_ref, cb_ref, c0, c1)
        f = f + _dot(act.astype(BF16), wfo_ref[c0:c1, :])
        tail = g[T - SUBLANES:T, :]
        carry[:, c0:c1] = tail
        cs_ref[:, c0:c1] = tail
    xo_ref[...] = x + gt2 * f


def _ffn_prompt_call(l, x, mod_all, mod_row0, p):
    B, S, _ = x.shape
    T = TILE_T
    tok = pl.BlockSpec((None, T, D_MODEL), lambda b, t: (b, t, 0))
    return pl.pallas_call(
        _ffn_prompt_kernel,
        grid=(B, S // T),
        in_specs=[
            tok,
            pl.BlockSpec((None, None, 6, D_MODEL), lambda b, t: (l, mod_row0 + b, 0, 0)),
            _layer_spec(l, (1, D_MODEL)),
            _layer_spec(l, (D_MODEL, 2 * D_FF)),
            _layer_spec(l, (CONV_W, D_FF)),
            _layer_spec(l, (1, D_FF)),
            _layer_spec(l, (D_FF, D_MODEL)),
        ],
        out_specs=[tok, pl.BlockSpec((None, SUBLANES, D_FF), lambda b, t: (b, 0, 0))],
        out_shape=[
            jax.ShapeDtypeStruct((B, S, D_MODEL), F32),
            jax.ShapeDtypeStruct((B, SUBLANES, D_FF), F32),
        ],
        scratch_shapes=[pltpu.VMEM((SUBLANES, D_FF), F32)],
        compiler_params=pltpu.CompilerParams(
            dimension_semantics=("arbitrary", "arbitrary"), vmem_limit_bytes=VMEM_LIMIT),
        name="ffn_prompt",
    )(x, mod_all, p["n2g"], p["wfi"], p["cw"], p["cb"], p["wfo"])


def _mixer_sample_kernel(x_ref, mod_ref, n1g_ref, win_ref, qg_ref, kg_ref, e_ref, vg_ref, rel_ref,
                         kc_ref, vc_ref, wsp_ref, bsp_ref, woa_ref, wob_ref, wo_ref,
                         xo_ref, ks_ref, vs_ref, sv_ref,
                         q_scr, kn_scr, vn_scr, oa_scr, ub_scr, vb_scr, ob_scr, bias_scr,
                         *, n_batch, n_tok, cache_len):
    step = pl.program_id(0)
    n_rows = n_batch * n_tok

    def per_token(m):
        return jnp.broadcast_to(m, (n_batch, n_tok, m.shape[-1])).reshape(n_rows, m.shape[-1])

    def normed_input():
        mod = mod_ref[...]
        sh1, sc1 = per_token(mod[:, 0:1, :]), per_token(mod[:, 1:2, :])
        return (_rms(x_ref[...], n1g_ref[...]) * (1.0 + sc1) + sh1).astype(BF16)

    @pl.when(step == 0)
    def _():
        h = normed_input()
        q = _head_rms(_dot(h, win_ref[:, _Q0:_Q0 + D_A]), qg_ref[...], e_ref)
        k = _head_rms(_dot(h, win_ref[:, _K0:_K0 + D_A]), kg_ref[...], e_ref)
        v = _dot(h, win_ref[:, _V0:_V0 + D_A])
        ks_ref[...] = k
        vs_ref[...] = v
        kn_scr[...] = k.astype(BF16)
        vn_scr[...] = v.astype(BF16)
        q_scr[...] = (q * (HEAD_DIM ** -0.5)).astype(BF16)
        for hh in range(N_HEADS):
            tab = _rel_rows(rel_ref, hh, n_tok)[:, 0:cache_len + LANES]
            bias_scr[hh // 2, (hh % 2) * n_tok:(hh % 2 + 1) * n_tok, :] = tab

    lane = lax.broadcasted_iota(jnp.int32, (1, LANES), 1)
    low_half = lane < HEAD_DIM
    for i in range(SAMPLE_BG):
        r0 = pl.multiple_of((step * SAMPLE_BG + i) * n_tok, n_tok)
        for p in range(N_PAIRS):
            c0 = p * LANES
            qp = q_scr[pl.ds(r0, n_tok), c0:c0 + LANES]
            q2 = jnp.concatenate([jnp.where(low_half, qp, jnp.zeros_like(qp)),
                                  jnp.where(low_half, jnp.zeros_like(qp), qp)], axis=0)
            kc = kc_ref[i, :, c0:c0 + LANES].astype(BF16)
            vc = vc_ref[i, :, c0:c0 + LANES].astype(BF16)
            kn = kn_scr[pl.ds(r0, n_tok), c0:c0 + LANES]
            vn = vn_scr[pl.ds(r0, n_tok), c0:c0 + LANES]
            s_c = _dot_nt(q2, kc) + bias_scr[p, :, 0:cache_len]
            s_n = _dot_nt(q2, kn) + bias_scr[p, :, cache_len:cache_len + n_tok]
            m = jnp.maximum(jnp.max(s_c, axis=-1, keepdims=True), jnp.max(s_n, axis=-1, keepdims=True))
            e_c = jnp.exp(s_c - m)
            e_n = jnp.exp(s_n - m)
            l = jnp.sum(e_c, axis=-1, keepdims=True) + jnp.sum(e_n, axis=-1, keepdims=True)
            o2 = (_dot(e_c.astype(BF16), vc) + _dot(e_n.astype(BF16), vn)) / l
            oa_scr[pl.ds(r0, n_tok), c0:c0 + LANES] = jnp.where(low_half, o2[0:n_tok], o2[n_tok:])

    @pl.when(step == pl.num_programs(0) - 1)
    def _():
        h = normed_input()
        zb = jax.nn.gelu(_dot(h, win_ref[:, _B0:_B0 + 2 * D_B]))
        vbn = _rms(zb[:, D_B:2 * D_B], vg_ref[...])
        sv_ref[...] = vbn
        for g in range(N_GROUPS):
            ub_scr[g] = zb[:, g * GROUP_DIM:(g + 1) * GROUP_DIM]
            vb_scr[g] = vbn[:, g * GROUP_DIM:(g + 1) * GROUP_DIM]
        for g in range(N_GROUPS):
            for t in range(n_tok):
                mix = bsp_ref[g, t:t + 1, :]
                for s in range(t + 1):
                    mix = mix + wsp_ref[g, t, s:s + 1, :] * vb_scr[g, pl.ds(s, n_batch, stride=n_tok), :]
                ob_scr[g, pl.ds(t, n_batch, stride=n_tok), :] = ub_scr[g, pl.ds(t, n_batch, stride=n_tok), :] * mix
        ob = jnp.concatenate([ob_scr[g] for g in range(N_GROUPS)], axis=1)
        ga = jax.nn.sigmoid(_dot(h, win_ref[:, _G0:_G0 + D_MODEL]))
        gb = jax.nn.sigmoid(_dot(h, win_ref[:, _G0 + D_MODEL:_G0 + 2 * D_MODEL]))
        merged = ga * _dot(oa_scr[...].astype(BF16), woa_ref[...]) + gb * _dot(ob.astype(BF16), wob_ref[...])
        gt1 = per_token(mod_ref[...][:, 2:3, :])
        xo_ref[...] = x_ref[...] + gt1 * _dot(merged.astype(BF16), wo_ref[...])


def _mixer_sample_call(l, x2, mod_all, kc_all, vc_all, p, n_batch, n_tok):
    n_rows = n_batch * n_tok
    cache_len = kc_all.shape[2]
    cache = pl.BlockSpec((None, SAMPLE_BG, cache_len, D_A), lambda s: (l, s, 0, 0))
    full = lambda shape: pl.BlockSpec(shape, lambda s: (0,) * len(shape))
    return pl.pallas_call(
        functools.partial(_mixer_sample_kernel, n_batch=n_batch, n_tok=n_tok, cache_len=cache_len),
        grid=(n_batch // SAMPLE_BG,),
        in_specs=[
            _const_spec((n_rows, D_MODEL)),
            pl.BlockSpec((None, n_batch, 6, D_MODEL), lambda s: (l, 0, 0, 0), pipeline_mode=pl.Buffered(1)),
            _layer_spec(l, (1, D_MODEL)),
            _layer_spec(l, (D_MODEL, D_IN)),
            _layer_spec(l, (1, D_A)),
            _layer_spec(l, (1, D_A)),
            _const_spec((D_A, D_A)),
            _layer_spec(l, (1, D_B)),
            _layer_spec(l, (N_HEADS, REL_SPAN)),
            cache,
            cache,
            _layer_spec(l, (N_GROUPS, n_tok, n_tok, GROUP_DIM)),
            _layer_spec(l, (N_GROUPS, n_tok, GROUP_DIM)),
            _layer_spec(l, (D_A, D_MODEL)),
            _layer_spec(l, (D_B, D_MODEL)),
            _layer_spec(l, (D_MODEL, D_MODEL)),
        ],
        out_specs=[full((n_rows, D_MODEL)), full((n_rows, D_A)), full((n_rows, D_A)), full((n_rows, D_B))],
        out_shape=[
            jax.ShapeDtypeStruct((n_rows, D_MODEL), F32),
            jax.ShapeDtypeStruct((n_rows, D_A), F32),
            jax.ShapeDtypeStruct((n_rows, D_A), F32),
            jax.ShapeDtypeStruct((n_rows, D_B), F32),
        ],
        scratch_shapes=[
            pltpu.VMEM((n_rows, D_A), BF16),
            pltpu.VMEM((n_rows, D_A), BF16),
            pltpu.VMEM((n_rows, D_A), BF16),
            pltpu.VMEM((n_rows, D_A), F32),
            pltpu.VMEM((N_GROUPS, n_rows, GROUP_DIM), F32),
            pltpu.VMEM((N_GROUPS, n_rows, GROUP_DIM), F32),
            pltpu.VMEM((N_GROUPS, n_rows, GROUP_DIM), F32),
            pltpu.VMEM((N_PAIRS, 2 * n_tok, cache_len + LANES), F32),
        ],
        compiler_params=pltpu.CompilerParams(
            dimension_semantics=("arbitrary",), vmem_limit_bytes=VMEM_LIMIT),
        name="mixer_sample",
    )(x2, mod_all, p["n1g"], p["win"], p["qg"], p["kg"], p["e_mat"], p["vg"], p["rel"],
      kc_all, vc_all, p["wsp_s"], p["bsp_s"], p["woa"], p["wob"], p["wo"])


def _ffn_sample_kernel(x_ref, mod_ref, n2g_ref, wg_ref, wu_ref, cw_ref, cb_ref, cc_ref, wfo_ref,
                       xo_ref, cs_ref, h_scr, acc, *, n_batch, n_tok):
    step = pl.program_id(0)
    n_rows = n_batch * n_tok

    def per_token(m):
        return jnp.broadcast_to(m, (n_batch, n_tok, m.shape[-1])).reshape(n_rows, m.shape[-1])

    @pl.when(step == 0)
    def _():
        mod = mod_ref[...]
        sh2, sc2 = per_token(mod[:, 3:4, :]), per_token(mod[:, 4:5, :])
        h_scr[...] = (_rms(x_ref[...], n2g_ref[...]) * (1.0 + sc2) + sh2).astype(BF16)
        acc[...] = jnp.zeros(acc.shape, F32)

    h2 = h_scr[...]
    g = _dot(h2, wg_ref[...])
    u = _dot(h2, wu_ref[...])
    cc = cc_ref[...]
    prev2, prev1 = per_token(cc[:, 0:1, :]), per_token(cc[:, 1:2, :])
    row = lax.broadcasted_iota(jnp.int32, (n_rows, 1), 0) % n_tok
    act = _conv_gate(g, u, prev1, prev2, row, cw_ref, cb_ref, 0, FF_BLOCK)
    acc[...] += _dot(act.astype(BF16), wfo_ref[...])
    cs_ref[...] = g.reshape(n_batch, n_tok, FF_BLOCK)[:, n_tok - SUBLANES:, :]

    @pl.when(step == pl.num_programs(0) - 1)
    def _():
        gt2 = per_token(mod_ref[...][:, 5:6, :])
        xo_ref[...] = x_ref[...] + gt2 * acc[...]


def _ffn_sample_call(l, x2, mod_all, cc_all, p, n_batch, n_tok):
    n_rows = n_batch * n_tok
    n_blk = D_FF // FF_BLOCK
    return pl.pallas_call(
        functools.partial(_ffn_sample_kernel, n_batch=n_batch, n_tok=n_tok),
        grid=(n_blk,),
        in_specs=[
            _const_spec((n_rows, D_MODEL)),
            pl.BlockSpec((None, n_batch, 6, D_MODEL), lambda c: (l, 0, 0, 0), pipeline_mode=pl.Buffered(1)),
            _layer_spec(l, (1, D_MODEL)),
            pl.BlockSpec((None, D_MODEL, FF_BLOCK), lambda c: (l, 0, c)),
            pl.BlockSpec((None, D_MODEL, FF_BLOCK), lambda c: (l, 0, n_blk + c)),
            pl.BlockSpec((None, CONV_W, FF_BLOCK), lambda c: (l, 0, c)),
            pl.BlockSpec((None, 1, FF_BLOCK), lambda c: (l, 0, c)),
            pl.BlockSpec((None, n_batch, CONV_W - 1, FF_BLOCK), lambda c: (l, 0, 0, c)),
            pl.BlockSpec((None, FF_BLOCK, D_MODEL), lambda c: (l, c, 0)),
        ],
        out_specs=[
            pl.BlockSpec((n_rows, D_MODEL), lambda c: (0, 0)),
            pl.BlockSpec((n_batch, SUBLANES, FF_BLOCK), lambda c: (0, 0, c)),
        ],
        out_shape=[
            jax.ShapeDtypeStruct((n_rows, D_MODEL), F32),
            jax.ShapeDtypeStruct((n_batch, SUBLANES, D_FF), F32),
        ],
        scratch_shapes=[pltpu.VMEM((n_rows, D_MODEL), BF16), pltpu.VMEM((n_rows, D_MODEL), F32)],
        compiler_params=pltpu.CompilerParams(
            dimension_semantics=("arbitrary",), vmem_limit_bytes=VMEM_LIMIT),
        name="ffn_sample",
    )(x2, mod_all, p["n2g"], p["wfi"], p["wfi"], p["cw"], p["cb"], cc_all, p["wfo"])


def _rel_vector(rel_bias):
    far = rel_bias[..., 2 * REL_CLIP:]
    lead = ATTN_REACH - REL_CLIP
    return jnp.concatenate([
        jnp.broadcast_to(far, (*rel_bias.shape[:-1], lead)),
        rel_bias[..., ::-1],
        jnp.broadcast_to(far, (*rel_bias.shape[:-1], REL_SPAN - lead - (2 * REL_CLIP + 1))),
    ], axis=-1)


def kernel(x_prompt, x_sample, cache_attn_k, cache_attn_v, cache_ffn_conv, c_prompt, c_sample, norm1_g, norm2_g, w_ada, b_ada, w_in, q_norm_g, k_norm_g, rel_bias, v_norm_g, w_spatial, b_spatial, w_out_a, w_out_b, w_out, w_ffn_in, ffn_conv_w, ffn_conv_b, w_ffn_out):
    depth = w_in.shape[0]
    B, S, _ = x_prompt.shape
    NB, NT, _ = x_sample.shape
    cache_len = cache_attn_k.shape[2]
    assert S % TILE_T == 0 and TILE_T % Q_BLOCK == 0 and HIST % TILE_T == 0 and S >= ATTN_REACH
    assert NB % SAMPLE_BG == 0 and NT % SUBLANES == 0 and NT <= MLP_CHUNK and NT <= LANES
    assert cache_len == ATTN_REACH and rel_bias.shape[-1] == 2 * REL_CLIP + 1

    rows = NB + B
    rows_pad = -(-rows // SUBLANES) * SUBLANES
    c_all = jnp.concatenate([c_sample, c_prompt, jnp.zeros((rows_pad - rows, D_MODEL), F32)], axis=0)
    mod_all = _ada_call(c_all, w_ada, b_ada).reshape(depth, rows_pad, 6, D_MODEL)

    head_of = jnp.arange(D_A) // HEAD_DIM
    tril = jnp.tril(jnp.ones((NT, NT), F32))
    ws_small = w_spatial[:, :, :NT, :NT] * tril
    params = {
        "n1g": norm1_g[:, None, :], "n2g": norm2_g[:, None, :],
        "win": w_in.astype(BF16), "woa": w_out_a.astype(BF16), "wob": w_out_b.astype(BF16),
        "wo": w_out.astype(BF16), "wfi": w_ffn_in.astype(BF16), "wfo": w_ffn_out.astype(BF16),
        "qg": jnp.tile(q_norm_g, (1, N_HEADS))[:, None, :], "kg": jnp.tile(k_norm_g, (1, N_HEADS))[:, None, :],
        "vg": v_norm_g[:, None, :],
        "e_mat": jnp.where(head_of[:, None] == head_of[None, :], 1.0 / HEAD_DIM, 0.0).astype(BF16),
        "rel": _rel_vector(rel_bias),
        "wsp": w_spatial,
        "bsp_b": jnp.broadcast_to(b_spatial[:, :, :, None], (depth, N_GROUPS, MLP_CHUNK, GROUP_DIM)),
        "wsp_s": jnp.broadcast_to(ws_small[..., None], (depth, N_GROUPS, NT, NT, GROUP_DIM)),
        "bsp_s": jnp.broadcast_to(b_spatial[:, :, :NT, None], (depth, N_GROUPS, NT, GROUP_DIM)),
        "cw": ffn_conv_w, "cb": ffn_conv_b[:, None, :],
    }
    kc_all = cache_attn_k.reshape(depth, NB, cache_len, D_A)
    vc_all = cache_attn_v.reshape(depth, NB, cache_len, D_A)

    xp = x_prompt
    xs = x_sample.reshape(NB * NT, D_MODEL)
    outs = [[] for _ in range(7)]
    for l in range(depth):
        xp, kp, vp = _mixer_prompt_call(l, xp, mod_all, NB, params)
        xp, cp = _ffn_prompt_call(l, xp, mod_all, NB, params)
        xs, ks, vs, sv = _mixer_sample_call(l, xs, mod_all, kc_all, vc_all, params, NB, NT)
        xs, cs = _ffn_sample_call(l, xs, mod_all, cache_ffn_conv, params, NB, NT)

        outs[0].append(kp.reshape(B, ATTN_REACH, N_HEADS, HEAD_DIM))
        outs[1].append(vp.reshape(B, ATTN_REACH, N_HEADS, HEAD_DIM))
        outs[2].append(cp[:, SUBLANES - (CONV_W - 1):])
        outs[3].append(ks.reshape(NB, NT, N_HEADS, HEAD_DIM))
        outs[4].append(vs.reshape(NB, NT, N_HEADS, HEAD_DIM))
        outs[5].append(sv.reshape(NB, NT, N_GROUPS, GROUP_DIM))
        outs[6].append(cs[:, SUBLANES - (CONV_W - 1):])
    stacked = [jnp.stack(o) for o in outs]
    return (xp, xs.reshape(NB, NT, D_MODEL), *stacked)
```

```python
import functools

import jax
import jax.numpy as jnp
from jax import lax
from jax.experimental import pallas as pl
from jax.experimental.pallas import tpu as pltpu

F32 = jnp.float32
BF16 = jnp.bfloat16

D_MODEL = 1024
CHUNK = 64
ATTN_REACH = 512
N_HEADS = 8
HEAD_DIM = 64
D_A = N_HEADS * HEAD_DIM
REL_CLIP = 128
MLP_CHUNK = 128
N_GROUPS = 4
D_B = 512
GROUP_DIM = D_B // N_GROUPS
D_FF = 2816
CONV_W = 3
EPS = 1e-6

LANES = 128
SUBLANES = 8
MXU_DIM = 256
N_PAIRS = N_HEADS // 2
Q_BLOCK = 2 * CHUNK
KV_WINDOW = ATTN_REACH + Q_BLOCK
REL_SPAN = KV_WINDOW + Q_BLOCK
TILE_T = 256
HIST = ATTN_REACH
FF_CHUNKS = ((0, 1024), (1024, 2048), (2048, D_FF))
SAMPLE_BG = 4
FF_BLOCK = 256
VMEM_LIMIT = 56 * 1024 * 1024

_Q0, _K0, _V0, _B0, _G0 = 0, D_A, 2 * D_A, 3 * D_A, 3 * D_A + 2 * D_B
D_IN = _G0 + 2 * D_MODEL


def _dot(a, b):
    return jnp.dot(a, b, preferred_element_type=F32)


def _dot_nt(a, b):
    return lax.dot_general(a, b, (((1,), (1,)), ((), ())), preferred_element_type=F32)


def _dot_tn(a, b):
    return lax.dot_general(a, b, (((0,), (0,)), ((), ())), preferred_element_type=F32)


def _rms(x, g):
    return (x * lax.rsqrt(jnp.mean(x * x, axis=-1, keepdims=True) + EPS)) * g


def _head_rms(a, g, e_ref):
    sq = a * a
    hi = sq.astype(BF16)
    lo = (sq - hi.astype(F32)).astype(BF16)
    e = e_ref[...]
    w = e.shape[0]
    ms = jnp.concatenate([_dot(hi[:, c:c + w], e) + _dot(lo[:, c:c + w], e) for c in range(0, a.shape[1], w)],
                         axis=1)
    return (a * lax.rsqrt(ms + EPS)) * g


def _rel_rows(rel_ref, h, n_rows):
    wb = jnp.broadcast_to(rel_ref[h:h + 1, :], (n_rows, REL_SPAN))
    return pltpu.roll(wb, 0, axis=1, stride=1, stride_axis=0)


def _const_spec(shape):
    n = len(shape)
    return pl.BlockSpec(shape, lambda *_: (0,) * n, pipeline_mode=pl.Buffered(1))


def _layer_spec(l, shape):
    n = len(shape)
    return pl.BlockSpec((None, *shape), lambda *_: (l,) + (0,) * n, pipeline_mode=pl.Buffered(1))


def _ada_kernel(c_ref, w_ref, b_ref, o_ref):
    s = jax.nn.silu(c_ref[...]).astype(BF16)
    o_ref[...] = _dot(s, w_ref[...].astype(BF16)) + b_ref[...]


def _ada_call(c_all, w_ada, b_ada):
    depth = w_ada.shape[0]
    rows = c_all.shape[0]
    n_out = w_ada.shape[2]
    bn = 1536
    return pl.pallas_call(
        _ada_kernel,
        grid=(depth, n_out // bn),
        in_specs=[
            pl.BlockSpec((rows, D_MODEL), lambda l, n: (0, 0)),
            pl.BlockSpec((None, D_MODEL, bn), lambda l, n: (l, 0, n)),
            pl.BlockSpec((None, 1, bn), lambda l, n: (l, 0, n)),
        ],
        out_specs=pl.BlockSpec((None, rows, bn), lambda l, n: (l, 0, n)),
        out_shape=jax.ShapeDtypeStruct((depth, rows, n_out), F32),
        compiler_params=pltpu.CompilerParams(
            dimension_semantics=("arbitrary", "arbitrary"), vmem_limit_bytes=VMEM_LIMIT),
        name="ada_mod",
    )(c_all, w_ada, b_ada.reshape(depth, 1, n_out))


def _mixer_prompt_kernel(x_ref, mod_ref, n1g_ref, win_ref, qg_ref, kg_ref, e_ref, vg_ref, rel_ref,
                         wsp_ref, bsp_ref, woa_ref, wob_ref, wo_ref,
                         xo_ref, ks_ref, vs_ref,
                         kbuf, vtbuf, oat_scr, ob_scr, bias_scr):
    b = pl.program_id(0)
    t = pl.program_id(1)
    T = TILE_T

    @pl.when((b == 0) & (t == 0))
    def _():
        qi = lax.broadcasted_iota(jnp.int32, (Q_BLOCK, KV_WINDOW), 0) // CHUNK
        ki = lax.broadcasted_iota(jnp.int32, (Q_BLOCK, KV_WINDOW), 1) // CHUNK
        band = (ki >= qi) & (ki <= qi + ATTN_REACH // CHUNK)
        for h in range(N_HEADS):
            tab = jnp.where(band, _rel_rows(rel_ref, h, Q_BLOCK)[:, 0:KV_WINDOW], -jnp.inf)
            bias_scr[h // 2, :, (h % 2) * Q_BLOCK:(h % 2 + 1) * Q_BLOCK] = tab.T

    @pl.when(t == 0)
    def _():
        kbuf[0:HIST, :] = jnp.zeros((HIST, D_A), BF16)
        vtbuf[:, 0:HIST] = jnp.zeros((D_A, HIST), BF16)

    @pl.when(t > 0)
    def _():
        for r in range(0, HIST, T):
            kbuf[r:r + T, :] = kbuf[r + T:r + 2 * T, :]
            vtbuf[:, r:r + T] = vtbuf[:, r + T:r + 2 * T]

    x = x_ref[...]
    mod = mod_ref[...]
    sh1, sc1, gt1 = mod[0:1], mod[1:2], mod[2:3]
    h = (_rms(x, n1g_ref[...]) * (1.0 + sc1) + sh1).astype(BF16)

    q = _head_rms(_dot(h, win_ref[:, _Q0:_Q0 + D_A]), qg_ref[...], e_ref)
    k = _head_rms(_dot(h, win_ref[:, _K0:_K0 + D_A]), kg_ref[...], e_ref)
    v = _dot(h, win_ref[:, _V0:_V0 + D_A])
    ks_ref[...] = k
    vs_ref[...] = v
    kbuf[HIST:HIST + T, :] = k.astype(BF16)
    vtbuf[:, HIST:HIST + T] = v.T.astype(BF16)
    qb = (q * (HEAD_DIM ** -0.5)).astype(BF16)

    lane = lax.broadcasted_iota(jnp.int32, (1, LANES), 1)
    low_half = lane < HEAD_DIM

    def attend(j, p):
        r0, c0 = j * Q_BLOCK, p * LANES
        first_valid = HIST - t * T - r0
        qp = qb[r0:r0 + Q_BLOCK, c0:c0 + LANES]
        q2 = jnp.concatenate([jnp.where(low_half, qp, jnp.zeros_like(qp)),
                              jnp.where(low_half, jnp.zeros_like(qp), qp)], axis=0)
        st = _dot_nt(kbuf[r0:r0 + KV_WINDOW, c0:c0 + LANES], q2) + bias_scr[p]
        st = jnp.concatenate(
            [st[r:r + Q_BLOCK] + jnp.where(first_valid > r, -jnp.inf, 0.0) for r in range(0, HIST, Q_BLOCK)]
            + [st[HIST:]], axis=0)
        m = jnp.max(st, axis=0, keepdims=True)
        e = jnp.exp(st - m)
        l = jnp.sum(e, axis=0, keepdims=True)
        ot = _dot(vtbuf[c0:c0 + LANES, r0:r0 + KV_WINDOW], e.astype(BF16)) / l
        oat_scr[c0:c0 + HEAD_DIM, r0:r0 + Q_BLOCK] = ot[0:HEAD_DIM, 0:Q_BLOCK]
        oat_scr[c0 + HEAD_DIM:c0 + LANES, r0:r0 + Q_BLOCK] = ot[HEAD_DIM:, Q_BLOCK:]

    side = {}

    def project(name, act, c0):
        side[name] = act(_dot(h, win_ref[:, c0:c0 + MXU_DIM]))

    side_work = ([functools.partial(project, ("b", i), jax.nn.gelu, _B0 + i * MXU_DIM)
                  for i in range(2 * D_B // MXU_DIM)]
                 + [functools.partial(project, ("g", i), jax.nn.sigmoid, _G0 + i * MXU_DIM)
                    for i in range(2 * D_MODEL // MXU_DIM)])
    blocks = [(j, p) for j in range(T // Q_BLOCK) for p in range(N_PAIRS)]
    for i, (j, p) in enumerate(blocks):
        attend(j, p)
        for work in side_work[len(side_work) * i // len(blocks):len(side_work) * (i + 1) // len(blocks)]:
            work()
    n_b, n_g = D_B // MXU_DIM, D_MODEL // MXU_DIM
    ub = jnp.concatenate([side["b", i] for i in range(n_b)], axis=1)
    vbn = _rms(jnp.concatenate([side["b", n_b + i] for i in range(n_b)], axis=1), vg_ref[...]).astype(BF16)
    ga = jnp.concatenate([side["g", i] for i in range(n_g)], axis=1)
    gb = jnp.concatenate([side["g", n_g + i] for i in range(n_g)], axis=1)
    row_i = lax.broadcasted_iota(jnp.int32, (MLP_CHUNK, MLP_CHUNK), 0)
    col_i = lax.broadcasted_iota(jnp.int32, (MLP_CHUNK, MLP_CHUNK), 1)
    for g in range(N_GROUPS):
        wc = jnp.where(row_i >= col_i, wsp_ref[g], 0.0).astype(BF16)
        g0 = g * GROUP_DIM
        for c in range(T // MLP_CHUNK):
            r0 = c * MLP_CHUNK
            mix = _dot(wc, vbn[r0:r0 + MLP_CHUNK, g0:g0 + GROUP_DIM]) + bsp_ref[g]
            ob_scr[r0:r0 + MLP_CHUNK, g0:g0 + GROUP_DIM] = ub[r0:r0 + MLP_CHUNK, g0:g0 + GROUP_DIM] * mix

    merged = (ga * _dot_tn(oat_scr[...].astype(BF16), woa_ref[...])
              + gb * _dot(ob_scr[...].astype(BF16), wob_ref[...]))
    xo_ref[...] = x + gt1 * _dot(merged.astype(BF16), wo_ref[...])


def _mixer_prompt_call(l, x, mod_all, mod_row0, p):
    B, S, _ = x.shape
    T = TILE_T
    n_t = S // T
    keep_t = ATTN_REACH // T
    tok = pl.BlockSpec((None, T, D_MODEL), lambda b, t: (b, t, 0))
    state = pl.BlockSpec((None, T, D_A), lambda b, t: (b, jnp.maximum(t - (n_t - keep_t), 0), 0))
    return pl.pallas_call(
        _mixer_prompt_kernel,
        grid=(B, n_t),
        in_specs=[
            tok,
            pl.BlockSpec((None, None, 6, D_MODEL), lambda b, t: (l, mod_row0 + b, 0, 0)),
            _layer_spec(l, (1, D_MODEL)),
            _layer_spec(l, (D_MODEL, D_IN)),
            _layer_spec(l, (1, D_A)),
            _layer_spec(l, (1, D_A)),
            _const_spec((MXU_DIM, MXU_DIM)),
            _layer_spec(l, (1, D_B)),
            _layer_spec(l, (N_HEADS, REL_SPAN)),
            _layer_spec(l, (N_GROUPS, MLP_CHUNK, MLP_CHUNK)),
            _layer_spec(l, (N_GROUPS, MLP_CHUNK, GROUP_DIM)),
            _layer_spec(l, (D_A, D_MODEL)),
            _layer_spec(l, (D_B, D_MODEL)),
            _layer_spec(l, (D_MODEL, D_MODEL)),
        ],
        out_specs=[tok, state, state],
        out_shape=[
            jax.ShapeDtypeStruct((B, S, D_MODEL), F32),
            jax.ShapeDtypeStruct((B, ATTN_REACH, D_A), F32),
            jax.ShapeDtypeStruct((B, ATTN_REACH, D_A), F32),
        ],
        scratch_shapes=[
            pltpu.VMEM((HIST + T, D_A), BF16),
            pltpu.VMEM((D_A, HIST + T), BF16),
            pltpu.VMEM((D_A, T), F32),
            pltpu.VMEM((T, D_B), F32),
            pltpu.VMEM((N_PAIRS, KV_WINDOW, 2 * Q_BLOCK), F32),
        ],
        compiler_params=pltpu.CompilerParams(
            dimension_semantics=("arbitrary", "arbitrary"), vmem_limit_bytes=VMEM_LIMIT),
        name="mixer_prompt",
    )(x, mod_all, p["n1g"], p["win"], p["qg"], p["kg"], p["e_mat"], p["vg"], p["rel"],
      p["wsp"], p["bsp_b"], p["woa"], p["wob"], p["wo"])


def _conv_gate(g, u, prev1, prev2, row, cw_ref, cb_ref, c0, c1):
    gm1 = jnp.where(row == 0, prev1, pltpu.roll(g, 1, axis=0))
    gm2 = jnp.where(row == 0, prev2, jnp.where(row == 1, prev1, pltpu.roll(g, 2, axis=0)))
    gc = cb_ref[:, c0:c1] + cw_ref[0:1, c0:c1] * gm2
    gc = gc + cw_ref[1:2, c0:c1] * gm1
    gc = gc + cw_ref[2:3, c0:c1] * g
    return jax.nn.gelu(gc) * u


def _ffn_prompt_kernel(x_ref, mod_ref, n2g_ref, wfi_ref, cw_ref, cb_ref, wfo_ref,
                       xo_ref, cs_ref, carry):
    t = pl.program_id(1)
    T = TILE_T

    @pl.when(t == 0)
    def _():
        carry[...] = jnp.zeros(carry.shape, F32)

    x = x_ref[...]
    mod = mod_ref[...]
    sh2, sc2, gt2 = mod[3:4], mod[4:5], mod[5:6]
    h2 = (_rms(x, n2g_ref[...]) * (1.0 + sc2) + sh2).astype(BF16)
    row = lax.broadcasted_iota(jnp.int32, (T, 1), 0)
    f = jnp.zeros((T, D_MODEL), F32)
    for c0, c1 in FF_CHUNKS:
        g = _dot(h2, wfi_ref[:, c0:c1])
        u = _dot(h2, wfi_ref[:, D_FF + c0:D_FF + c1])
        prev2 = carry[SUBLANES - 2:SUBLANES - 1, c0:c1]
        prev1 = carry[SUBLANES - 1:SUBLANES, c0:c1]
        act = _conv_gate(g, u, prev1, prev2, row, cw_ref, cb_ref, c0, c1)
        f = f + _dot(act.astype(BF16), wfo_ref[c0:c1, :])
        tail = g[T - SUBLANES:T, :]
        carry[:, c0:c1] = tail
        cs_ref[:, c0:c1] = tail
    xo_ref[...] = x + gt2 * f


def _ffn_prompt_call(l, x, mod_all, mod_row0, p):
    B, S, _ = x.shape
    T = TILE_T
    tok = pl.BlockSpec((None, T, D_MODEL), lambda b, t: (b, t, 0))
    return pl.pallas_call(
        _ffn_prompt_kernel,
        grid=(B, S // T),
        in_specs=[
            tok,
            pl.BlockSpec((None, None, 6, D_MODEL), lambda b, t: (l, mod_row0 + b, 0, 0)),
            _layer_spec(l, (1, D_MODEL)),
            _layer_spec(l, (D_MODEL, 2 * D_FF)),
            _layer_spec(l, (CONV_W, D_FF)),
            _layer_spec(l, (1, D_FF)),
            _layer_spec(l, (D_FF, D_MODEL)),
        ],
        out_specs=[tok, pl.BlockSpec((None, SUBLANES, D_FF), lambda b, t: (b, 0, 0))],
        out_shape=[
            jax.ShapeDtypeStruct((B, S, D_MODEL), F32),
            jax.ShapeDtypeStruct((B, SUBLANES, D_FF), F32),
        ],
        scratch_shapes=[pltpu.VMEM((SUBLANES, D_FF), F32)],
        compiler_params=pltpu.CompilerParams(
            dimension_semantics=("arbitrary", "arbitrary"), vmem_limit_bytes=VMEM_LIMIT),
        name="ffn_prompt",
    )(x, mod_all, p["n2g"], p["wfi"], p["cw"], p["cb"], p["wfo"])


def _mixer_sample_kernel(x_ref, mod_ref, n1g_ref, win_ref, qg_ref, kg_ref, e_ref, vg_ref, rel_ref,
                         kc_ref, vc_ref, wsp_ref, bsp_ref, woa_ref, wob_ref, wo_ref,
                         xo_ref, ks_ref, vs_ref, sv_ref,
                         q_scr, kn_scr, vn_scr, oa_scr, ub_scr, vb_scr, ob_scr, bias_scr,
                         *, n_batch, n_tok, cache_len):
    step = pl.program_id(0)
    n_rows = n_batch * n_tok

    def per_token(m):
        return jnp.broadcast_to(m, (n_batch, n_tok, m.shape[-1])).reshape(n_rows, m.shape[-1])

    def normed_input():
        mod = mod_ref[...]
        sh1, sc1 = per_token(mod[:, 0:1, :]), per_token(mod[:, 1:2, :])
        return (_rms(x_ref[...], n1g_ref[...]) * (1.0 + sc1) + sh1).astype(BF16)

    @pl.when(step == 0)
    def _():
        h = normed_input()
        q = _head_rms(_dot(h, win_ref[:, _Q0:_Q0 + D_A]), qg_ref[...], e_ref)
        k = _head_rms(_dot(h, win_ref[:, _K0:_K0 + D_A]), kg_ref[...], e_ref)
        v = _dot(h, win_ref[:, _V0:_V0 + D_A])
        ks_ref[...] = k
        vs_ref[...] = v
        kn_scr[...] = k.astype(BF16)
        vn_scr[...] = v.astype(BF16)
        q_scr[...] = (q * (HEAD_DIM ** -0.5)).astype(BF16)
        for hh in range(N_HEADS):
            tab = _rel_rows(rel_ref, hh, n_tok)[:, 0:cache_len + LANES]
            bias_scr[hh // 2, (hh % 2) * n_tok:(hh % 2 + 1) * n_tok, :] = tab

    lane = lax.broadcasted_iota(jnp.int32, (1, LANES), 1)
    low_half = lane < HEAD_DIM
    for i in range(SAMPLE_BG):
        r0 = pl.multiple_of((step * SAMPLE_BG + i) * n_tok, n_tok)
        for p in range(N_PAIRS):
            c0 = p * LANES
            qp = q_scr[pl.ds(r0, n_tok), c0:c0 + LANES]
            q2 = jnp.concatenate([jnp.where(low_half, qp, jnp.zeros_like(qp)),
                                  jnp.where(low_half, jnp.zeros_like(qp), qp)], axis=0)
            kc = kc_ref[i, :, c0:c0 + LANES].astype(BF16)
            vc = vc_ref[i, :, c0:c0 + LANES].astype(BF16)
            kn = kn_scr[pl.ds(r0, n_tok), c0:c0 + LANES]
            vn = vn_scr[pl.ds(r0, n_tok), c0:c0 + LANES]
            s_c = _dot_nt(q2, kc) + bias_scr[p, :, 0:cache_len]
            s_n = _dot_nt(q2, kn) + bias_scr[p, :, cache_len:cache_len + n_tok]
            m = jnp.maximum(jnp.max(s_c, axis=-1, keepdims=True), jnp.max(s_n, axis=-1, keepdims=True))
            e_c = jnp.exp(s_c - m)
            e_n = jnp.exp(s_n - m)
            l = jnp.sum(e_c, axis=-1, keepdims=True) + jnp.sum(e_n, axis=-1, keepdims=True)
            o2 = (_dot(e_c.astype(BF16), vc) + _dot(e_n.astype(BF16), vn)) / l
            oa_scr[pl.ds(r0, n_tok), c0:c0 + LANES] = jnp.where(low_half, o2[0:n_tok], o2[n_tok:])

    @pl.when(step == pl.num_programs(0) - 1)
    def _():
        h = normed_input()
        zb = jax.nn.gelu(_dot(h, win_ref[:, _B0:_B0 + 2 * D_B]))
        vbn = _rms(zb[:, D_B:2 * D_B], vg_ref[...])
        sv_ref[...] = vbn
        for g in range(N_GROUPS):
            ub_scr[g] = zb[:, g * GROUP_DIM:(g + 1) * GROUP_DIM]
            vb_scr[g] = vbn[:, g * GROUP_DIM:(g + 1) * GROUP_DIM]
        for g in range(N_GROUPS):
            for t in range(n_tok):
                mix = bsp_ref[g, t:t + 1, :]
                for s in range(t + 1):
                    mix = mix + wsp_ref[g, t, s:s + 1, :] * vb_scr[g, pl.ds(s, n_batch, stride=n_tok), :]
                ob_scr[g, pl.ds(t, n_batch, stride=n_tok), :] = ub_scr[g, pl.ds(t, n_batch, stride=n_tok), :] * mix
        ob = jnp.concatenate([ob_scr[g] for g in range(N_GROUPS)], axis=1)
        ga = jax.nn.sigmoid(_dot(h, win_ref[:, _G0:_G0 + D_MODEL]))
        gb = jax.nn.sigmoid(_dot(h, win_ref[:, _G0 + D_MODEL:_G0 + 2 * D_MODEL]))
        merged = ga * _dot(oa_scr[...].astype(BF16), woa_ref[...]) + gb * _dot(ob.astype(BF16), wob_ref[...])
        gt1 = per_token(mod_ref[...][:, 2:3, :])
        xo_ref[...] = x_ref[...] + gt1 * _dot(merged.astype(BF16), wo_ref[...])


def _mixer_sample_call(l, x2, mod_all, kc_all, vc_all, p, n_batch, n_tok):
    n_rows = n_batch * n_tok
    cache_len = kc_all.shape[2]
    cache = pl.BlockSpec((None, SAMPLE_BG, cache_len, D_A), lambda s: (l, s, 0, 0))
    full = lambda shape: pl.BlockSpec(shape, lambda s: (0,) * len(shape))
    return pl.pallas_call(
        functools.partial(_mixer_sample_kernel, n_batch=n_batch, n_tok=n_tok, cache_len=cache_len),
        grid=(n_batch // SAMPLE_BG,),
        in_specs=[
            _const_spec((n_rows, D_MODEL)),
            pl.BlockSpec((None, n_batch, 6, D_MODEL), lambda s: (l, 0, 0, 0), pipeline_mode=pl.Buffered(1)),
            _layer_spec(l, (1, D_MODEL)),
            _layer_spec(l, (D_MODEL, D_IN)),
            _layer_spec(l, (1, D_A)),
            _layer_spec(l, (1, D_A)),
            _const_spec((MXU_DIM, MXU_DIM)),
            _layer_spec(l, (1, D_B)),
            _layer_spec(l, (N_HEADS, REL_SPAN)),
            cache,
            cache,
            _layer_spec(l, (N_GROUPS, n_tok, n_tok, GROUP_DIM)),
            _layer_spec(l, (N_GROUPS, n_tok, GROUP_DIM)),
            _layer_spec(l, (D_A, D_MODEL)),
            _layer_spec(l, (D_B, D_MODEL)),
            _layer_spec(l, (D_MODEL, D_MODEL)),
        ],
        out_specs=[full((n_rows, D_MODEL)), full((n_rows, D_A)), full((n_rows, D_A)), full((n_rows, D_B))],
        out_shape=[
            jax.ShapeDtypeStruct((n_rows, D_MODEL), F32),
            jax.ShapeDtypeStruct((n_rows, D_A), F32),
            jax.ShapeDtypeStruct((n_rows, D_A), F32),
            jax.ShapeDtypeStruct((n_rows, D_B), F32),
        ],
        scratch_shapes=[
            pltpu.VMEM((n_rows, D_A), BF16),
            pltpu.VMEM((n_rows, D_A), BF16),
            pltpu.VMEM((n_rows, D_A), BF16),
            pltpu.VMEM((n_rows, D_A), F32),
            pltpu.VMEM((N_GROUPS, n_rows, GROUP_DIM), F32),
            pltpu.VMEM((N_GROUPS, n_rows, GROUP_DIM), F32),
            pltpu.VMEM((N_GROUPS, n_rows, GROUP_DIM), F32),
            pltpu.VMEM((N_PAIRS, 2 * n_tok, cache_len + LANES), F32),
        ],
        compiler_params=pltpu.CompilerParams(
            dimension_semantics=("arbitrary",), vmem_limit_bytes=VMEM_LIMIT),
        name="mixer_sample",
    )(x2, mod_all, p["n1g"], p["win"], p["qg"], p["kg"], p["e_mat"], p["vg"], p["rel"],
      kc_all, vc_all, p["wsp_s"], p["bsp_s"], p["woa"], p["wob"], p["wo"])


def _ffn_sample_kernel(x_ref, mod_ref, n2g_ref, wg_ref, wu_ref, cw_ref, cb_ref, cc_ref, wfo_ref,
                       xo_ref, cs_ref, h_scr, acc, *, n_batch, n_tok):
    step = pl.program_id(0)
    n_rows = n_batch * n_tok

    def per_token(m):
        return jnp.broadcast_to(m, (n_batch, n_tok, m.shape[-1])).reshape(n_rows, m.shape[-1])

    @pl.when(step == 0)
    def _():
        mod = mod_ref[...]
        sh2, sc2 = per_token(mod[:, 3:4, :]), per_token(mod[:, 4:5, :])
        h_scr[...] = (_rms(x_ref[...], n2g_ref[...]) * (1.0 + sc2) + sh2).astype(BF16)
        acc[...] = jnp.zeros(acc.shape, F32)

    h2 = h_scr[...]
    g = _dot(h2, wg_ref[...])
    u = _dot(h2, wu_ref[...])
    cc = cc_ref[...]
    prev2, prev1 = per_token(cc[:, 0:1, :]), per_token(cc[:, 1:2, :])
    row = lax.broadcasted_iota(jnp.int32, (n_rows, 1), 0) % n_tok
    act = _conv_gate(g, u, prev1, prev2, row, cw_ref, cb_ref, 0, FF_BLOCK)
    acc[...] += _dot(act.astype(BF16), wfo_ref[...])
    cs_ref[...] = g.reshape(n_batch, n_tok, FF_BLOCK)[:, n_tok - SUBLANES:, :]

    @pl.when(step == pl.num_programs(0) - 1)
    def _():
        gt2 = per_token(mod_ref[...][:, 5:6, :])
        xo_ref[...] = x_ref[...] + gt2 * acc[...]


def _ffn_sample_call(l, x2, mod_all, cc_all, p, n_batch, n_tok):
    n_rows = n_batch * n_tok
    n_blk = D_FF // FF_BLOCK
    return pl.pallas_call(
        functools.partial(_ffn_sample_kernel, n_batch=n_batch, n_tok=n_tok),
        grid=(n_blk,),
        in_specs=[
            _const_spec((n_rows, D_MODEL)),
            pl.BlockSpec((None, n_batch, 6, D_MODEL), lambda c: (l, 0, 0, 0), pipeline_mode=pl.Buffered(1)),
            _layer_spec(l, (1, D_MODEL)),
            pl.BlockSpec((None, D_MODEL, FF_BLOCK), lambda c: (l, 0, c)),
            pl.BlockSpec((None, D_MODEL, FF_BLOCK), lambda c: (l, 0, n_blk + c)),
            pl.BlockSpec((None, CONV_W, FF_BLOCK), lambda c: (l, 0, c)),
            pl.BlockSpec((None, 1, FF_BLOCK), lambda c: (l, 0, c)),
            pl.BlockSpec((None, n_batch, CONV_W - 1, FF_BLOCK), lambda c: (l, 0, 0, c)),
            pl.BlockSpec((None, FF_BLOCK, D_MODEL), lambda c: (l, c, 0)),
        ],
        out_specs=[
            pl.BlockSpec((n_rows, D_MODEL), lambda c: (0, 0)),
            pl.BlockSpec((n_batch, SUBLANES, FF_BLOCK), lambda c: (0, 0, c)),
        ],
        out_shape=[
            jax.ShapeDtypeStruct((n_rows, D_MODEL), F32),
            jax.ShapeDtypeStruct((n_batch, SUBLANES, D_FF), F32),
        ],
        scratch_shapes=[pltpu.VMEM((n_rows, D_MODEL), BF16), pltpu.VMEM((n_rows, D_MODEL), F32)],
        compiler_params=pltpu.CompilerParams(
            dimension_semantics=("arbitrary",), vmem_limit_bytes=VMEM_LIMIT),
        name="ffn_sample",
    )(x2, mod_all, p["n2g"], p["wfi"], p["wfi"], p["cw"], p["cb"], cc_all, p["wfo"])


def _rel_vector(rel_bias):
    far = rel_bias[..., 2 * REL_CLIP:]
    lead = ATTN_REACH - REL_CLIP
    return jnp.concatenate([
        jnp.broadcast_to(far, (*rel_bias.shape[:-1], lead)),
        rel_bias[..., ::-1],
        jnp.broadcast_to(far, (*rel_bias.shape[:-1], REL_SPAN - lead - (2 * REL_CLIP + 1))),
    ], axis=-1)


def kernel(x_prompt, x_sample, cache_attn_k, cache_attn_v, cache_ffn_conv, c_prompt, c_sample, norm1_g, norm2_g, w_ada, b_ada, w_in, q_norm_g, k_norm_g, rel_bias, v_norm_g, w_spatial, b_spatial, w_out_a, w_out_b, w_out, w_ffn_in, ffn_conv_w, ffn_conv_b, w_ffn_out):
    depth = w_in.shape[0]
    B, S, _ = x_prompt.shape
    NB, NT, _ = x_sample.shape
    cache_len = cache_attn_k.shape[2]
    assert S % TILE_T == 0 and TILE_T % Q_BLOCK == 0 and HIST % TILE_T == 0 and S >= ATTN_REACH
    assert NB % SAMPLE_BG == 0 and NT % SUBLANES == 0 and NT <= MLP_CHUNK and NT <= LANES
    assert cache_len == ATTN_REACH and rel_bias.shape[-1] == 2 * REL_CLIP + 1

    rows = NB + B
    rows_pad = -(-rows // SUBLANES) * SUBLANES
    c_all = jnp.concatenate([c_sample, c_prompt, jnp.zeros((rows_pad - rows, D_MODEL), F32)], axis=0)
    mod_all = _ada_call(c_all, w_ada, b_ada).reshape(depth, rows_pad, 6, D_MODEL)

    head_of = jnp.arange(MXU_DIM) // HEAD_DIM
    tril = jnp.tril(jnp.ones((NT, NT), F32))
    ws_small = w_spatial[:, :, :NT, :NT] * tril
    params = {
        "n1g": norm1_g[:, None, :], "n2g": norm2_g[:, None, :],
        "win": w_in.astype(BF16), "woa": w_out_a.astype(BF16), "wob": w_out_b.astype(BF16),
        "wo": w_out.astype(BF16), "wfi": w_ffn_in.astype(BF16), "wfo": w_ffn_out.astype(BF16),
        "qg": jnp.tile(q_norm_g, (1, N_HEADS))[:, None, :], "kg": jnp.tile(k_norm_g, (1, N_HEADS))[:, None, :],
        "vg": v_norm_g[:, None, :],
        "e_mat": jnp.where(head_of[:, None] == head_of[None, :], 1.0 / HEAD_DIM, 0.0).astype(BF16),
        "rel": _rel_vector(rel_bias),
        "wsp": w_spatial,
        "bsp_b": jnp.broadcast_to(b_spatial[:, :, :, None], (depth, N_GROUPS, MLP_CHUNK, GROUP_DIM)),
        "wsp_s": jnp.broadcast_to(ws_small[..., None], (depth, N_GROUPS, NT, NT, GROUP_DIM)),
        "bsp_s": jnp.broadcast_to(b_spatial[:, :, :NT, None], (depth, N_GROUPS, NT, GROUP_DIM)),
        "cw": ffn_conv_w, "cb": ffn_conv_b[:, None, :],
    }
    kc_all = cache_attn_k.reshape(depth, NB, cache_len, D_A)
    vc_all = cache_attn_v.reshape(depth, NB, cache_len, D_A)

    xp = x_prompt
    xs = x_sample.reshape(NB * NT, D_MODEL)
    outs = [[] for _ in range(7)]
    for l in range(depth):
        xp, kp, vp = _mixer_prompt_call(l, xp, mod_all, NB, params)
        xp, cp = _ffn_prompt_call(l, xp, mod_all, NB, params)
        xs, ks, vs, sv = _mixer_sample_call(l, xs, mod_all, kc_all, vc_all, params, NB, NT)
        xs, cs = _ffn_sample_call(l, xs, mod_all, cache_ffn_conv, params, NB, NT)

        outs[0].append(kp.reshape(B, ATTN_REACH, N_HEADS, HEAD_DIM))
        outs[1].append(vp.reshape(B, ATTN_REACH, N_HEADS, HEAD_DIM))
        outs[2].append(cp[:, SUBLANES - (CONV_W - 1):])
        outs[3].append(ks.reshape(NB, NT, N_HEADS, HEAD_DIM))
        outs[4].append(vs.reshape(NB, NT, N_HEADS, HEAD_DIM))
        outs[5].append(sv.reshape(NB, NT, N_GROUPS, GROUP_DIM))
        outs[6].append(cs[:, SUBLANES - (CONV_W - 1):])
    stacked = [jnp.stack(o) for o in outs]
    return (xp, xs.reshape(NB, NT, D_MODEL), *stacked)
```

```python
import functools

import jax
import jax.numpy as jnp
from jax import lax
from jax.experimental import pallas as pl
from jax.experimental.pallas import tpu as pltpu

F32 = jnp.float32
BF16 = jnp.bfloat16

D_MODEL = 1024
CHUNK = 64
ATTN_REACH = 512
N_HEADS = 8
HEAD_DIM = 64
D_A = N_HEADS * HEAD_DIM
REL_CLIP = 128
MLP_CHUNK = 128
N_GROUPS = 4
D_B = 512
GROUP_DIM = D_B // N_GROUPS
D_FF = 2816
CONV_W = 3
EPS = 1e-6

LANES = 128
SUBLANES = 8
MXU_DIM = 256
N_PAIRS = N_HEADS // 2
Q_BLOCK = 2 * CHUNK
KV_WINDOW = ATTN_REACH + Q_BLOCK
REL_SPAN = KV_WINDOW + Q_BLOCK
TILE_T = 256
FFN_TILE_T = 512
HIST = ATTN_REACH
FF_CHUNK = 1024
FF_CHUNKS = tuple((c, min(c + FF_CHUNK, D_FF)) for c in range(0, D_FF, FF_CHUNK))
SAMPLE_BG = 4
FF_BLOCK = 256
VMEM_LIMIT = 56 * 1024 * 1024

_Q0, _K0, _V0, _B0, _G0 = 0, D_A, 2 * D_A, 3 * D_A, 3 * D_A + 2 * D_B
D_IN = _G0 + 2 * D_MODEL


def _dot(a, b):
    return jnp.dot(a, b, preferred_element_type=F32)


def _dot_nt(a, b):
    return lax.dot_general(a, b, (((1,), (1,)), ((), ())), preferred_element_type=F32)


def _dot_tn(a, b):
    return lax.dot_general(a, b, (((0,), (0,)), ((), ())), preferred_element_type=F32)


def _rms(x, g):
    return (x * lax.rsqrt(jnp.mean(x * x, axis=-1, keepdims=True) + EPS)) * g


def _head_rms(a, g, e_ref):
    sq = a * a
    hi = sq.astype(BF16)
    lo = (sq - hi.astype(F32)).astype(BF16)
    e = e_ref[...]
    w = e.shape[0]
    ms = jnp.concatenate([_dot(hi[:, c:c + w], e) + _dot(lo[:, c:c + w], e) for c in range(0, a.shape[1], w)],
                         axis=1)
    return (a * lax.rsqrt(ms + EPS)) * g


def _rel_rows(rel_ref, h, n_rows):
    wb = jnp.broadcast_to(rel_ref[h:h + 1, :], (n_rows, REL_SPAN))
    return pltpu.roll(wb, 0, axis=1, stride=1, stride_axis=0)


def _const_spec(shape):
    n = len(shape)
    return pl.BlockSpec(shape, lambda *_: (0,) * n, pipeline_mode=pl.Buffered(1))


def _layer_spec(l, shape):
    n = len(shape)
    return pl.BlockSpec((None, *shape), lambda *_: (l,) + (0,) * n, pipeline_mode=pl.Buffered(1))


def _ada_kernel(c_ref, w_ref, b_ref, o_ref):
    s = jax.nn.silu(c_ref[...]).astype(BF16)
    o_ref[...] = _dot(s, w_ref[...].astype(BF16)) + b_ref[...]


def _ada_call(c_all, w_ada, b_ada):
    depth = w_ada.shape[0]
    rows = c_all.shape[0]
    n_out = w_ada.shape[2]
    bn = 1536
    return pl.pallas_call(
        _ada_kernel,
        grid=(depth, n_out // bn),
        in_specs=[
            pl.BlockSpec((rows, D_MODEL), lambda l, n: (0, 0)),
            pl.BlockSpec((None, D_MODEL, bn), lambda l, n: (l, 0, n)),
            pl.BlockSpec((None, 1, bn), lambda l, n: (l, 0, n)),
        ],
        out_specs=pl.BlockSpec((None, rows, bn), lambda l, n: (l, 0, n)),
        out_shape=jax.ShapeDtypeStruct((depth, rows, n_out), F32),
        compiler_params=pltpu.CompilerParams(
            dimension_semantics=("arbitrary", "arbitrary"), vmem_limit_bytes=VMEM_LIMIT),
        name="ada_mod",
    )(c_all, w_ada, b_ada.reshape(depth, 1, n_out))


def _mixer_prompt_kernel(x_ref, mod_ref, n1g_ref, win_ref, qg_ref, kg_ref, e_ref, vg_ref, rel_ref,
                         wsp_ref, bsp_ref, woa_ref, wob_ref, wo_ref,
                         xo_ref, ks_ref, vs_ref,
                         kbuf, vtbuf, oat_scr, ob_scr, bias_scr, st_scr, e_scr):
    b = pl.program_id(0)
    t = pl.program_id(1)
    T = TILE_T

    @pl.when((b == 0) & (t == 0))
    def _():
        qi = lax.broadcasted_iota(jnp.int32, (Q_BLOCK, KV_WINDOW), 0) // CHUNK
        ki = lax.broadcasted_iota(jnp.int32, (Q_BLOCK, KV_WINDOW), 1) // CHUNK
        band = (ki >= qi) & (ki <= qi + ATTN_REACH // CHUNK)
        for h in range(N_HEADS):
            tab = jnp.where(band, _rel_rows(rel_ref, h, Q_BLOCK)[:, 0:KV_WINDOW], -jnp.inf)
            bias_scr[h // 2, 0:KV_WINDOW, (h % 2) * Q_BLOCK:(h % 2 + 1) * Q_BLOCK] = tab.T
        bias_scr[:, KV_WINDOW:, :] = jnp.full((N_PAIRS, Q_BLOCK, 2 * Q_BLOCK), -jnp.inf, F32)

    @pl.when(t == 0)
    def _():
        kbuf[0:HIST, :] = jnp.zeros((HIST, D_A), BF16)
        vtbuf[:, 0:HIST] = jnp.zeros((D_A, HIST), BF16)

    @pl.when(t > 0)
    def _():
        for r in range(0, HIST, T):
            kbuf[r:r + T, :] = kbuf[r + T:r + 2 * T, :]
            vtbuf[:, r:r + T] = vtbuf[:, r + T:r + 2 * T]

    x = x_ref[...]
    mod = mod_ref[...]
    sh1, sc1, gt1 = mod[0:1], mod[1:2], mod[2:3]
    h = (_rms(x, n1g_ref[...]) * (1.0 + sc1) + sh1).astype(BF16)

    q = _head_rms(_dot(h, win_ref[:, _Q0:_Q0 + D_A]), qg_ref[...], e_ref)
    k = _head_rms(_dot(h, win_ref[:, _K0:_K0 + D_A]), kg_ref[...], e_ref)
    v = _dot(h, win_ref[:, _V0:_V0 + D_A])
    ks_ref[...] = k
    vs_ref[...] = v
    kbuf[HIST:HIST + T, :] = k.astype(BF16)
    vtbuf[:, HIST:HIST + T] = v.T.astype(BF16)
    qb = (q * (HEAD_DIM ** -0.5)).astype(BF16)

    lane = lax.broadcasted_iota(jnp.int32, (1, LANES), 1)
    low_half = lane < HEAD_DIM

    n_qb = T // Q_BLOCK

    def scores(p, slot):
        c0 = p * LANES
        maxima = []
        for j in range(n_qb):
            r0 = j * Q_BLOCK
            qp = qb[r0:r0 + Q_BLOCK, c0:c0 + LANES]
            q2 = jnp.concatenate([jnp.where(low_half, qp, jnp.zeros_like(qp)),
                                  jnp.where(low_half, jnp.zeros_like(qp), qp)], axis=0)
            st = _dot_nt(kbuf[r0:r0 + KV_WINDOW, c0:c0 + LANES], q2)
            first_valid = HIST - t * T - r0
            m = None
            for r in range(0, KV_WINDOW, Q_BLOCK):
                src = jnp.where(first_valid > r, KV_WINDOW, r) if r < HIST else r
                blk = st[r:r + Q_BLOCK] + bias_scr[p, pl.ds(pl.multiple_of(src, Q_BLOCK), Q_BLOCK), :]
                st_scr[slot, j, r:r + Q_BLOCK, :] = blk
                bm = jnp.max(blk, axis=0, keepdims=True)
                m = bm if m is None else jnp.maximum(m, bm)
            maxima.append(m)
        return maxima

    def weights(slot, maxima):
        sums = []
        for j in range(n_qb):
            e = jnp.exp(st_scr[slot, j] - maxima[j])
            e_scr[slot, j] = e.astype(BF16)
            sums.append(jnp.sum(e, axis=0, keepdims=True))
        return sums

    def outputs(p, slot, sums):
        c0 = p * LANES
        for j in range(n_qb):
            r0 = j * Q_BLOCK
            ot = _dot(vtbuf[c0:c0 + LANES, r0:r0 + KV_WINDOW], e_scr[slot, j]) / sums[j]
            oat_scr[c0:c0 + HEAD_DIM, r0:r0 + Q_BLOCK] = ot[0:HEAD_DIM, 0:Q_BLOCK]
            oat_scr[c0 + HEAD_DIM:c0 + LANES, r0:r0 + Q_BLOCK] = ot[HEAD_DIM:, Q_BLOCK:]

    side = {}

    def project(name, act, c0):
        side[name] = act(_dot(h, win_ref[:, c0:c0 + MXU_DIM]))

    side_work = ([functools.partial(project, ("b", i), jax.nn.gelu, _B0 + i * MXU_DIM)
                  for i in range(2 * D_B // MXU_DIM)]
                 + [functools.partial(project, ("g", i), jax.nn.sigmoid, _G0 + i * MXU_DIM)
                    for i in range(2 * D_MODEL // MXU_DIM)])
    maxima = scores(0, 0)
    for p in range(N_PAIRS):
        slot = p % 2
        if p + 1 < N_PAIRS:
            next_maxima = scores(p + 1, 1 - slot)
        for work in side_work[len(side_work) * p // N_PAIRS:len(side_work) * (p + 1) // N_PAIRS]:
            work()
        outputs(p, slot, weights(slot, maxima))
        maxima = next_maxima
    n_b, n_g = D_B // MXU_DIM, D_MODEL // MXU_DIM
    ub = jnp.concatenate([side["b", i] for i in range(n_b)], axis=1)
    vbn = _rms(jnp.concatenate([side["b", n_b + i] for i in range(n_b)], axis=1), vg_ref[...]).astype(BF16)
    ga = jnp.concatenate([side["g", i] for i in range(n_g)], axis=1)
    gb = jnp.concatenate([side["g", n_g + i] for i in range(n_g)], axis=1)
    row_i = lax.broadcasted_iota(jnp.int32, (MLP_CHUNK, MLP_CHUNK), 0)
    col_i = lax.broadcasted_iota(jnp.int32, (MLP_CHUNK, MLP_CHUNK), 1)
    for g in range(N_GROUPS):
        wc = jnp.where(row_i >= col_i, wsp_ref[g], 0.0).astype(BF16)
        g0 = g * GROUP_DIM
        for c in range(T // MLP_CHUNK):
            r0 = c * MLP_CHUNK
            mix = _dot(wc, vbn[r0:r0 + MLP_CHUNK, g0:g0 + GROUP_DIM]) + bsp_ref[g]
            ob_scr[r0:r0 + MLP_CHUNK, g0:g0 + GROUP_DIM] = ub[r0:r0 + MLP_CHUNK, g0:g0 + GROUP_DIM] * mix

    merged = (ga * _dot_tn(oat_scr[...].astype(BF16), woa_ref[...])
              + gb * _dot(ob_scr[...].astype(BF16), wob_ref[...]))
    xo_ref[...] = x + gt1 * _dot(merged.astype(BF16), wo_ref[...])


def _mixer_prompt_call(l, x, mod_all, mod_row0, p):
    B, S, _ = x.shape
    T = TILE_T
    n_t = S // T
    keep_t = ATTN_REACH // T
    tok = pl.BlockSpec((None, T, D_MODEL), lambda b, t: (b, t, 0))
    state = pl.BlockSpec((None, T, D_A), lambda b, t: (b, jnp.maximum(t - (n_t - keep_t), 0), 0))
    return pl.pallas_call(
        _mixer_prompt_kernel,
        grid=(B, n_t),
        in_specs=[
            tok,
            pl.BlockSpec((None, None, 6, D_MODEL), lambda b, t: (l, mod_row0 + b, 0, 0)),
            _layer_spec(l, (1, D_MODEL)),
            _layer_spec(l, (D_MODEL, D_IN)),
            _layer_spec(l, (1, D_A)),
            _layer_spec(l, (1, D_A)),
            _const_spec((MXU_DIM, MXU_DIM)),
            _layer_spec(l, (1, D_B)),
            _layer_spec(l, (N_HEADS, REL_SPAN)),
            _layer_spec(l, (N_GROUPS, MLP_CHUNK, MLP_CHUNK)),
            _layer_spec(l, (N_GROUPS, MLP_CHUNK, GROUP_DIM)),
            _layer_spec(l, (D_A, D_MODEL)),
            _layer_spec(l, (D_B, D_MODEL)),
            _layer_spec(l, (D_MODEL, D_MODEL)),
        ],
        out_specs=[tok, state, state],
        out_shape=[
            jax.ShapeDtypeStruct((B, S, D_MODEL), F32),
            jax.ShapeDtypeStruct((B, ATTN_REACH, D_A), F32),
            jax.ShapeDtypeStruct((B, ATTN_REACH, D_A), F32),
        ],
        scratch_shapes=[
            pltpu.VMEM((HIST + T, D_A), BF16),
            pltpu.VMEM((D_A, HIST + T), BF16),
            pltpu.VMEM((D_A, T), F32),
            pltpu.VMEM((T, D_B), F32),
            pltpu.VMEM((N_PAIRS, KV_WINDOW + Q_BLOCK, 2 * Q_BLOCK), F32),
            pltpu.VMEM((2, T // Q_BLOCK, KV_WINDOW, 2 * Q_BLOCK), F32),
            pltpu.VMEM((2, T // Q_BLOCK, KV_WINDOW, 2 * Q_BLOCK), BF16),
        ],
        compiler_params=pltpu.CompilerParams(
            dimension_semantics=("arbitrary", "arbitrary"), vmem_limit_bytes=VMEM_LIMIT),
        name="mixer_prompt",
    )(x, mod_all, p["n1g"], p["win"], p["qg"], p["kg"], p["e_mat"], p["vg"], p["rel"],
      p["wsp"], p["bsp_b"], p["woa"], p["wob"], p["wo"])


def _conv_gate(g, u, prev1, prev2, row, cw_ref, cb_ref, c0, c1):
    gm1 = jnp.where(row == 0, prev1, pltpu.roll(g, 1, axis=0))
    gm2 = jnp.where(row == 0, prev2, jnp.where(row == 1, prev1, pltpu.roll(g, 2, axis=0)))
    gc = cb_ref[:, c0:c1] + cw_ref[0:1, c0:c1] * gm2
    gc = gc + cw_ref[1:2, c0:c1] * gm1
    gc = gc + cw_ref[2:3, c0:c1] * g
    return jax.nn.gelu(gc) * u


def _ffn_prompt_kernel(x_ref, mod_ref, n2g_ref, wfi_ref, cw_ref, cb_ref, wfo_ref,
                       xo_ref, cs_ref, carry):
    t = pl.program_id(1)
    T = FFN_TILE_T

    @pl.when(t == 0)
    def _():
        carry[...] = jnp.zeros(carry.shape, F32)

    x = x_ref[...]
    mod = mod_ref[...]
    sh2, sc2, gt2 = mod[3:4], mod[4:5], mod[5:6]
    h2 = (_rms(x, n2g_ref[...]) * (1.0 + sc2) + sh2).astype(BF16)
    row = lax.broadcasted_iota(jnp.int32, (T, 1), 0)
    prev2 = carry[SUBLANES - 2:SUBLANES - 1, :]
    prev1 = carry[SUBLANES - 1:SUBLANES, :]

    def up_project(c0, c1):
        return _dot(h2, wfi_ref[:, c0:c1]), _dot(h2, wfi_ref[:, D_FF + c0:D_FF + c1])

    f = jnp.zeros((T, D_MODEL), F32)
    tails = []
    ahead = up_project(*FF_CHUNKS[0])
    for i, (c0, c1) in enumerate(FF_CHUNKS):
        g, u = ahead
        if i + 1 < len(FF_CHUNKS):
            ahead = up_project(*FF_CHUNKS[i + 1])
        act = _conv_gate(g, u, prev1[:, c0:c1], prev2[:, c0:c1], row, cw_ref, cb_ref, c0, c1)
        f = f + _dot(act.astype(BF16), wfo_ref[c0:c1, :])
        tails.append(g[T - SUBLANES:T, :])
    tail = jnp.concatenate(tails, axis=1)
    carry[...] = tail
    cs_ref[...] = tail
    xo_ref[...] = x + gt2 * f


def _ffn_prompt_call(l, x, mod_all, mod_row0, p):
    B, S, _ = x.shape
    T = FFN_TILE_T
    tok = pl.BlockSpec((None, T, D_MODEL), lambda b, t: (b, t, 0))
    return pl.pallas_call(
        _ffn_prompt_kernel,
        grid=(B, S // T),
        in_specs=[
            tok,
            pl.BlockSpec((None, None, 6, D_MODEL), lambda b, t: (l, mod_row0 + b, 0, 0)),
            _layer_spec(l, (1, D_MODEL)),
            _layer_spec(l, (D_MODEL, 2 * D_FF)),
            _layer_spec(l, (CONV_W, D_FF)),
            _layer_spec(l, (1, D_FF)),
            _layer_spec(l, (D_FF, D_MODEL)),
        ],
        out_specs=[tok, pl.BlockSpec((None, SUBLANES, D_FF), lambda b, t: (b, 0, 0))],
        out_shape=[
            jax.ShapeDtypeStruct((B, S, D_MODEL), F32),
            jax.ShapeDtypeStruct((B, SUBLANES, D_FF), F32),
        ],
        scratch_shapes=[pltpu.VMEM((SUBLANES, D_FF), F32)],
        compiler_params=pltpu.CompilerParams(
            dimension_semantics=("arbitrary", "arbitrary"), vmem_limit_bytes=VMEM_LIMIT),
        name="ffn_prompt",
    )(x, mod_all, p["n2g"], p["wfi"], p["cw"], p["cb"], p["wfo"])


def _mixer_sample_kernel(x_ref, mod_ref, n1g_ref, win_ref, qg_ref, kg_ref, e_ref, vg_ref, rel_ref,
                         kc_ref, vc_ref, wsp_ref, bsp_ref, woa_ref, wob_ref, wo_ref,
                         xo_ref, ks_ref, vs_ref, sv_ref,
                         q_scr, kn_scr, vn_scr, oa_scr, ub_scr, vb_scr, ob_scr, bias_scr,
                         *, n_batch, n_tok, cache_len):
    step = pl.program_id(0)
    n_rows = n_batch * n_tok

    def per_token(m):
        return jnp.broadcast_to(m, (n_batch, n_tok, m.shape[-1])).reshape(n_rows, m.shape[-1])

    def normed_input():
        mod = mod_ref[...]
        sh1, sc1 = per_token(mod[:, 0:1, :]), per_token(mod[:, 1:2, :])
        return (_rms(x_ref[...], n1g_ref[...]) * (1.0 + sc1) + sh1).astype(BF16)

    @pl.when(step == 0)
    def _():
        h = normed_input()
        q = _head_rms(_dot(h, win_ref[:, _Q0:_Q0 + D_A]), qg_ref[...], e_ref)
        k = _head_rms(_dot(h, win_ref[:, _K0:_K0 + D_A]), kg_ref[...], e_ref)
        v = _dot(h, win_ref[:, _V0:_V0 + D_A])
        ks_ref[...] = k
        vs_ref[...] = v
        kn_scr[...] = k.astype(BF16)
        vn_scr[...] = v.astype(BF16)
        q_scr[...] = (q * (HEAD_DIM ** -0.5)).astype(BF16)
        for hh in range(N_HEADS):
            tab = _rel_rows(rel_ref, hh, n_tok)[:, 0:cache_len + LANES]
            bias_scr[hh // 2, (hh % 2) * n_tok:(hh % 2 + 1) * n_tok, :] = tab

    lane = lax.broadcasted_iota(jnp.int32, (1, LANES), 1)
    low_half = lane < HEAD_DIM
    for i in range(SAMPLE_BG):
        r0 = pl.multiple_of((step * SAMPLE_BG + i) * n_tok, n_tok)
        for p in range(N_PAIRS):
            c0 = p * LANES
            qp = q_scr[pl.ds(r0, n_tok), c0:c0 + LANES]
            q2 = jnp.concatenate([jnp.where(low_half, qp, jnp.zeros_like(qp)),
                                  jnp.where(low_half, jnp.zeros_like(qp), qp)], axis=0)
            kc = kc_ref[i, :, c0:c0 + LANES].astype(BF16)
            vc = vc_ref[i, :, c0:c0 + LANES].astype(BF16)
            kn = kn_scr[pl.ds(r0, n_tok), c0:c0 + LANES]
            vn = vn_scr[pl.ds(r0, n_tok), c0:c0 + LANES]
            s_c = _dot_nt(q2, kc) + bias_scr[p, :, 0:cache_len]
            s_n = _dot_nt(q2, kn) + bias_scr[p, :, cache_len:cache_len + n_tok]
            m = jnp.maximum(jnp.max(s_c, axis=-1, keepdims=True), jnp.max(s_n, axis=-1, keepdims=True))
            e_c = jnp.exp(s_c - m)
            e_n = jnp.exp(s_n - m)
            l = jnp.sum(e_c, axis=-1, keepdims=True) + jnp.sum(e_n, axis=-1, keepdims=True)
            o2 = (_dot(e_c.astype(BF16), vc) + _dot(e_n.astype(BF16), vn)) / l
            oa_scr[pl.ds(r0, n_tok), c0:c0 + LANES] = jnp.where(low_half, o2[0:n_tok], o2[n_tok:])

    @pl.when(step == pl.num_programs(0) - 1)
    def _():
        h = normed_input()
        zb = jax.nn.gelu(_dot(h, win_ref[:, _B0:_B0 + 2 * D_B]))
        vbn = _rms(zb[:, D_B:2 * D_B], vg_ref[...])
        sv_ref[...] = vbn
        for g in range(N_GROUPS):
            ub_scr[g] = zb[:, g * GROUP_DIM:(g + 1) * GROUP_DIM]
            vb_scr[g] = vbn[:, g * GROUP_DIM:(g + 1) * GROUP_DIM]
        for g in range(N_GROUPS):
            for t in range(n_tok):
                mix = bsp_ref[g, t:t + 1, :]
                for s in range(t + 1):
                    mix = mix + wsp_ref[g, t, s:s + 1, :] * vb_scr[g, pl.ds(s, n_batch, stride=n_tok), :]
                ob_scr[g, pl.ds(t, n_batch, stride=n_tok), :] = ub_scr[g, pl.ds(t, n_batch, stride=n_tok), :] * mix
        ob = jnp.concatenate([ob_scr[g] for g in range(N_GROUPS)], axis=1)
        ga = jax.nn.sigmoid(_dot(h, win_ref[:, _G0:_G0 + D_MODEL]))
        gb = jax.nn.sigmoid(_dot(h, win_ref[:, _G0 + D_MODEL:_G0 + 2 * D_MODEL]))
        merged = ga * _dot(oa_scr[...].astype(BF16), woa_ref[...]) + gb * _dot(ob.astype(BF16), wob_ref[...])
        gt1 = per_token(mod_ref[...][:, 2:3, :])
        xo_ref[...] = x_ref[...] + gt1 * _dot(merged.astype(BF16), wo_ref[...])


def _mixer_sample_call(l, x2, mod_all, kc_all, vc_all, p, n_batch, n_tok):
    n_rows = n_batch * n_tok
    cache_len = kc_all.shape[2]
    cache = pl.BlockSpec((None, SAMPLE_BG, cache_len, D_A), lambda s: (l, s, 0, 0))
    full = lambda shape: pl.BlockSpec(shape, lambda s: (0,) * len(shape))
    return pl.pallas_call(
        functools.partial(_mixer_sample_kernel, n_batch=n_batch, n_tok=n_tok, cache_len=cache_len),
        grid=(n_batch // SAMPLE_BG,),
        in_specs=[
            _const_spec((n_rows, D_MODEL)),
            pl.BlockSpec((None, n_batch, 6, D_MODEL), lambda s: (l, 0, 0, 0), pipeline_mode=pl.Buffered(1)),
            _layer_spec(l, (1, D_MODEL)),
            _layer_spec(l, (D_MODEL, D_IN)),
            _layer_spec(l, (1, D_A)),
            _layer_spec(l, (1, D_A)),
            _const_spec((MXU_DIM, MXU_DIM)),
            _layer_spec(l, (1, D_B)),
            _layer_spec(l, (N_HEADS, REL_SPAN)),
            cache,
            cache,
            _layer_spec(l, (N_GROUPS, n_tok, n_tok, GROUP_DIM)),
            _layer_spec(l, (N_GROUPS, n_tok, GROUP_DIM)),
            _layer_spec(l, (D_A, D_MODEL)),
            _layer_spec(l, (D_B, D_MODEL)),
            _layer_spec(l, (D_MODEL, D_MODEL)),
        ],
        out_specs=[full((n_rows, D_MODEL)), full((n_rows, D_A)), full((n_rows, D_A)), full((n_rows, D_B))],
        out_shape=[
            jax.ShapeDtypeStruct((n_rows, D_MODEL), F32),
            jax.ShapeDtypeStruct((n_rows, D_A), F32),
            jax.ShapeDtypeStruct((n_rows, D_A), F32),
            jax.ShapeDtypeStruct((n_rows, D_B), F32),
        ],
        scratch_shapes=[
            pltpu.VMEM((n_rows, D_A), BF16),
            pltpu.VMEM((n_rows, D_A), BF16),
            pltpu.VMEM((n_rows, D_A), BF16),
            pltpu.VMEM((n_rows, D_A), F32),
            pltpu.VMEM((N_GROUPS, n_rows, GROUP_DIM), F32),
            pltpu.VMEM((N_GROUPS, n_rows, GROUP_DIM), F32),
            pltpu.VMEM((N_GROUPS, n_rows, GROUP_DIM), F32),
            pltpu.VMEM((N_PAIRS, 2 * n_tok, cache_len + LANES), F32),
        ],
        compiler_params=pltpu.CompilerParams(
            dimension_semantics=("arbitrary",), vmem_limit_bytes=VMEM_LIMIT),
        name="mixer_sample",
    )(x2, mod_all, p["n1g"], p["win"], p["qg"], p["kg"], p["e_mat"], p["vg"], p["rel"],
      kc_all, vc_all, p["wsp_s"], p["bsp_s"], p["woa"], p["wob"], p["wo"])


def _ffn_sample_kernel(x_ref, mod_ref, n2g_ref, wg_ref, wu_ref, cw_ref, cb_ref, cc_ref, wfo_ref,
                       xo_ref, cs_ref, h_scr, acc, *, n_batch, n_tok):
    step = pl.program_id(0)
    n_rows = n_batch * n_tok

    def per_token(m):
        return jnp.broadcast_to(m, (n_batch, n_tok, m.shape[-1])).reshape(n_rows, m.shape[-1])

    @pl.when(step == 0)
    def _():
        mod = mod_ref[...]
        sh2, sc2 = per_token(mod[:, 3:4, :]), per_token(mod[:, 4:5, :])
        h_scr[...] = (_rms(x_ref[...], n2g_ref[...]) * (1.0 + sc2) + sh2).astype(BF16)
        acc[...] = jnp.zeros(acc.shape, F32)

    h2 = h_scr[...]
    g = _dot(h2, wg_ref[...])
    u = _dot(h2, wu_ref[...])
    cc = cc_ref[...]
    prev2, prev1 = per_token(cc[:, 0:1, :]), per_token(cc[:, 1:2, :])
    row = lax.broadcasted_iota(jnp.int32, (n_rows, 1), 0) % n_tok
    act = _conv_gate(g, u, prev1, prev2, row, cw_ref, cb_ref, 0, FF_BLOCK)
    acc[...] += _dot(act.astype(BF16), wfo_ref[...])
    cs_ref[...] = g.reshape(n_batch, n_tok, FF_BLOCK)[:, n_tok - SUBLANES:, :]

    @pl.when(step == pl.num_programs(0) - 1)
    def _():
        gt2 = per_token(mod_ref[...][:, 5:6, :])
        xo_ref[...] = x_ref[...] + gt2 * acc[...]


def _ffn_sample_call(l, x2, mod_all, cc_all, p, n_batch, n_tok):
    n_rows = n_batch * n_tok
    n_blk = D_FF // FF_BLOCK
    return pl.pallas_call(
        functools.partial(_ffn_sample_kernel, n_batch=n_batch, n_tok=n_tok),
        grid=(n_blk,),
        in_specs=[
            _const_spec((n_rows, D_MODEL)),
            pl.BlockSpec((None, n_batch, 6, D_MODEL), lambda c: (l, 0, 0, 0), pipeline_mode=pl.Buffered(1)),
            _layer_spec(l, (1, D_MODEL)),
            pl.BlockSpec((None, D_MODEL, FF_BLOCK), lambda c: (l, 0, c)),
            pl.BlockSpec((None, D_MODEL, FF_BLOCK), lambda c: (l, 0, n_blk + c)),
            pl.BlockSpec((None, CONV_W, FF_BLOCK), lambda c: (l, 0, c)),
            pl.BlockSpec((None, 1, FF_BLOCK), lambda c: (l, 0, c)),
            pl.BlockSpec((None, n_batch, CONV_W - 1, FF_BLOCK), lambda c: (l, 0, 0, c)),
            pl.BlockSpec((None, FF_BLOCK, D_MODEL), lambda c: (l, c, 0)),
        ],
        out_specs=[
            pl.BlockSpec((n_rows, D_MODEL), lambda c: (0, 0)),
            pl.BlockSpec((n_batch, SUBLANES, FF_BLOCK), lambda c: (0, 0, c)),
        ],
        out_shape=[
            jax.ShapeDtypeStruct((n_rows, D_MODEL), F32),
            jax.ShapeDtypeStruct((n_batch, SUBLANES, D_FF), F32),
        ],
        scratch_shapes=[pltpu.VMEM((n_rows, D_MODEL), BF16), pltpu.VMEM((n_rows, D_MODEL), F32)],
        compiler_params=pltpu.CompilerParams(
            dimension_semantics=("arbitrary",), vmem_limit_bytes=VMEM_LIMIT),
        name="ffn_sample",
    )(x2, mod_all, p["n2g"], p["wfi"], p["wfi"], p["cw"], p["cb"], cc_all, p["wfo"])


def _rel_vector(rel_bias):
    far = rel_bias[..., 2 * REL_CLIP:]
    lead = ATTN_REACH - REL_CLIP
    return jnp.concatenate([
        jnp.broadcast_to(far, (*rel_bias.shape[:-1], lead)),
        rel_bias[..., ::-1],
        jnp.broadcast_to(far, (*rel_bias.shape[:-1], REL_SPAN - lead - (2 * REL_CLIP + 1))),
    ], axis=-1)


def kernel(x_prompt, x_sample, cache_attn_k, cache_attn_v, cache_ffn_conv, c_prompt, c_sample, norm1_g, norm2_g, w_ada, b_ada, w_in, q_norm_g, k_norm_g, rel_bias, v_norm_g, w_spatial, b_spatial, w_out_a, w_out_b, w_out, w_ffn_in, ffn_conv_w, ffn_conv_b, w_ffn_out):
    depth = w_in.shape[0]
    B, S, _ = x_prompt.shape
    NB, NT, _ = x_sample.shape
    cache_len = cache_attn_k.shape[2]
    assert S % FFN_TILE_T == 0 and S % TILE_T == 0 and TILE_T % Q_BLOCK == 0 and HIST % TILE_T == 0 and S >= ATTN_REACH
    assert NB % SAMPLE_BG == 0 and NT % SUBLANES == 0 and NT <= MLP_CHUNK and NT <= LANES
    assert cache_len == ATTN_REACH and rel_bias.shape[-1] == 2 * REL_CLIP + 1

    rows = NB + B
    rows_pad = -(-rows // SUBLANES) * SUBLANES
    c_all = jnp.concatenate([c_sample, c_prompt, jnp.zeros((rows_pad - rows, D_MODEL), F32)], axis=0)
    mod_all = _ada_call(c_all, w_ada, b_ada).reshape(depth, rows_pad, 6, D_MODEL)

    head_of = jnp.arange(MXU_DIM) // HEAD_DIM
    tril = jnp.tril(jnp.ones((NT, NT), F32))
    ws_small = w_spatial[:, :, :NT, :NT] * tril
    params = {
        "n1g": norm1_g[:, None, :], "n2g": norm2_g[:, None, :],
        "win": w_in.astype(BF16), "woa": w_out_a.astype(BF16), "wob": w_out_b.astype(BF16),
        "wo": w_out.astype(BF16), "wfi": w_ffn_in.astype(BF16), "wfo": w_ffn_out.astype(BF16),
        "qg": jnp.tile(q_norm_g, (1, N_HEADS))[:, None, :], "kg": jnp.tile(k_norm_g, (1, N_HEADS))[:, None, :],
        "vg": v_norm_g[:, None, :],
        "e_mat": jnp.where(head_of[:, None] == head_of[None, :], 1.0 / HEAD_DIM, 0.0).astype(BF16),
        "rel": _rel_vector(rel_bias),
        "wsp": w_spatial,
        "bsp_b": jnp.broadcast_to(b_spatial[:, :, :, None], (depth, N_GROUPS, MLP_CHUNK, GROUP_DIM)),
        "wsp_s": jnp.broadcast_to(ws_small[..., None], (depth, N_GROUPS, NT, NT, GROUP_DIM)),
        "bsp_s": jnp.broadcast_to(b_spatial[:, :, :NT, None], (depth, N_GROUPS, NT, GROUP_DIM)),
        "cw": ffn_conv_w, "cb": ffn_conv_b[:, None, :],
    }
    kc_all = cache_attn_k.reshape(depth, NB, cache_len, D_A)
    vc_all = cache_attn_v.reshape(depth, NB, cache_len, D_A)

    xp = x_prompt
    xs = x_sample.reshape(NB * NT, D_MODEL)
    outs = [[] for _ in range(7)]
    for l in range(depth):
        xp, kp, vp = _mixer_prompt_call(l, xp, mod_all, NB, params)
        xp, cp = _ffn_prompt_call(l, xp, mod_all, NB, params)
        xs, ks, vs, sv = _mixer_sample_call(l, xs, mod_all, kc_all, vc_all, params, NB, NT)
        xs, cs = _ffn_sample_call(l, xs, mod_all, cache_ffn_conv, params, NB, NT)

        outs[0].append(kp.reshape(B, ATTN_REACH, N_HEADS, HEAD_DIM))
        outs[1].append(vp.reshape(B, ATTN_REACH, N_HEADS, HEAD_DIM))
        outs[2].append(cp[:, SUBLANES - (CONV_W - 1):])
        outs[3].append(ks.reshape(NB, NT, N_HEADS, HEAD_DIM))
        outs[4].append(vs.reshape(NB, NT, N_HEADS, HEAD_DIM))
        outs[5].append(sv.reshape(NB, NT, N_GROUPS, GROUP_DIM))
        outs[6].append(cs[:, SUBLANES - (CONV_W - 1):])
    stacked = [jnp.stack(o) for o in outs]
    return (xp, xs.reshape(NB, NT, D_MODEL), *stacked)
```

```python
import functools

import jax
import jax.numpy as jnp
from jax import lax
from jax.experimental import pallas as pl
from jax.experimental.pallas import tpu as pltpu

F32 = jnp.float32
BF16 = jnp.bfloat16

D_MODEL = 1024
CHUNK = 64
ATTN_REACH = 512
N_HEADS = 8
HEAD_DIM = 64
D_A = N_HEADS * HEAD_DIM
REL_CLIP = 128
MLP_CHUNK = 128
N_GROUPS = 4
D_B = 512
GROUP_DIM = D_B // N_GROUPS
D_FF = 2816
CONV_W = 3
EPS = 1e-6

LANES = 128
SUBLANES = 8
MXU_DIM = 256
N_PAIRS = N_HEADS // 2
Q_BLOCK = 2 * CHUNK
KV_WINDOW = ATTN_REACH + Q_BLOCK
REL_SPAN = KV_WINDOW + Q_BLOCK
TILE_T = 512
FFN_TILE_T = 512
HIST = ATTN_REACH
FF_CHUNK = 1024
FF_CHUNKS = tuple((c, min(c + FF_CHUNK, D_FF)) for c in range(0, D_FF, FF_CHUNK))
SAMPLE_BG = 4
FF_BLOCK = 256
VMEM_LIMIT = 56 * 1024 * 1024

_Q0, _K0, _V0, _B0, _G0 = 0, D_A, 2 * D_A, 3 * D_A, 3 * D_A + 2 * D_B
D_IN = _G0 + 2 * D_MODEL


def _dot(a, b):
    return jnp.dot(a, b, preferred_element_type=F32)


def _dot_nt(a, b):
    return lax.dot_general(a, b, (((1,), (1,)), ((), ())), preferred_element_type=F32)


def _dot_tn(a, b):
    return lax.dot_general(a, b, (((0,), (0,)), ((), ())), preferred_element_type=F32)


def _rms(x, g):
    return (x * lax.rsqrt(jnp.mean(x * x, axis=-1, keepdims=True) + EPS)) * g


def _head_rms(a, g, e_ref):
    sq = a * a
    hi = sq.astype(BF16)
    lo = (sq - hi.astype(F32)).astype(BF16)
    e = e_ref[...]
    w = e.shape[0]
    ms = jnp.concatenate([_dot(hi[:, c:c + w], e) + _dot(lo[:, c:c + w], e) for c in range(0, a.shape[1], w)],
                         axis=1)
    return (a * lax.rsqrt(ms + EPS)) * g


def _rel_rows(rel_ref, h, n_rows):
    wb = jnp.broadcast_to(rel_ref[h:h + 1, :], (n_rows, REL_SPAN))
    return pltpu.roll(wb, 0, axis=1, stride=1, stride_axis=0)


def _const_spec(shape):
    n = len(shape)
    return pl.BlockSpec(shape, lambda *_: (0,) * n, pipeline_mode=pl.Buffered(1))


def _layer_spec(l, shape):
    n = len(shape)
    return pl.BlockSpec((None, *shape), lambda *_: (l,) + (0,) * n, pipeline_mode=pl.Buffered(1))


def _ada_kernel(c_ref, w_ref, b_ref, o_ref):
    s = jax.nn.silu(c_ref[...]).astype(BF16)
    o_ref[...] = _dot(s, w_ref[...].astype(BF16)) + b_ref[...]


def _ada_call(c_all, w_ada, b_ada):
    depth = w_ada.shape[0]
    rows = c_all.shape[0]
    n_out = w_ada.shape[2]
    bn = 1536
    return pl.pallas_call(
        _ada_kernel,
        grid=(depth, n_out // bn),
        in_specs=[
            pl.BlockSpec((rows, D_MODEL), lambda l, n: (0, 0)),
            pl.BlockSpec((None, D_MODEL, bn), lambda l, n: (l, 0, n)),
            pl.BlockSpec((None, 1, bn), lambda l, n: (l, 0, n)),
        ],
        out_specs=pl.BlockSpec((None, rows, bn), lambda l, n: (l, 0, n)),
        out_shape=jax.ShapeDtypeStruct((depth, rows, n_out), F32),
        compiler_params=pltpu.CompilerParams(
            dimension_semantics=("arbitrary", "arbitrary"), vmem_limit_bytes=VMEM_LIMIT),
        name="ada_mod",
    )(c_all, w_ada, b_ada.reshape(depth, 1, n_out))


def _mixer_prompt_kernel(x_ref, mod_ref, n1g_ref, win_ref, qg_ref, kg_ref, e_ref, vg_ref, rel_ref,
                         wsp_ref, bsp_ref, woa_ref, wob_ref, wo_ref,
                         xo_ref, ks_ref, vs_ref,
                         kbuf, vtbuf, oat_scr, ob_scr, bias_scr, st_scr, e_scr):
    b = pl.program_id(0)
    t = pl.program_id(1)
    T = TILE_T

    @pl.when((b == 0) & (t == 0))
    def _():
        qi = lax.broadcasted_iota(jnp.int32, (Q_BLOCK, KV_WINDOW), 0) // CHUNK
        ki = lax.broadcasted_iota(jnp.int32, (Q_BLOCK, KV_WINDOW), 1) // CHUNK
        band = (ki >= qi) & (ki <= qi + ATTN_REACH // CHUNK)
        for h in range(N_HEADS):
            tab = jnp.where(band, _rel_rows(rel_ref, h, Q_BLOCK)[:, 0:KV_WINDOW], -jnp.inf)
            bias_scr[h // 2, 0:KV_WINDOW, (h % 2) * Q_BLOCK:(h % 2 + 1) * Q_BLOCK] = tab.T
        bias_scr[:, KV_WINDOW:, :] = jnp.full((N_PAIRS, Q_BLOCK, 2 * Q_BLOCK), -jnp.inf, F32)

    @pl.when(t == 0)
    def _():
        kbuf[0:HIST, :] = jnp.zeros((HIST, D_A), BF16)
        vtbuf[:, 0:HIST] = jnp.zeros((D_A, HIST), BF16)

    @pl.when(t > 0)
    def _():
        for r in range(0, HIST, T):
            kbuf[r:r + T, :] = kbuf[r + T:r + 2 * T, :]
            vtbuf[:, r:r + T] = vtbuf[:, r + T:r + 2 * T]

    x = x_ref[...]
    mod = mod_ref[...]
    sh1, sc1, gt1 = mod[0:1], mod[1:2], mod[2:3]
    h = (_rms(x, n1g_ref[...]) * (1.0 + sc1) + sh1).astype(BF16)

    q = _head_rms(_dot(h, win_ref[:, _Q0:_Q0 + D_A]), qg_ref[...], e_ref)
    k = _head_rms(_dot(h, win_ref[:, _K0:_K0 + D_A]), kg_ref[...], e_ref)
    v = _dot(h, win_ref[:, _V0:_V0 + D_A])
    ks_ref[...] = k
    vs_ref[...] = v
    kbuf[HIST:HIST + T, :] = k.astype(BF16)
    vtbuf[:, HIST:HIST + T] = v.T.astype(BF16)
    qb = (q * (HEAD_DIM ** -0.5)).astype(BF16)

    lane = lax.broadcasted_iota(jnp.int32, (1, LANES), 1)
    low_half = lane < HEAD_DIM

    n_qb = T // Q_BLOCK

    def scores(p, slot):
        c0 = p * LANES
        maxima = []
        for j in range(n_qb):
            r0 = j * Q_BLOCK
            qp = qb[r0:r0 + Q_BLOCK, c0:c0 + LANES]
            q2 = jnp.concatenate([jnp.where(low_half, qp, jnp.zeros_like(qp)),
                                  jnp.where(low_half, jnp.zeros_like(qp), qp)], axis=0)
            st = _dot_nt(kbuf[r0:r0 + KV_WINDOW, c0:c0 + LANES], q2)
            first_valid = HIST - t * T - r0
            m = None
            for r in range(0, KV_WINDOW, Q_BLOCK):
                src = jnp.where(first_valid > r, KV_WINDOW, r) if r < HIST else r
                blk = st[r:r + Q_BLOCK] + bias_scr[p, pl.ds(pl.multiple_of(src, Q_BLOCK), Q_BLOCK), :]
                st_scr[slot, j, r:r + Q_BLOCK, :] = blk
                bm = jnp.max(blk, axis=0, keepdims=True)
                m = bm if m is None else jnp.maximum(m, bm)
            maxima.append(m)
        return maxima

    def weights(slot, maxima):
        sums = []
        for j in range(n_qb):
            e = jnp.exp(st_scr[slot, j] - maxima[j])
            e_scr[slot, j] = e.astype(BF16)
            sums.append(jnp.sum(e, axis=0, keepdims=True))
        return sums

    def outputs(p, slot, sums):
        c0 = p * LANES
        for j in range(n_qb):
            r0 = j * Q_BLOCK
            ot = _dot(vtbuf[c0:c0 + LANES, r0:r0 + KV_WINDOW], e_scr[slot, j]) / sums[j]
            oat_scr[c0:c0 + HEAD_DIM, r0:r0 + Q_BLOCK] = ot[0:HEAD_DIM, 0:Q_BLOCK]
            oat_scr[c0 + HEAD_DIM:c0 + LANES, r0:r0 + Q_BLOCK] = ot[HEAD_DIM:, Q_BLOCK:]

    side = {}

    def project(name, act, c0):
        side[name] = act(_dot(h, win_ref[:, c0:c0 + MXU_DIM]))

    side_work = ([functools.partial(project, ("b", i), jax.nn.gelu, _B0 + i * MXU_DIM)
                  for i in range(2 * D_B // MXU_DIM)]
                 + [functools.partial(project, ("g", i), jax.nn.sigmoid, _G0 + i * MXU_DIM)
                    for i in range(2 * D_MODEL // MXU_DIM)])
    maxima = scores(0, 0)
    for p in range(N_PAIRS):
        slot = p % 2
        if p + 1 < N_PAIRS:
            next_maxima = scores(p + 1, 1 - slot)
        for work in side_work[len(side_work) * p // N_PAIRS:len(side_work) * (p + 1) // N_PAIRS]:
            work()
        outputs(p, slot, weights(slot, maxima))
        maxima = next_maxima
    n_b, n_g = D_B // MXU_DIM, D_MODEL // MXU_DIM
    ub = jnp.concatenate([side["b", i] for i in range(n_b)], axis=1)
    vbn = _rms(jnp.concatenate([side["b", n_b + i] for i in range(n_b)], axis=1), vg_ref[...]).astype(BF16)
    ga = jnp.concatenate([side["g", i] for i in range(n_g)], axis=1)
    gb = jnp.concatenate([side["g", n_g + i] for i in range(n_g)], axis=1)
    row_i = lax.broadcasted_iota(jnp.int32, (MLP_CHUNK, MLP_CHUNK), 0)
    col_i = lax.broadcasted_iota(jnp.int32, (MLP_CHUNK, MLP_CHUNK), 1)
    for g in range(N_GROUPS):
        wc = jnp.where(row_i >= col_i, wsp_ref[g], 0.0).astype(BF16)
        g0 = g * GROUP_DIM
        for c in range(T // MLP_CHUNK):
            r0 = c * MLP_CHUNK
            mix = _dot(wc, vbn[r0:r0 + MLP_CHUNK, g0:g0 + GROUP_DIM]) + bsp_ref[g]
            ob_scr[r0:r0 + MLP_CHUNK, g0:g0 + GROUP_DIM] = ub[r0:r0 + MLP_CHUNK, g0:g0 + GROUP_DIM] * mix

    merged = (ga * _dot_tn(oat_scr[...].astype(BF16), woa_ref[...])
              + gb * _dot(ob_scr[...].astype(BF16), wob_ref[...]))
    xo_ref[...] = x + gt1 * _dot(merged.astype(BF16), wo_ref[...])


def _mixer_prompt_call(l, x, mod_all, mod_row0, p):
    B, S, _ = x.shape
    T = TILE_T
    n_t = S // T
    keep_t = ATTN_REACH // T
    tok = pl.BlockSpec((None, T, D_MODEL), lambda b, t: (b, t, 0))
    state = pl.BlockSpec((None, T, D_A), lambda b, t: (b, jnp.maximum(t - (n_t - keep_t), 0), 0))
    return pl.pallas_call(
        _mixer_prompt_kernel,
        grid=(B, n_t),
        in_specs=[
            tok,
            pl.BlockSpec((None, None, 6, D_MODEL), lambda b, t: (l, mod_row0 + b, 0, 0)),
            _layer_spec(l, (1, D_MODEL)),
            _layer_spec(l, (D_MODEL, D_IN)),
            _layer_spec(l, (1, D_A)),
            _layer_spec(l, (1, D_A)),
            _const_spec((MXU_DIM, MXU_DIM)),
            _layer_spec(l, (1, D_B)),
            _layer_spec(l, (N_HEADS, REL_SPAN)),
            _layer_spec(l, (N_GROUPS, MLP_CHUNK, MLP_CHUNK)),
            _layer_spec(l, (N_GROUPS, MLP_CHUNK, GROUP_DIM)),
            _layer_spec(l, (D_A, D_MODEL)),
            _layer_spec(l, (D_B, D_MODEL)),
            _layer_spec(l, (D_MODEL, D_MODEL)),
        ],
        out_specs=[tok, state, state],
        out_shape=[
            jax.ShapeDtypeStruct((B, S, D_MODEL), F32),
            jax.ShapeDtypeStruct((B, ATTN_REACH, D_A), F32),
            jax.ShapeDtypeStruct((B, ATTN_REACH, D_A), F32),
        ],
        scratch_shapes=[
            pltpu.VMEM((HIST + T, D_A), BF16),
            pltpu.VMEM((D_A, HIST + T), BF16),
            pltpu.VMEM((D_A, T), F32),
            pltpu.VMEM((T, D_B), F32),
            pltpu.VMEM((N_PAIRS, KV_WINDOW + Q_BLOCK, 2 * Q_BLOCK), F32),
            pltpu.VMEM((2, T // Q_BLOCK, KV_WINDOW, 2 * Q_BLOCK), F32),
            pltpu.VMEM((2, T // Q_BLOCK, KV_WINDOW, 2 * Q_BLOCK), BF16),
        ],
        compiler_params=pltpu.CompilerParams(
            dimension_semantics=("arbitrary", "arbitrary"), vmem_limit_bytes=VMEM_LIMIT),
        name="mixer_prompt",
    )(x, mod_all, p["n1g"], p["win"], p["qg"], p["kg"], p["e_mat"], p["vg"], p["rel"],
      p["wsp"], p["bsp_b"], p["woa"], p["wob"], p["wo"])


def _conv_gate(g, u, prev1, prev2, row, cw_ref, cb_ref, c0, c1):
    gm1 = jnp.where(row == 0, prev1, pltpu.roll(g, 1, axis=0))
    gm2 = jnp.where(row == 0, prev2, jnp.where(row == 1, prev1, pltpu.roll(g, 2, axis=0)))
    gc = cb_ref[:, c0:c1] + cw_ref[0:1, c0:c1] * gm2
    gc = gc + cw_ref[1:2, c0:c1] * gm1
    gc = gc + cw_ref[2:3, c0:c1] * g
    return jax.nn.gelu(gc) * u


def _ffn_prompt_kernel(x_ref, mod_ref, n2g_ref, wfi_ref, cw_ref, cb_ref, wfo_ref,
                       xo_ref, cs_ref, carry):
    t = pl.program_id(1)
    T = FFN_TILE_T

    @pl.when(t == 0)
    def _():
        carry[...] = jnp.zeros(carry.shape, F32)

    x = x_ref[...]
    mod = mod_ref[...]
    sh2, sc2, gt2 = mod[3:4], mod[4:5], mod[5:6]
    h2 = (_rms(x, n2g_ref[...]) * (1.0 + sc2) + sh2).astype(BF16)
    row = lax.broadcasted_iota(jnp.int32, (T, 1), 0)
    prev2 = carry[SUBLANES - 2:SUBLANES - 1, :]
    prev1 = carry[SUBLANES - 1:SUBLANES, :]

    def up_project(c0, c1):
        return _dot(h2, wfi_ref[:, c0:c1]), _dot(h2, wfi_ref[:, D_FF + c0:D_FF + c1])

    f = jnp.zeros((T, D_MODEL), F32)
    tails = []
    ahead = up_project(*FF_CHUNKS[0])
    for i, (c0, c1) in enumerate(FF_CHUNKS):
        g, u = ahead
        if i + 1 < len(FF_CHUNKS):
            ahead = up_project(*FF_CHUNKS[i + 1])
        act = _conv_gate(g, u, prev1[:, c0:c1], prev2[:, c0:c1], row, cw_ref, cb_ref, c0, c1)
        f = f + _dot(act.astype(BF16), wfo_ref[c0:c1, :])
        tails.append(g[T - SUBLANES:T, :])
    tail = jnp.concatenate(tails, axis=1)
    carry[...] = tail
    cs_ref[...] = tail
    xo_ref[...] = x + gt2 * f


def _ffn_prompt_call(l, x, mod_all, mod_row0, p):
    B, S, _ = x.shape
    T = FFN_TILE_T
    tok = pl.BlockSpec((None, T, D_MODEL), lambda b, t: (b, t, 0))
    return pl.pallas_call(
        _ffn_prompt_kernel,
        grid=(B, S // T),
        in_specs=[
            tok,
            pl.BlockSpec((None, None, 6, D_MODEL), lambda b, t: (l, mod_row0 + b, 0, 0)),
            _layer_spec(l, (1, D_MODEL)),
            _layer_spec(l, (D_MODEL, 2 * D_FF)),
            _layer_spec(l, (CONV_W, D_FF)),
            _layer_spec(l, (1, D_FF)),
            _layer_spec(l, (D_FF, D_MODEL)),
        ],
        out_specs=[tok, pl.BlockSpec((None, SUBLANES, D_FF), lambda b, t: (b, 0, 0))],
        out_shape=[
            jax.ShapeDtypeStruct((B, S, D_MODEL), F32),
            jax.ShapeDtypeStruct((B, SUBLANES, D_FF), F32),
        ],
        scratch_shapes=[pltpu.VMEM((SUBLANES, D_FF), F32)],
        compiler_params=pltpu.CompilerParams(
            dimension_semantics=("arbitrary", "arbitrary"), vmem_limit_bytes=VMEM_LIMIT),
        name="ffn_prompt",
    )(x, mod_all, p["n2g"], p["wfi"], p["cw"], p["cb"], p["wfo"])


def _mixer_sample_kernel(x_ref, mod_ref, n1g_ref, win_ref, qg_ref, kg_ref, e_ref, vg_ref, rel_ref,
                         kc_ref, vc_ref, wsp_ref, bsp_ref, woa_ref, wob_ref, wo_ref,
                         xo_ref, ks_ref, vs_ref, sv_ref,
                         q_scr, kn_scr, vn_scr, oa_scr, ub_scr, vb_scr, ob_scr, bias_scr,
                         *, n_batch, n_tok, cache_len):
    step = pl.program_id(0)
    n_rows = n_batch * n_tok
    HT = N_HEADS * n_tok

    def per_token(m):
        return jnp.broadcast_to(m, (n_batch, n_tok, m.shape[-1])).reshape(n_rows, m.shape[-1])

    def normed_input():
        mod = mod_ref[...]
        sh1, sc1 = per_token(mod[:, 0:1, :]), per_token(mod[:, 1:2, :])
        return (_rms(x_ref[...], n1g_ref[...]) * (1.0 + sc1) + sh1).astype(BF16)

    @pl.when(step == 0)
    def _():
        h = normed_input()
        q = _head_rms(_dot(h, win_ref[:, _Q0:_Q0 + D_A]), qg_ref[...], e_ref)
        k = _head_rms(_dot(h, win_ref[:, _K0:_K0 + D_A]), kg_ref[...], e_ref)
        v = _dot(h, win_ref[:, _V0:_V0 + D_A])
        ks_ref[...] = k
        vs_ref[...] = v
        kn_scr[...] = k.astype(BF16)
        vn_scr[...] = v.astype(BF16)
        q_scr[...] = (q * (HEAD_DIM ** -0.5)).astype(BF16)
        for hh in range(N_HEADS):
            bias_scr[hh * n_tok:(hh + 1) * n_tok, :] = _rel_rows(rel_ref, hh, n_tok)[:, 0:cache_len + LANES]

    lane_head = lax.broadcasted_iota(jnp.int32, (n_tok, D_A), 1) // HEAD_DIM
    for i in range(SAMPLE_BG):
        r0 = pl.multiple_of((step * SAMPLE_BG + i) * n_tok, n_tok)
        qb = q_scr[pl.ds(r0, n_tok), :]
        q_bd = jnp.concatenate([jnp.where(lane_head == hh, qb, jnp.zeros_like(qb)) for hh in range(N_HEADS)],
                               axis=0)
        kn = kn_scr[pl.ds(r0, n_tok), :]
        vn = vn_scr[pl.ds(r0, n_tok), :]
        kt = kc_ref[i].reshape(D_A, cache_len).astype(BF16)
        vt = vc_ref[i].reshape(D_A, cache_len).astype(BF16)
        s_c = _dot(q_bd, kt) + bias_scr[:, 0:cache_len]
        s_n = _dot_nt(q_bd, kn) + bias_scr[:, cache_len:cache_len + n_tok]
        m = jnp.maximum(jnp.max(s_c, axis=-1, keepdims=True), jnp.max(s_n, axis=-1, keepdims=True))
        e_c = jnp.exp(s_c - m)
        e_n = jnp.exp(s_n - m)
        l = jnp.sum(e_c, axis=-1, keepdims=True) + jnp.sum(e_n, axis=-1, keepdims=True)
        o_all = (_dot_nt(e_c.astype(BF16), vt) + _dot(e_n.astype(BF16), vn)) / l
        o = jnp.zeros((n_tok, D_A), F32)
        for hh in range(N_HEADS):
            o = jnp.where(lane_head == hh, o_all[hh * n_tok:(hh + 1) * n_tok, :], o)
        oa_scr[pl.ds(r0, n_tok), :] = o

    @pl.when(step == pl.num_programs(0) - 1)
    def _():
        h = normed_input()
        zb = jax.nn.gelu(_dot(h, win_ref[:, _B0:_B0 + 2 * D_B]))
        vbn = _rms(zb[:, D_B:2 * D_B], vg_ref[...])
        sv_ref[...] = vbn
        for g in range(N_GROUPS):
            ub_scr[g] = zb[:, g * GROUP_DIM:(g + 1) * GROUP_DIM]
            vb_scr[g] = vbn[:, g * GROUP_DIM:(g + 1) * GROUP_DIM]
        for g in range(N_GROUPS):
            for t in range(n_tok):
                mix = bsp_ref[g, t:t + 1, :]
                for s in range(t + 1):
                    mix = mix + wsp_ref[g, t, s:s + 1, :] * vb_scr[g, pl.ds(s, n_batch, stride=n_tok), :]
                ob_scr[g, pl.ds(t, n_batch, stride=n_tok), :] = ub_scr[g, pl.ds(t, n_batch, stride=n_tok), :] * mix
        ob = jnp.concatenate([ob_scr[g] for g in range(N_GROUPS)], axis=1)
        ga = jax.nn.sigmoid(_dot(h, win_ref[:, _G0:_G0 + D_MODEL]))
        gb = jax.nn.sigmoid(_dot(h, win_ref[:, _G0 + D_MODEL:_G0 + 2 * D_MODEL]))
        merged = ga * _dot(oa_scr[...].astype(BF16), woa_ref[...]) + gb * _dot(ob.astype(BF16), wob_ref[...])
        gt1 = per_token(mod_ref[...][:, 2:3, :])
        xo_ref[...] = x_ref[...] + gt1 * _dot(merged.astype(BF16), wo_ref[...])


def _mixer_sample_call(l, x2, mod_all, kc_all, vc_all, p, n_batch, n_tok):
    n_rows = n_batch * n_tok
    cache_len = kc_all.shape[-1]
    cache = pl.BlockSpec((None, SAMPLE_BG, N_HEADS, HEAD_DIM, cache_len), lambda s: (l, s, 0, 0, 0))
    full = lambda shape: pl.BlockSpec(shape, lambda s: (0,) * len(shape))
    return pl.pallas_call(
        functools.partial(_mixer_sample_kernel, n_batch=n_batch, n_tok=n_tok, cache_len=cache_len),
        grid=(n_batch // SAMPLE_BG,),
        in_specs=[
            _const_spec((n_rows, D_MODEL)),
            pl.BlockSpec((None, n_batch, 6, D_MODEL), lambda s: (l, 0, 0, 0), pipeline_mode=pl.Buffered(1)),
            _layer_spec(l, (1, D_MODEL)),
            _layer_spec(l, (D_MODEL, D_IN)),
            _layer_spec(l, (1, D_A)),
            _layer_spec(l, (1, D_A)),
            _const_spec((MXU_DIM, MXU_DIM)),
            _layer_spec(l, (1, D_B)),
            _layer_spec(l, (N_HEADS, REL_SPAN)),
            cache,
            cache,
            _layer_spec(l, (N_GROUPS, n_tok, n_tok, GROUP_DIM)),
            _layer_spec(l, (N_GROUPS, n_tok, GROUP_DIM)),
            _layer_spec(l, (D_A, D_MODEL)),
            _layer_spec(l, (D_B, D_MODEL)),
            _layer_spec(l, (D_MODEL, D_MODEL)),
        ],
        out_specs=[full((n_rows, D_MODEL)), full((n_rows, D_A)), full((n_rows, D_A)), full((n_rows, D_B))],
        out_shape=[
            jax.ShapeDtypeStruct((n_rows, D_MODEL), F32),
            jax.ShapeDtypeStruct((n_rows, D_A), F32),
            jax.ShapeDtypeStruct((n_rows, D_A), F32),
            jax.ShapeDtypeStruct((n_rows, D_B), F32),
        ],
        scratch_shapes=[
            pltpu.VMEM((n_rows, D_A), BF16),
            pltpu.VMEM((n_rows, D_A), BF16),
            pltpu.VMEM((n_rows, D_A), BF16),
            pltpu.VMEM((n_rows, D_A), F32),
            pltpu.VMEM((N_GROUPS, n_rows, GROUP_DIM), F32),
            pltpu.VMEM((N_GROUPS, n_rows, GROUP_DIM), F32),
            pltpu.VMEM((N_GROUPS, n_rows, GROUP_DIM), F32),
            pltpu.VMEM((N_HEADS * n_tok, cache_len + LANES), F32),
        ],
        compiler_params=pltpu.CompilerParams(
            dimension_semantics=("arbitrary",), vmem_limit_bytes=VMEM_LIMIT),
        name="mixer_sample",
    )(x2, mod_all, p["n1g"], p["win"], p["qg"], p["kg"], p["e_mat"], p["vg"], p["rel"],
      kc_all, vc_all, p["wsp_s"], p["bsp_s"], p["woa"], p["wob"], p["wo"])


def _ffn_sample_kernel(x_ref, mod_ref, n2g_ref, wg_ref, wu_ref, cw_ref, cb_ref, cc_ref, wfo_ref,
                       xo_ref, cs_ref, h_scr, acc, *, n_batch, n_tok):
    step = pl.program_id(0)
    n_rows = n_batch * n_tok

    def per_token(m):
        return jnp.broadcast_to(m, (n_batch, n_tok, m.shape[-1])).reshape(n_rows, m.shape[-1])

    @pl.when(step == 0)
    def _():
        mod = mod_ref[...]
        sh2, sc2 = per_token(mod[:, 3:4, :]), per_token(mod[:, 4:5, :])
        h_scr[...] = (_rms(x_ref[...], n2g_ref[...]) * (1.0 + sc2) + sh2).astype(BF16)
        acc[...] = jnp.zeros(acc.shape, F32)

    h2 = h_scr[...]
    g = _dot(h2, wg_ref[...])
    u = _dot(h2, wu_ref[...])
    cc = cc_ref[...]
    prev2, prev1 = per_token(cc[:, 0:1, :]), per_token(cc[:, 1:2, :])
    row = lax.broadcasted_iota(jnp.int32, (n_rows, 1), 0) % n_tok
    act = _conv_gate(g, u, prev1, prev2, row, cw_ref, cb_ref, 0, FF_BLOCK)
    acc[...] += _dot(act.astype(BF16), wfo_ref[...])
    cs_ref[...] = g.reshape(n_batch, n_tok, FF_BLOCK)[:, n_tok - SUBLANES:, :]

    @pl.when(step == pl.num_programs(0) - 1)
    def _():
        gt2 = per_token(mod_ref[...][:, 5:6, :])
        xo_ref[...] = x_ref[...] + gt2 * acc[...]


def _ffn_sample_call(l, x2, mod_all, cc_all, p, n_batch, n_tok):
    n_rows = n_batch * n_tok
    n_blk = D_FF // FF_BLOCK
    return pl.pallas_call(
        functools.partial(_ffn_sample_kernel, n_batch=n_batch, n_tok=n_tok),
        grid=(n_blk,),
        in_specs=[
            _const_spec((n_rows, D_MODEL)),
            pl.BlockSpec((None, n_batch, 6, D_MODEL), lambda c: (l, 0, 0, 0), pipeline_mode=pl.Buffered(1)),
            _layer_spec(l, (1, D_MODEL)),
            pl.BlockSpec((None, D_MODEL, FF_BLOCK), lambda c: (l, 0, c)),
            pl.BlockSpec((None, D_MODEL, FF_BLOCK), lambda c: (l, 0, n_blk + c)),
            pl.BlockSpec((None, CONV_W, FF_BLOCK), lambda c: (l, 0, c)),
            pl.BlockSpec((None, 1, FF_BLOCK), lambda c: (l, 0, c)),
            pl.BlockSpec((None, n_batch, CONV_W - 1, FF_BLOCK), lambda c: (l, 0, 0, c)),
            pl.BlockSpec((None, FF_BLOCK, D_MODEL), lambda c: (l, c, 0)),
        ],
        out_specs=[
            pl.BlockSpec((n_rows, D_MODEL), lambda c: (0, 0)),
            pl.BlockSpec((n_batch, SUBLANES, FF_BLOCK), lambda c: (0, 0, c)),
        ],
        out_shape=[
            jax.ShapeDtypeStruct((n_rows, D_MODEL), F32),
            jax.ShapeDtypeStruct((n_batch, SUBLANES, D_FF), F32),
        ],
        scratch_shapes=[pltpu.VMEM((n_rows, D_MODEL), BF16), pltpu.VMEM((n_rows, D_MODEL), F32)],
        compiler_params=pltpu.CompilerParams(
            dimension_semantics=("arbitrary",), vmem_limit_bytes=VMEM_LIMIT),
        name="ffn_sample",
    )(x2, mod_all, p["n2g"], p["wfi"], p["wfi"], p["cw"], p["cb"], cc_all, p["wfo"])


def _rel_vector(rel_bias):
    far = rel_bias[..., 2 * REL_CLIP:]
    lead = ATTN_REACH - REL_CLIP
    return jnp.concatenate([
        jnp.broadcast_to(far, (*rel_bias.shape[:-1], lead)),
        rel_bias[..., ::-1],
        jnp.broadcast_to(far, (*rel_bias.shape[:-1], REL_SPAN - lead - (2 * REL_CLIP + 1))),
    ], axis=-1)


def kernel(x_prompt, x_sample, cache_attn_k, cache_attn_v, cache_ffn_conv, c_prompt, c_sample, norm1_g, norm2_g, w_ada, b_ada, w_in, q_norm_g, k_norm_g, rel_bias, v_norm_g, w_spatial, b_spatial, w_out_a, w_out_b, w_out, w_ffn_in, ffn_conv_w, ffn_conv_b, w_ffn_out):
    depth = w_in.shape[0]
    B, S, _ = x_prompt.shape
    NB, NT, _ = x_sample.shape
    cache_len = cache_attn_k.shape[2]
    assert S % FFN_TILE_T == 0 and S % TILE_T == 0 and TILE_T % Q_BLOCK == 0 and HIST % TILE_T == 0 and S >= ATTN_REACH
    assert NB % SAMPLE_BG == 0 and NT % SUBLANES == 0 and NT <= MLP_CHUNK and N_HEADS * NT == LANES
    assert cache_len == ATTN_REACH and rel_bias.shape[-1] == 2 * REL_CLIP + 1

    rows = NB + B
    rows_pad = -(-rows // SUBLANES) * SUBLANES
    c_all = jnp.concatenate([c_sample, c_prompt, jnp.zeros((rows_pad - rows, D_MODEL), F32)], axis=0)
    mod_all = _ada_call(c_all, w_ada, b_ada).reshape(depth, rows_pad, 6, D_MODEL)

    head_of = jnp.arange(MXU_DIM) // HEAD_DIM
    tril = jnp.tril(jnp.ones((NT, NT), F32))
    ws_small = w_spatial[:, :, :NT, :NT] * tril
    params = {
        "n1g": norm1_g[:, None, :], "n2g": norm2_g[:, None, :],
        "win": w_in.astype(BF16), "woa": w_out_a.astype(BF16), "wob": w_out_b.astype(BF16),
        "wo": w_out.astype(BF16), "wfi": w_ffn_in.astype(BF16), "wfo": w_ffn_out.astype(BF16),
        "qg": jnp.tile(q_norm_g, (1, N_HEADS))[:, None, :], "kg": jnp.tile(k_norm_g, (1, N_HEADS))[:, None, :],
        "vg": v_norm_g[:, None, :],
        "e_mat": jnp.where(head_of[:, None] == head_of[None, :], 1.0 / HEAD_DIM, 0.0).astype(BF16),
        "rel": _rel_vector(rel_bias),
        "wsp": w_spatial,
        "bsp_b": jnp.broadcast_to(b_spatial[:, :, :, None], (depth, N_GROUPS, MLP_CHUNK, GROUP_DIM)),
        "wsp_s": jnp.broadcast_to(ws_small[..., None], (depth, N_GROUPS, NT, NT, GROUP_DIM)),
        "bsp_s": jnp.broadcast_to(b_spatial[:, :, :NT, None], (depth, N_GROUPS, NT, GROUP_DIM)),
        "cw": ffn_conv_w, "cb": ffn_conv_b[:, None, :],
    }
    kc_all = cache_attn_k.transpose(0, 1, 3, 4, 2)
    vc_all = cache_attn_v.transpose(0, 1, 3, 4, 2)

    xp = x_prompt
    xs = x_sample.reshape(NB * NT, D_MODEL)
    outs = [[] for _ in range(7)]
    for l in range(depth):
        xp, kp, vp = _mixer_prompt_call(l, xp, mod_all, NB, params)
        xp, cp = _ffn_prompt_call(l, xp, mod_all, NB, params)
        xs, ks, vs, sv = _mixer_sample_call(l, xs, mod_all, kc_all, vc_all, params, NB, NT)
        xs, cs = _ffn_sample_call(l, xs, mod_all, cache_ffn_conv, params, NB, NT)

        outs[0].append(kp.reshape(B, ATTN_REACH, N_HEADS, HEAD_DIM))
        outs[1].append(vp.reshape(B, ATTN_REACH, N_HEADS, HEAD_DIM))
        outs[2].append(cp[:, SUBLANES - (CONV_W - 1):])
        outs[3].append(ks.reshape(NB, NT, N_HEADS, HEAD_DIM))
        outs[4].append(vs.reshape(NB, NT, N_HEADS, HEAD_DIM))
        outs[5].append(sv.reshape(NB, NT, N_GROUPS, GROUP_DIM))
        outs[6].append(cs[:, SUBLANES - (CONV_W - 1):])
    stacked = [jnp.stack(o) for o in outs]
    return (xp, xs.reshape(NB, NT, D_MODEL), *stacked)
```

```python
import functools

import jax
import jax.numpy as jnp
from jax import lax
from jax.experimental import pallas as pl
from jax.experimental.pallas import tpu as pltpu

F32 = jnp.float32
BF16 = jnp.bfloat16

D_MODEL = 1024
CHUNK = 64
ATTN_REACH = 512
N_HEADS = 8
HEAD_DIM = 64
D_A = N_HEADS * HEAD_DIM
REL_CLIP = 128
MLP_CHUNK = 128
N_GROUPS = 4
D_B = 512
GROUP_DIM = D_B // N_GROUPS
D_FF = 2816
CONV_W = 3
EPS = 1e-6

LANES = 128
SUBLANES = 8
MXU_DIM = 256
N_PAIRS = N_HEADS // 2
Q_BLOCK = 2 * CHUNK
KV_WINDOW = ATTN_REACH + Q_BLOCK
REL_SPAN = KV_WINDOW + Q_BLOCK
TILE_T = 512
FFN_TILE_T = 512
FFN_SUB_T = 512
HIST = ATTN_REACH
FF_CHUNK = 1024
FF_CHUNKS = ((0, 1024), (1024, 2048), (2048, 2560), (2560, D_FF))
SAMPLE_BG = 4
FF_BLOCK = 256
VMEM_LIMIT = 56 * 1024 * 1024

_Q0, _K0, _V0, _B0, _G0 = 0, D_A, 2 * D_A, 3 * D_A, 3 * D_A + 2 * D_B
D_IN = _G0 + 2 * D_MODEL


def _dot(a, b):
    return jnp.dot(a, b, preferred_element_type=F32)


def _dot_nt(a, b):
    return lax.dot_general(a, b, (((1,), (1,)), ((), ())), preferred_element_type=F32)


def _dot_tn(a, b):
    return lax.dot_general(a, b, (((0,), (0,)), ((), ())), preferred_element_type=F32)


def _rms(x, g):
    return (x * lax.rsqrt(jnp.mean(x * x, axis=-1, keepdims=True) + EPS)) * g


def _head_rms(a, g, e_ref):
    sq = a * a
    hi = sq.astype(BF16)
    lo = (sq - hi.astype(F32)).astype(BF16)
    e = e_ref[...]
    w = e.shape[0]
    ms = jnp.concatenate([_dot(hi[:, c:c + w], e) + _dot(lo[:, c:c + w], e) for c in range(0, a.shape[1], w)],
                         axis=1)
    return (a * lax.rsqrt(ms + EPS)) * g


def _rel_rows(rel_ref, h, n_rows):
    wb = jnp.broadcast_to(rel_ref[h:h + 1, :], (n_rows, REL_SPAN))
    return pltpu.roll(wb, 0, axis=1, stride=1, stride_axis=0)


def _const_spec(shape):
    n = len(shape)
    return pl.BlockSpec(shape, lambda *_: (0,) * n, pipeline_mode=pl.Buffered(1))


def _layer_spec(l, shape):
    n = len(shape)
    return pl.BlockSpec((None, *shape), lambda *_: (l,) + (0,) * n, pipeline_mode=pl.Buffered(1))


def _ada_kernel(c_ref, w_ref, b_ref, o_ref):
    s = jax.nn.silu(c_ref[...]).astype(BF16)
    o_ref[...] = _dot(s, w_ref[...].astype(BF16)) + b_ref[...]


def _ada_call(c_all, w_ada, b_ada):
    depth = w_ada.shape[0]
    rows = c_all.shape[0]
    n_out = w_ada.shape[2]
    bn = 1536
    return pl.pallas_call(
        _ada_kernel,
        grid=(depth, n_out // bn),
        in_specs=[
            pl.BlockSpec((rows, D_MODEL), lambda l, n: (0, 0)),
            pl.BlockSpec((None, D_MODEL, bn), lambda l, n: (l, 0, n)),
            pl.BlockSpec((None, 1, bn), lambda l, n: (l, 0, n)),
        ],
        out_specs=pl.BlockSpec((None, rows, bn), lambda l, n: (l, 0, n)),
        out_shape=jax.ShapeDtypeStruct((depth, rows, n_out), F32),
        compiler_params=pltpu.CompilerParams(
            dimension_semantics=("arbitrary", "arbitrary"), vmem_limit_bytes=VMEM_LIMIT),
        name="ada_mod",
    )(c_all, w_ada, b_ada.reshape(depth, 1, n_out))


def _mixer_prompt_kernel(x_ref, mod_ref, n1g_ref, win_ref, qg_ref, kg_ref, e_ref, vg_ref, rel_ref,
                         wsp_ref, bsp_ref, woa_ref, wob_ref, wo_ref, *rest, n_carried):
    (xo_ref, kst_ref, vst_ref,
     kbuf, vtbuf, oat_scr, ob_scr, bias_scr, st_scr, e_scr) = rest[n_carried:]
    b = pl.program_id(0)
    t = pl.program_id(1)
    T = TILE_T

    @pl.when((b == 0) & (t == 0))
    def _():
        qi = lax.broadcasted_iota(jnp.int32, (Q_BLOCK, KV_WINDOW), 0) // CHUNK
        ki = lax.broadcasted_iota(jnp.int32, (Q_BLOCK, KV_WINDOW), 1) // CHUNK
        band = (ki >= qi) & (ki <= qi + ATTN_REACH // CHUNK)
        for h in range(N_HEADS):
            tab = jnp.where(band, _rel_rows(rel_ref, h, Q_BLOCK)[:, 0:KV_WINDOW], -jnp.inf)
            bias_scr[h // 2, 0:KV_WINDOW, (h % 2) * Q_BLOCK:(h % 2 + 1) * Q_BLOCK] = tab.T
        bias_scr[:, KV_WINDOW:, :] = jnp.full((N_PAIRS, Q_BLOCK, 2 * Q_BLOCK), -jnp.inf, F32)

    @pl.when(t == 0)
    def _():
        kbuf[0:HIST, :] = jnp.zeros((HIST, D_A), BF16)
        vtbuf[:, 0:HIST] = jnp.zeros((D_A, HIST), BF16)

    @pl.when(t > 0)
    def _():
        for r in range(0, HIST, T):
            kbuf[r:r + T, :] = kbuf[r + T:r + 2 * T, :]
            vtbuf[:, r:r + T] = vtbuf[:, r + T:r + 2 * T]

    mod = mod_ref[...]
    sh1, sc1, gt1 = mod[0:1], mod[1:2], mod[2:3]

    x = x_ref[...]
    h = (_rms(x, n1g_ref[...]) * (1.0 + sc1) + sh1).astype(BF16)

    q = _head_rms(_dot(h, win_ref[:, _Q0:_Q0 + D_A]), qg_ref[...], e_ref)
    k = _head_rms(_dot(h, win_ref[:, _K0:_K0 + D_A]), kg_ref[...], e_ref)
    v = _dot(h, win_ref[:, _V0:_V0 + D_A])
    vt = v.T
    kbuf[HIST:HIST + T, :] = k.astype(BF16)
    vtbuf[:, HIST:HIST + T] = vt.astype(BF16)

    kst_ref[...] = k.T
    vst_ref[...] = vt

    qb = (q * (HEAD_DIM ** -0.5)).astype(BF16)

    lane = lax.broadcasted_iota(jnp.int32, (1, LANES), 1)
    low_half = lane < HEAD_DIM

    n_qb = T // Q_BLOCK

    def scores(p, j, slot):
        r0, c0 = j * Q_BLOCK, p * LANES
        qp = qb[r0:r0 + Q_BLOCK, c0:c0 + LANES]
        q2 = jnp.concatenate([jnp.where(low_half, qp, jnp.zeros_like(qp)),
                              jnp.where(low_half, jnp.zeros_like(qp), qp)], axis=0)
        st = _dot_nt(kbuf[r0:r0 + KV_WINDOW, c0:c0 + LANES], q2)
        first_valid = HIST - t * T - r0
        m = None
        for r in range(0, KV_WINDOW, Q_BLOCK):
            src = jnp.where(first_valid > r, KV_WINDOW, r) if r < HIST else r
            blk = st[r:r + Q_BLOCK] + bias_scr[p, pl.ds(pl.multiple_of(src, Q_BLOCK), Q_BLOCK), :]
            st_scr[slot, j, r:r + Q_BLOCK, :] = blk
            bm = jnp.max(blk, axis=0, keepdims=True)
            m = bm if m is None else jnp.maximum(m, bm)
        return m

    def weights(j, slot, m):
        e = jnp.exp(st_scr[slot, j] - m)
        e_scr[slot, j] = e.astype(BF16)
        return jnp.sum(e, axis=0, keepdims=True)

    def outputs(p, j, slot, total):
        r0, c0 = j * Q_BLOCK, p * LANES
        ot = _dot(vtbuf[c0:c0 + LANES, r0:r0 + KV_WINDOW], e_scr[slot, j]) / total
        oat_scr[c0:c0 + HEAD_DIM, r0:r0 + Q_BLOCK] = ot[0:HEAD_DIM, 0:Q_BLOCK]
        oat_scr[c0 + HEAD_DIM:c0 + LANES, r0:r0 + Q_BLOCK] = ot[HEAD_DIM:, Q_BLOCK:]

    side = {}

    def project(name, act, c0):
        side[name] = act(_dot(h, win_ref[:, c0:c0 + MXU_DIM]))

    side_work = ([functools.partial(project, ("b", i), jax.nn.gelu, _B0 + i * MXU_DIM)
                  for i in range(2 * D_B // MXU_DIM)]
                 + [functools.partial(project, ("g", i), jax.nn.sigmoid, _G0 + i * MXU_DIM)
                    for i in range(2 * D_MODEL // MXU_DIM)])
    maxima = [scores(0, j, 0) for j in range(n_qb)]
    for p in range(N_PAIRS):
        slot = p % 2
        if p + 1 < N_PAIRS:
            next_maxima = [scores(p + 1, j, 1 - slot) for j in range(n_qb)]
        for work in side_work[len(side_work) * p // N_PAIRS:len(side_work) * (p + 1) // N_PAIRS]:
            work()
        totals = [weights(j, slot, maxima[j]) for j in range(n_qb)]
        for j in range(n_qb):
            outputs(p, j, slot, totals[j])
        maxima = next_maxima
    n_b, n_g = D_B // MXU_DIM, D_MODEL // MXU_DIM
    ub = jnp.concatenate([side["b", i] for i in range(n_b)], axis=1)
    vbn = _rms(jnp.concatenate([side["b", n_b + i] for i in range(n_b)], axis=1), vg_ref[...]).astype(BF16)
    ga = jnp.concatenate([side["g", i] for i in range(n_g)], axis=1)
    gb = jnp.concatenate([side["g", n_g + i] for i in range(n_g)], axis=1)
    row_i = lax.broadcasted_iota(jnp.int32, (MLP_CHUNK, MLP_CHUNK), 0)
    col_i = lax.broadcasted_iota(jnp.int32, (MLP_CHUNK, MLP_CHUNK), 1)
    for g in range(N_GROUPS):
        wc = jnp.where(row_i >= col_i, wsp_ref[g], 0.0).astype(BF16)
        g0 = g * GROUP_DIM
        for c in range(T // MLP_CHUNK):
            r0 = c * MLP_CHUNK
            mix = _dot(wc, vbn[r0:r0 + MLP_CHUNK, g0:g0 + GROUP_DIM]) + bsp_ref[g]
            ob_scr[r0:r0 + MLP_CHUNK, g0:g0 + GROUP_DIM] = ub[r0:r0 + MLP_CHUNK, g0:g0 + GROUP_DIM] * mix

    merged = (ga * _dot_tn(oat_scr[...].astype(BF16), woa_ref[...])
              + gb * _dot(ob_scr[...].astype(BF16), wob_ref[...]))
    xo_ref[...] = x + gt1 * _dot(merged.astype(BF16), wo_ref[...])


def _mixer_prompt_call(l, depth, x, mod_all, mod_row0, p, carried):
    B, S, _ = x.shape
    T = TILE_T
    n_t = S // T
    tok = pl.BlockSpec((None, T, D_MODEL), lambda b, t: (b, t, 0))
    state = pl.BlockSpec((None, None, D_A, ATTN_REACH), lambda b, t: (l, b, 0, 0))
    n_in = 14
    return pl.pallas_call(
        functools.partial(_mixer_prompt_kernel, n_carried=len(carried)),
        grid=(B, n_t),
        input_output_aliases={n_in + i: 1 + i for i in range(len(carried))},
        in_specs=[
            tok,
            pl.BlockSpec((None, None, 6, D_MODEL), lambda b, t: (l, mod_row0 + b, 0, 0)),
            _layer_spec(l, (1, D_MODEL)),
            _layer_spec(l, (D_MODEL, D_IN)),
            _layer_spec(l, (1, D_A)),
            _layer_spec(l, (1, D_A)),
            _const_spec((MXU_DIM, MXU_DIM)),
            _layer_spec(l, (1, D_B)),
            _layer_spec(l, (N_HEADS, REL_SPAN)),
            _layer_spec(l, (N_GROUPS, MLP_CHUNK, MLP_CHUNK)),
            _layer_spec(l, (N_GROUPS, MLP_CHUNK, GROUP_DIM)),
            _layer_spec(l, (D_A, D_MODEL)),
            _layer_spec(l, (D_B, D_MODEL)),
            _layer_spec(l, (D_MODEL, D_MODEL)),
        ] + [pl.BlockSpec(memory_space=pl.ANY)] * len(carried),
        out_specs=[tok, state, state],
        out_shape=[
            jax.ShapeDtypeStruct((B, S, D_MODEL), F32),
            jax.ShapeDtypeStruct((depth, B, D_A, ATTN_REACH), F32),
            jax.ShapeDtypeStruct((depth, B, D_A, ATTN_REACH), F32),
        ],
        scratch_shapes=[
            pltpu.VMEM((HIST + T, D_A), BF16),
            pltpu.VMEM((D_A, HIST + T), BF16),
            pltpu.VMEM((D_A, T), F32),
            pltpu.VMEM((T, D_B), F32),
            pltpu.VMEM((N_PAIRS, KV_WINDOW + Q_BLOCK, 2 * Q_BLOCK), F32),
            pltpu.VMEM((2, T // Q_BLOCK, KV_WINDOW, 2 * Q_BLOCK), F32),
            pltpu.VMEM((2, T // Q_BLOCK, KV_WINDOW, 2 * Q_BLOCK), BF16),
        ],
        compiler_params=pltpu.CompilerParams(
            dimension_semantics=("arbitrary", "arbitrary"), vmem_limit_bytes=VMEM_LIMIT),
        name="mixer_prompt",
    )(x, mod_all, p["n1g"], p["win"], p["qg"], p["kg"], p["e_mat"], p["vg"], p["rel"],
      p["wsp"], p["bsp_b"], p["woa"], p["wob"], p["wo"], *carried)


def _conv_gate(g, u, prev1, prev2, row, cw_ref, cb_ref, c0, c1):
    gm1 = jnp.where(row == 0, prev1, pltpu.roll(g, 1, axis=0))
    gm2 = jnp.where(row == 0, prev2, jnp.where(row == 1, prev1, pltpu.roll(g, 2, axis=0)))
    gc = cb_ref[:, c0:c1] + cw_ref[0:1, c0:c1] * gm2
    gc = gc + cw_ref[1:2, c0:c1] * gm1
    gc = gc + cw_ref[2:3, c0:c1] * g
    return jax.nn.gelu(gc) * u


def _conv_ffn(x, sh2, sc2, gt2, prev1, prev2, row, n2g_ref, wfi_ref, cw_ref, cb_ref, wfo_ref):
    h2 = (_rms(x, n2g_ref[...]) * (1.0 + sc2) + sh2).astype(BF16)

    def up_project(c0, c1):
        return _dot(h2, wfi_ref[:, c0:c1]), _dot(h2, wfi_ref[:, D_FF + c0:D_FF + c1])

    f = jnp.zeros(x.shape, F32)
    gs = []
    ahead = up_project(*FF_CHUNKS[0])
    for i, (c0, c1) in enumerate(FF_CHUNKS):
        g, u = ahead
        if i + 1 < len(FF_CHUNKS):
            ahead = up_project(*FF_CHUNKS[i + 1])
        act = _conv_gate(g, u, prev1[:, c0:c1], prev2[:, c0:c1], row, cw_ref, cb_ref, c0, c1)
        f = f + _dot(act.astype(BF16), wfo_ref[c0:c1, :])
        gs.append(g)
    return x + gt2 * f, gs


def _ffn_prompt_kernel(x_ref, mod_ref, n2g_ref, wfi_ref, cw_ref, cb_ref, wfo_ref,
                       xo_ref, cs_ref, carry):
    t = pl.program_id(1)
    T = FFN_TILE_T
    mod = mod_ref[...]

    @pl.when(t == 0)
    def _():
        carry[...] = jnp.zeros(carry.shape, F32)

    row = lax.broadcasted_iota(jnp.int32, (T, 1), 0)
    prev = carry[...]
    xo, gs = _conv_ffn(x_ref[...], mod[3:4], mod[4:5], mod[5:6], prev[SUBLANES - 1:], prev[SUBLANES - 2:SUBLANES - 1],
                       row, n2g_ref, wfi_ref, cw_ref, cb_ref, wfo_ref)
    xo_ref[...] = xo
    tail = jnp.concatenate([g[T - SUBLANES:T, :] for g in gs], axis=1)
    carry[...] = tail
    cs_ref[...] = tail


def _ffn_prompt_call(l, x, mod_all, mod_row0, p):
    B, S, _ = x.shape
    T = FFN_TILE_T
    tok = pl.BlockSpec((None, T, D_MODEL), lambda b, t: (b, t, 0))
    return pl.pallas_call(
        _ffn_prompt_kernel,
        grid=(B, S // T),
        in_specs=[
            tok,
            pl.BlockSpec((None, None, 6, D_MODEL), lambda b, t: (l, mod_row0 + b, 0, 0)),
            _layer_spec(l, (1, D_MODEL)),
            _layer_spec(l, (D_MODEL, 2 * D_FF)),
            _layer_spec(l, (CONV_W, D_FF)),
            _layer_spec(l, (1, D_FF)),
            _layer_spec(l, (D_FF, D_MODEL)),
        ],
        out_specs=[tok, pl.BlockSpec((None, SUBLANES, D_FF), lambda b, t: (b, 0, 0))],
        out_shape=[
            jax.ShapeDtypeStruct((B, S, D_MODEL), F32),
            jax.ShapeDtypeStruct((B, SUBLANES, D_FF), F32),
        ],
        scratch_shapes=[pltpu.VMEM((SUBLANES, D_FF), F32)],
        compiler_params=pltpu.CompilerParams(
            dimension_semantics=("arbitrary", "arbitrary"), vmem_limit_bytes=VMEM_LIMIT),
        name="ffn_prompt",
    )(x, mod_all, p["n2g"], p["wfi"], p["cw"], p["cb"], p["wfo"])


def _mixer_sample_kernel(x_ref, mod_ref, n1g_ref, win_ref, qg_ref, kg_ref, e_ref, vg_ref, rel_ref,
                         kc_ref, vc_ref, wsp_ref, bsp_ref, woa_ref, wob_ref, wo_ref,
                         xo_ref, ks_ref, vs_ref, sv_ref,
                         q_scr, kn_scr, vn_scr, oa_scr, ub_scr, vb_scr, ob_scr, bias_scr,
                         *, n_batch, n_tok, cache_len):
    step = pl.program_id(0)
    n_rows = n_batch * n_tok
    HT = N_HEADS * n_tok

    def per_token(m):
        return jnp.broadcast_to(m, (n_batch, n_tok, m.shape[-1])).reshape(n_rows, m.shape[-1])

    def normed_input():
        mod = mod_ref[...]
        sh1, sc1 = per_token(mod[:, 0:1, :]), per_token(mod[:, 1:2, :])
        return (_rms(x_ref[...], n1g_ref[...]) * (1.0 + sc1) + sh1).astype(BF16)

    @pl.when(step == 0)
    def _():
        h = normed_input()
        q = _head_rms(_dot(h, win_ref[:, _Q0:_Q0 + D_A]), qg_ref[...], e_ref)
        k = _head_rms(_dot(h, win_ref[:, _K0:_K0 + D_A]), kg_ref[...], e_ref)
        v = _dot(h, win_ref[:, _V0:_V0 + D_A])
        ks_ref[...] = k
        vs_ref[...] = v
        kn_scr[...] = k.astype(BF16)
        vn_scr[...] = v.astype(BF16)
        q_scr[...] = (q * (HEAD_DIM ** -0.5)).astype(BF16)
        for hh in range(N_HEADS):
            bias_scr[hh * n_tok:(hh + 1) * n_tok, :] = _rel_rows(rel_ref, hh, n_tok)[:, 0:cache_len + LANES]

    lane_head = lax.broadcasted_iota(jnp.int32, (n_tok, D_A), 1) // HEAD_DIM
    for i in range(SAMPLE_BG):
        r0 = pl.multiple_of((step * SAMPLE_BG + i) * n_tok, n_tok)
        qb = q_scr[pl.ds(r0, n_tok), :]
        q_bd = jnp.concatenate([jnp.where(lane_head == hh, qb, jnp.zeros_like(qb)) for hh in range(N_HEADS)],
                               axis=0)
        kn = kn_scr[pl.ds(r0, n_tok), :]
        vn = vn_scr[pl.ds(r0, n_tok), :]
        kt = kc_ref[i].reshape(D_A, cache_len).astype(BF16)
        vt = vc_ref[i].reshape(D_A, cache_len).astype(BF16)
        s_c = _dot(q_bd, kt) + bias_scr[:, 0:cache_len]
        s_n = _dot_nt(q_bd, kn) + bias_scr[:, cache_len:cache_len + n_tok]
        m = jnp.maximum(jnp.max(s_c, axis=-1, keepdims=True), jnp.max(s_n, axis=-1, keepdims=True))
        e_c = jnp.exp(s_c - m)
        e_n = jnp.exp(s_n - m)
        l = jnp.sum(e_c, axis=-1, keepdims=True) + jnp.sum(e_n, axis=-1, keepdims=True)
        o_all = (_dot_nt(e_c.astype(BF16), vt) + _dot(e_n.astype(BF16), vn)) / l
        o = jnp.zeros((n_tok, D_A), F32)
        for hh in range(N_HEADS):
            o = jnp.where(lane_head == hh, o_all[hh * n_tok:(hh + 1) * n_tok, :], o)
        oa_scr[pl.ds(r0, n_tok), :] = o

    @pl.when(step == pl.num_programs(0) - 1)
    def _():
        h = normed_input()
        zb = jax.nn.gelu(_dot(h, win_ref[:, _B0:_B0 + 2 * D_B]))
        vbn = _rms(zb[:, D_B:2 * D_B], vg_ref[...])
        sv_ref[...] = vbn
        for g in range(N_GROUPS):
            ub_scr[g] = zb[:, g * GROUP_DIM:(g + 1) * GROUP_DIM]
            vb_scr[g] = vbn[:, g * GROUP_DIM:(g + 1) * GROUP_DIM]
        for g in range(N_GROUPS):
            for t in range(n_tok):
                mix = bsp_ref[g, t:t + 1, :]
                for s in range(t + 1):
                    mix = mix + wsp_ref[g, t, s:s + 1, :] * vb_scr[g, pl.ds(s, n_batch, stride=n_tok), :]
                ob_scr[g, pl.ds(t, n_batch, stride=n_tok), :] = ub_scr[g, pl.ds(t, n_batch, stride=n_tok), :] * mix
        ob = jnp.concatenate([ob_scr[g] for g in range(N_GROUPS)], axis=1)
        ga = jax.nn.sigmoid(_dot(h, win_ref[:, _G0:_G0 + D_MODEL]))
        gb = jax.nn.sigmoid(_dot(h, win_ref[:, _G0 + D_MODEL:_G0 + 2 * D_MODEL]))
        merged = ga * _dot(oa_scr[...].astype(BF16), woa_ref[...]) + gb * _dot(ob.astype(BF16), wob_ref[...])
        gt1 = per_token(mod_ref[...][:, 2:3, :])
        xo_ref[...] = x_ref[...] + gt1 * _dot(merged.astype(BF16), wo_ref[...])


def _mixer_sample_call(l, x2, mod_all, kc_all, vc_all, p, n_batch, n_tok):
    n_rows = n_batch * n_tok
    cache_len = kc_all.shape[-1]
    cache = pl.BlockSpec((None, SAMPLE_BG, N_HEADS, HEAD_DIM, cache_len), lambda s: (l, s, 0, 0, 0))
    full = lambda shape: pl.BlockSpec(shape, lambda s: (0,) * len(shape))
    return pl.pallas_call(
        functools.partial(_mixer_sample_kernel, n_batch=n_batch, n_tok=n_tok, cache_len=cache_len),
        grid=(n_batch // SAMPLE_BG,),
        in_specs=[
            _const_spec((n_rows, D_MODEL)),
            pl.BlockSpec((None, n_batch, 6, D_MODEL), lambda s: (l, 0, 0, 0), pipeline_mode=pl.Buffered(1)),
            _layer_spec(l, (1, D_MODEL)),
            _layer_spec(l, (D_MODEL, D_IN)),
            _layer_spec(l, (1, D_A)),
            _layer_spec(l, (1, D_A)),
            _const_spec((MXU_DIM, MXU_DIM)),
            _layer_spec(l, (1, D_B)),
            _layer_spec(l, (N_HEADS, REL_SPAN)),
            cache,
            cache,
            _layer_spec(l, (N_GROUPS, n_tok, n_tok, GROUP_DIM)),
            _layer_spec(l, (N_GROUPS, n_tok, GROUP_DIM)),
            _layer_spec(l, (D_A, D_MODEL)),
            _layer_spec(l, (D_B, D_MODEL)),
            _layer_spec(l, (D_MODEL, D_MODEL)),
        ],
        out_specs=[full((n_rows, D_MODEL)), full((n_rows, D_A)), full((n_rows, D_A)), full((n_rows, D_B))],
        out_shape=[
            jax.ShapeDtypeStruct((n_rows, D_MODEL), F32),
            jax.ShapeDtypeStruct((n_rows, D_A), F32),
            jax.ShapeDtypeStruct((n_rows, D_A), F32),
            jax.ShapeDtypeStruct((n_rows, D_B), F32),
        ],
        scratch_shapes=[
            pltpu.VMEM((n_rows, D_A), BF16),
            pltpu.VMEM((n_rows, D_A), BF16),
            pltpu.VMEM((n_rows, D_A), BF16),
            pltpu.VMEM((n_rows, D_A), F32),
            pltpu.VMEM((N_GROUPS, n_rows, GROUP_DIM), F32),
            pltpu.VMEM((N_GROUPS, n_rows, GROUP_DIM), F32),
            pltpu.VMEM((N_GROUPS, n_rows, GROUP_DIM), F32),
            pltpu.VMEM((N_HEADS * n_tok, cache_len + LANES), F32),
        ],
        compiler_params=pltpu.CompilerParams(
            dimension_semantics=("arbitrary",), vmem_limit_bytes=VMEM_LIMIT),
        name="mixer_sample",
    )(x2, mod_all, p["n1g"], p["win"], p["qg"], p["kg"], p["e_mat"], p["vg"], p["rel"],
      kc_all, vc_all, p["wsp_s"], p["bsp_s"], p["woa"], p["wob"], p["wo"])


def _ffn_sample_kernel(x_ref, mod_ref, n2g_ref, wfi_ref, cw_ref, cb_ref, cc_ref, wfo_ref,
                       xo_ref, cs_ref, *, n_batch, n_tok):
    n_rows = n_batch * n_tok

    def per_token(m):
        return jnp.broadcast_to(m, (n_batch, n_tok, m.shape[-1])).reshape(n_rows, m.shape[-1])

    mod = mod_ref[...]
    cc = cc_ref[...]
    row = lax.broadcasted_iota(jnp.int32, (n_rows, 1), 0) % n_tok
    xo, gs = _conv_ffn(x_ref[...], per_token(mod[:, 3:4, :]), per_token(mod[:, 4:5, :]), per_token(mod[:, 5:6, :]),
                       per_token(cc[:, 1:2, :]), per_token(cc[:, 0:1, :]), row,
                       n2g_ref, wfi_ref, cw_ref, cb_ref, wfo_ref)
    xo_ref[...] = xo
    for g, (c0, c1) in zip(gs, FF_CHUNKS):
        cs_ref[:, :, c0:c1] = g.reshape(n_batch, n_tok, c1 - c0)[:, n_tok - SUBLANES:, :]


def _ffn_sample_call(l, x2, mod_all, cc_all, p, n_batch, n_tok):
    n_rows = n_batch * n_tok
    return pl.pallas_call(
        functools.partial(_ffn_sample_kernel, n_batch=n_batch, n_tok=n_tok),
        grid=(1,),
        in_specs=[
            _const_spec((n_rows, D_MODEL)),
            _layer_spec(l, (n_batch, 6, D_MODEL)),
            _layer_spec(l, (1, D_MODEL)),
            _layer_spec(l, (D_MODEL, 2 * D_FF)),
            _layer_spec(l, (CONV_W, D_FF)),
            _layer_spec(l, (1, D_FF)),
            _layer_spec(l, (n_batch, CONV_W - 1, D_FF)),
            _layer_spec(l, (D_FF, D_MODEL)),
        ],
        out_specs=[
            pl.BlockSpec((n_rows, D_MODEL), lambda c: (0, 0)),
            pl.BlockSpec((n_batch, SUBLANES, D_FF), lambda c: (0, 0, 0)),
        ],
        out_shape=[
            jax.ShapeDtypeStruct((n_rows, D_MODEL), F32),
            jax.ShapeDtypeStruct((n_batch, SUBLANES, D_FF), F32),
        ],
        compiler_params=pltpu.CompilerParams(
            dimension_semantics=("arbitrary",), vmem_limit_bytes=VMEM_LIMIT),
        name="ffn_sample",
    )(x2, mod_all, p["n2g"], p["wfi"], p["cw"], p["cb"], cc_all, p["wfo"])


def _rel_vector(rel_bias):
    far = rel_bias[..., 2 * REL_CLIP:]
    lead = ATTN_REACH - REL_CLIP
    return jnp.concatenate([
        jnp.broadcast_to(far, (*rel_bias.shape[:-1], lead)),
        rel_bias[..., ::-1],
        jnp.broadcast_to(far, (*rel_bias.shape[:-1], REL_SPAN - lead - (2 * REL_CLIP + 1))),
    ], axis=-1)


def kernel(x_prompt, x_sample, cache_attn_k, cache_attn_v, cache_ffn_conv, c_prompt, c_sample, norm1_g, norm2_g, w_ada, b_ada, w_in, q_norm_g, k_norm_g, rel_bias, v_norm_g, w_spatial, b_spatial, w_out_a, w_out_b, w_out, w_ffn_in, ffn_conv_w, ffn_conv_b, w_ffn_out):
    depth = w_in.shape[0]
    B, S, _ = x_prompt.shape
    NB, NT, _ = x_sample.shape
    cache_len = cache_attn_k.shape[2]
    assert S % FFN_TILE_T == 0 and S % TILE_T == 0 and TILE_T % Q_BLOCK == 0 and TILE_T == ATTN_REACH and S >= ATTN_REACH
    assert NB % SAMPLE_BG == 0 and NT % SUBLANES == 0 and NT <= MLP_CHUNK and N_HEADS * NT == LANES
    assert cache_len == ATTN_REACH and rel_bias.shape[-1] == 2 * REL_CLIP + 1

    rows = NB + B
    rows_pad = -(-rows // SUBLANES) * SUBLANES
    c_all = jnp.concatenate([c_sample, c_prompt, jnp.zeros((rows_pad - rows, D_MODEL), F32)], axis=0)
    mod_all = _ada_call(c_all, w_ada, b_ada).reshape(depth, rows_pad, 6, D_MODEL)

    head_of = jnp.arange(MXU_DIM) // HEAD_DIM
    tril = jnp.tril(jnp.ones((NT, NT), F32))
    ws_small = w_spatial[:, :, :NT, :NT] * tril
    params = {
        "n1g": norm1_g[:, None, :], "n2g": norm2_g[:, None, :],
        "win": w_in.astype(BF16), "woa": w_out_a.astype(BF16), "wob": w_out_b.astype(BF16),
        "wo": w_out.astype(BF16), "wfi": w_ffn_in.astype(BF16), "wfo": w_ffn_out.astype(BF16),
        "qg": jnp.tile(q_norm_g, (1, N_HEADS))[:, None, :], "kg": jnp.tile(k_norm_g, (1, N_HEADS))[:, None, :],
        "vg": v_norm_g[:, None, :],
        "e_mat": jnp.where(head_of[:, None] == head_of[None, :], 1.0 / HEAD_DIM, 0.0).astype(BF16),
        "rel": _rel_vector(rel_bias),
        "wsp": w_spatial,
        "bsp_b": jnp.broadcast_to(b_spatial[:, :, :, None], (depth, N_GROUPS, MLP_CHUNK, GROUP_DIM)),
        "wsp_s": jnp.broadcast_to(ws_small[..., None], (depth, N_GROUPS, NT, NT, GROUP_DIM)),
        "bsp_s": jnp.broadcast_to(b_spatial[:, :, :NT, None], (depth, N_GROUPS, NT, GROUP_DIM)),
        "cw": ffn_conv_w, "cb": ffn_conv_b[:, None, :],
    }
    kc_all = cache_attn_k.transpose(0, 1, 3, 4, 2)
    vc_all = cache_attn_v.transpose(0, 1, 3, 4, 2)

    xp = x_prompt
    xs = x_sample.reshape(NB * NT, D_MODEL)
    outs = [[] for _ in range(5)]
    prompt_state = ()
    for l in range(depth):
        xp, *prompt_state = _mixer_prompt_call(l, depth, xp, mod_all, NB, params, tuple(prompt_state))
        xp, cp = _ffn_prompt_call(l, xp, mod_all, NB, params)
        xs, ks, vs, sv = _mixer_sample_call(l, xs, mod_all, kc_all, vc_all, params, NB, NT)
        xs, cs = _ffn_sample_call(l, xs, mod_all, cache_ffn_conv, params, NB, NT)

        outs[0].append(cp[:, SUBLANES - (CONV_W - 1):])
        outs[1].append(ks.reshape(NB, NT, N_HEADS, HEAD_DIM))
        outs[2].append(vs.reshape(NB, NT, N_HEADS, HEAD_DIM))
        outs[3].append(sv.reshape(NB, NT, N_GROUPS, GROUP_DIM))
        outs[4].append(cs[:, SUBLANES - (CONV_W - 1):])
    new_k_prompt, new_v_prompt = (
        s.reshape(depth, B, N_HEADS, HEAD_DIM, ATTN_REACH).transpose(0, 1, 4, 2, 3) for s in prompt_state)
    stacked = [jnp.stack(o) for o in outs]
    return (xp, xs.reshape(NB, NT, D_MODEL), new_k_prompt, new_v_prompt, *stacked)
```

```python
import functools

import jax
import jax.numpy as jnp
from jax import lax
from jax.experimental import pallas as pl
from jax.experimental.pallas import tpu as pltpu

F32 = jnp.float32
BF16 = jnp.bfloat16

D_MODEL = 1024
CHUNK = 64
ATTN_REACH = 512
N_HEADS = 8
HEAD_DIM = 64
D_A = N_HEADS * HEAD_DIM
REL_CLIP = 128
MLP_CHUNK = 128
N_GROUPS = 4
D_B = 512
GROUP_DIM = D_B // N_GROUPS
D_FF = 2816
CONV_W = 3
EPS = 1e-6

LANES = 128
SUBLANES = 8
MXU_DIM = 256
N_PAIRS = N_HEADS // 2
Q_BLOCK = 2 * CHUNK
KV_WINDOW = ATTN_REACH + Q_BLOCK
REL_SPAN = KV_WINDOW + Q_BLOCK
TILE_T = 512
FFN_TILE_T = 512
HIST = ATTN_REACH
FF_CHUNKS = ((0, 1024), (1024, 2048), (2048, 2560), (2560, D_FF))
SAMPLE_BG = 4
VMEM_LIMIT = 56 * 1024 * 1024
_Q0, _K0, _V0, _B0, _G0 = 0, D_A, 2 * D_A, 3 * D_A, 3 * D_A + 2 * D_B
D_IN = _G0 + 2 * D_MODEL


def _dot(a, b):
    return jnp.dot(a, b, preferred_element_type=F32)


def _dot_nt(a, b):
    return lax.dot_general(a, b, (((1,), (1,)), ((), ())), preferred_element_type=F32)


def _dot_tn(a, b):
    return lax.dot_general(a, b, (((0,), (0,)), ((), ())), preferred_element_type=F32)


def _rms(x, g):
    return (x * lax.rsqrt(jnp.mean(x * x, axis=-1, keepdims=True) + EPS)) * g


def _head_rms(a, g, e_ref):
    sq = a * a
    hi = sq.astype(BF16)
    lo = (sq - hi.astype(F32)).astype(BF16)
    e = e_ref[...]
    w = e.shape[0]
    ms = jnp.concatenate([_dot(hi[:, c:c + w], e) + _dot(lo[:, c:c + w], e) for c in range(0, a.shape[1], w)],
                         axis=1)
    return (a * lax.rsqrt(ms + EPS)) * g


def _rel_rows(rel_ref, h, n_rows):
    wb = jnp.broadcast_to(rel_ref[h:h + 1, :], (n_rows, REL_SPAN))
    return pltpu.roll(wb, 0, axis=1, stride=1, stride_axis=0)


def _const_spec(shape):
    n = len(shape)
    return pl.BlockSpec(shape, lambda *_: (0,) * n, pipeline_mode=pl.Buffered(1))


def _layer_spec(l, shape):
    n = len(shape)
    return pl.BlockSpec((None, *shape), lambda *_: (l,) + (0,) * n, pipeline_mode=pl.Buffered(1))


def _ada_kernel(c_ref, w_ref, b_ref, o_ref):
    s = jax.nn.silu(c_ref[...]).astype(BF16)
    o_ref[...] = _dot(s, w_ref[...].astype(BF16)) + b_ref[...]


def _ada_call(c_all, w_ada, b_ada):
    depth = w_ada.shape[0]
    rows = c_all.shape[0]
    n_out = w_ada.shape[2]
    bn = 1536
    return pl.pallas_call(
        _ada_kernel,
        grid=(depth, n_out // bn),
        in_specs=[
            pl.BlockSpec((rows, D_MODEL), lambda l, n: (0, 0)),
            pl.BlockSpec((None, D_MODEL, bn), lambda l, n: (l, 0, n)),
            pl.BlockSpec((None, 1, bn), lambda l, n: (l, 0, n)),
        ],
        out_specs=pl.BlockSpec((None, rows, bn), lambda l, n: (l, 0, n)),
        out_shape=jax.ShapeDtypeStruct((depth, rows, n_out), F32),
        compiler_params=pltpu.CompilerParams(
            dimension_semantics=("arbitrary", "arbitrary"), vmem_limit_bytes=VMEM_LIMIT),
        name="ada_mod",
    )(c_all, w_ada, b_ada.reshape(depth, 1, n_out))


def _mixer_prompt_kernel(x_ref, mod_ref, n1g_ref, win_ref, qg_ref, kg_ref, e_ref, vg_ref, rel_ref,
                         wsp_ref, bsp_ref, woa_ref, wob_ref, wo_ref, *rest, n_carried):
    (xo_ref, kst_ref, vst_ref,
     kbuf, vtbuf, oat_scr, ob_scr, bias_scr, st_scr, e_scr) = rest[n_carried:]
    b = pl.program_id(0)
    t = pl.program_id(1)
    T = TILE_T

    @pl.when((b == 0) & (t == 0))
    def _():
        qi = lax.broadcasted_iota(jnp.int32, (Q_BLOCK, KV_WINDOW), 0) // CHUNK
        ki = lax.broadcasted_iota(jnp.int32, (Q_BLOCK, KV_WINDOW), 1) // CHUNK
        band = (ki >= qi) & (ki <= qi + ATTN_REACH // CHUNK)
        for h in range(N_HEADS):
            tab = jnp.where(band, _rel_rows(rel_ref, h, Q_BLOCK)[:, 0:KV_WINDOW], -jnp.inf)
            bias_scr[h // 2, 0:KV_WINDOW, (h % 2) * Q_BLOCK:(h % 2 + 1) * Q_BLOCK] = tab.T
        bias_scr[:, KV_WINDOW:, :] = jnp.full((N_PAIRS, Q_BLOCK, 2 * Q_BLOCK), -jnp.inf, F32)

    @pl.when(t == 0)
    def _():
        kbuf[0:HIST, :] = jnp.zeros((HIST, D_A), BF16)
        vtbuf[:, 0:HIST] = jnp.zeros((D_A, HIST), BF16)

    @pl.when(t > 0)
    def _():
        for r in range(0, HIST, T):
            kbuf[r:r + T, :] = kbuf[r + T:r + 2 * T, :]
            vtbuf[:, r:r + T] = vtbuf[:, r + T:r + 2 * T]

    mod = mod_ref[...]
    sh1, sc1, gt1 = mod[0:1], mod[1:2], mod[2:3]

    x = x_ref[...]
    h = (_rms(x, n1g_ref[...]) * (1.0 + sc1) + sh1).astype(BF16)

    q = _head_rms(_dot(h, win_ref[:, _Q0:_Q0 + D_A]), qg_ref[...], e_ref)
    k = _head_rms(_dot(h, win_ref[:, _K0:_K0 + D_A]), kg_ref[...], e_ref)
    v = _dot(h, win_ref[:, _V0:_V0 + D_A])
    vt = v.T
    kbuf[HIST:HIST + T, :] = k.astype(BF16)
    vtbuf[:, HIST:HIST + T] = vt.astype(BF16)

    kst_ref[...] = k.T
    vst_ref[...] = vt

    qb = (q * (HEAD_DIM ** -0.5)).astype(BF16)

    lane = lax.broadcasted_iota(jnp.int32, (1, LANES), 1)
    low_half = lane < HEAD_DIM

    n_qb = T // Q_BLOCK

    def scores(p, j, slot):
        r0, c0 = j * Q_BLOCK, p * LANES
        qp = qb[r0:r0 + Q_BLOCK, c0:c0 + LANES]
        q2 = jnp.concatenate([jnp.where(low_half, qp, jnp.zeros_like(qp)),
                              jnp.where(low_half, jnp.zeros_like(qp), qp)], axis=0)
        st = _dot_nt(kbuf[r0:r0 + KV_WINDOW, c0:c0 + LANES], q2)
        first_valid = HIST - t * T - r0
        m = None
        for r in range(0, KV_WINDOW, Q_BLOCK):
            src = jnp.where(first_valid > r, KV_WINDOW, r) if r < HIST else r
            blk = st[r:r + Q_BLOCK] + bias_scr[p, pl.ds(pl.multiple_of(src, Q_BLOCK), Q_BLOCK), :]
            st_scr[slot, j, r:r + Q_BLOCK, :] = blk
            bm = jnp.max(blk, axis=0, keepdims=True)
            m = bm if m is None else jnp.maximum(m, bm)
        return m

    def weights(j, slot, m):
        e = jnp.exp(st_scr[slot, j] - m)
        e_scr[slot, j] = e.astype(BF16)
        return jnp.sum(e, axis=0, keepdims=True)

    def outputs(p, j, slot, total):
        r0, c0 = j * Q_BLOCK, p * LANES
        ot = _dot(vtbuf[c0:c0 + LANES, r0:r0 + KV_WINDOW], e_scr[slot, j]) / total
        oat_scr[c0:c0 + HEAD_DIM, r0:r0 + Q_BLOCK] = ot[0:HEAD_DIM, 0:Q_BLOCK]
        oat_scr[c0 + HEAD_DIM:c0 + LANES, r0:r0 + Q_BLOCK] = ot[HEAD_DIM:, Q_BLOCK:]

    side = {}

    def project(name, act, c0):
        side[name] = act(_dot(h, win_ref[:, c0:c0 + MXU_DIM]))

    side_work = ([functools.partial(project, ("b", i), jax.nn.gelu, _B0 + i * MXU_DIM)
                  for i in range(2 * D_B // MXU_DIM)]
                 + [functools.partial(project, ("g", i), jax.nn.sigmoid, _G0 + i * MXU_DIM)
                    for i in range(2 * D_MODEL // MXU_DIM)])
    maxima = [scores(0, j, 0) for j in range(n_qb)]
    for p in range(N_PAIRS):
        slot = p % 2
        if p + 1 < N_PAIRS:
            next_maxima = [scores(p + 1, j, 1 - slot) for j in range(n_qb)]
        for work in side_work[len(side_work) * p // N_PAIRS:len(side_work) * (p + 1) // N_PAIRS]:
            work()
        totals = [weights(j, slot, maxima[j]) for j in range(n_qb)]
        for j in range(n_qb):
            outputs(p, j, slot, totals[j])
        maxima = next_maxima
    n_b, n_g = D_B // MXU_DIM, D_MODEL // MXU_DIM
    ub = jnp.concatenate([side["b", i] for i in range(n_b)], axis=1)
    vbn = _rms(jnp.concatenate([side["b", n_b + i] for i in range(n_b)], axis=1), vg_ref[...]).astype(BF16)
    ga = jnp.concatenate([side["g", i] for i in range(n_g)], axis=1)
    gb = jnp.concatenate([side["g", n_g + i] for i in range(n_g)], axis=1)
    row_i = lax.broadcasted_iota(jnp.int32, (MLP_CHUNK, MLP_CHUNK), 0)
    col_i = lax.broadcasted_iota(jnp.int32, (MLP_CHUNK, MLP_CHUNK), 1)
    for g in range(N_GROUPS):
        wc = jnp.where(row_i >= col_i, wsp_ref[g], 0.0).astype(BF16)
        g0 = g * GROUP_DIM
        for c in range(T // MLP_CHUNK):
            r0 = c * MLP_CHUNK
            mix = _dot(wc, vbn[r0:r0 + MLP_CHUNK, g0:g0 + GROUP_DIM]) + bsp_ref[g]
            ob_scr[r0:r0 + MLP_CHUNK, g0:g0 + GROUP_DIM] = ub[r0:r0 + MLP_CHUNK, g0:g0 + GROUP_DIM] * mix

    merged = (ga * _dot_tn(oat_scr[...].astype(BF16), woa_ref[...])
              + gb * _dot(ob_scr[...].astype(BF16), wob_ref[...]))
    xo_ref[...] = x + gt1 * _dot(merged.astype(BF16), wo_ref[...])


def _mixer_prompt_call(l, depth, x, mod_all, mod_row0, p, carried):
    B, S, _ = x.shape
    T = TILE_T
    n_t = S // T
    tok = pl.BlockSpec((None, T, D_MODEL), lambda b, t: (b, t, 0))
    state = pl.BlockSpec((None, None, D_A, ATTN_REACH), lambda b, t: (l, b, 0, 0))
    in_specs = [
        tok,
        pl.BlockSpec((None, None, 6, D_MODEL), lambda b, t: (l, mod_row0 + b, 0, 0)),
        _layer_spec(l, (1, D_MODEL)),
        _layer_spec(l, (D_MODEL, D_IN)),
        _layer_spec(l, (1, D_A)),
        _layer_spec(l, (1, D_A)),
        _const_spec((MXU_DIM, MXU_DIM)),
        _layer_spec(l, (1, D_B)),
        _layer_spec(l, (N_HEADS, REL_SPAN)),
        _layer_spec(l, (N_GROUPS, MLP_CHUNK, MLP_CHUNK)),
        _layer_spec(l, (N_GROUPS, MLP_CHUNK, GROUP_DIM)),
        _layer_spec(l, (D_A, D_MODEL)),
        _layer_spec(l, (D_B, D_MODEL)),
        _layer_spec(l, (D_MODEL, D_MODEL)),
    ]
    return pl.pallas_call(
        functools.partial(_mixer_prompt_kernel, n_carried=len(carried)),
        grid=(B, n_t),
        input_output_aliases={len(in_specs) + i: 1 + i for i in range(len(carried))},
        in_specs=in_specs + [pl.BlockSpec(memory_space=pl.ANY)] * len(carried),
        out_specs=[tok, state, state],
        out_shape=[
            jax.ShapeDtypeStruct((B, S, D_MODEL), F32),
            jax.ShapeDtypeStruct((depth, B, D_A, ATTN_REACH), F32),
            jax.ShapeDtypeStruct((depth, B, D_A, ATTN_REACH), F32),
        ],
        scratch_shapes=[
            pltpu.VMEM((HIST + T, D_A), BF16),
            pltpu.VMEM((D_A, HIST + T), BF16),
            pltpu.VMEM((D_A, T), F32),
            pltpu.VMEM((T, D_B), F32),
            pltpu.VMEM((N_PAIRS, KV_WINDOW + Q_BLOCK, 2 * Q_BLOCK), F32),
            pltpu.VMEM((2, T // Q_BLOCK, KV_WINDOW, 2 * Q_BLOCK), F32),
            pltpu.VMEM((2, T // Q_BLOCK, KV_WINDOW, 2 * Q_BLOCK), BF16),
        ],
        compiler_params=pltpu.CompilerParams(
            dimension_semantics=("arbitrary", "arbitrary"), vmem_limit_bytes=VMEM_LIMIT),
        name="mixer_prompt",
    )(x, mod_all, p["n1g"], p["win"], p["qg"], p["kg"], p["e_mat"], p["vg"], p["rel"],
      p["wsp"], p["bsp_b"], p["woa"], p["wob"], p["wo"], *carried)


def _conv_gate(g, u, prev1, prev2, row, cw_ref, cb_ref, c0, c1):
    gm1 = jnp.where(row == 0, prev1, pltpu.roll(g, 1, axis=0))
    gm2 = jnp.where(row == 0, prev2, jnp.where(row == 1, prev1, pltpu.roll(g, 2, axis=0)))
    gc = cb_ref[:, c0:c1] + cw_ref[0:1, c0:c1] * gm2
    gc = gc + cw_ref[1:2, c0:c1] * gm1
    gc = gc + cw_ref[2:3, c0:c1] * g
    return jax.nn.gelu(gc) * u


def _conv_ffn(x, sh2, sc2, gt2, prev1, prev2, row, n2g_ref, wfi_ref, cw_ref, cb_ref, wfo_ref):
    h2 = (_rms(x, n2g_ref[...]) * (1.0 + sc2) + sh2).astype(BF16)

    def up_project(c0, c1):
        return _dot(h2, wfi_ref[:, c0:c1]), _dot(h2, wfi_ref[:, D_FF + c0:D_FF + c1])

    f = jnp.zeros(x.shape, F32)
    gs = []
    ahead = up_project(*FF_CHUNKS[0])
    for i, (c0, c1) in enumerate(FF_CHUNKS):
        g, u = ahead
        if i + 1 < len(FF_CHUNKS):
            ahead = up_project(*FF_CHUNKS[i + 1])
        act = _conv_gate(g, u, prev1[:, c0:c1], prev2[:, c0:c1], row, cw_ref, cb_ref, c0, c1)
        f = f + _dot(act.astype(BF16), wfo_ref[c0:c1, :])
        gs.append(g)
    return x + gt2 * f, gs


def _ffn_prompt_kernel(x_ref, mod_ref, n2g_ref, wfi_ref, cw_ref, cb_ref, wfo_ref,
                       xo_ref, cs_ref, carry):
    t = pl.program_id(1)
    T = FFN_TILE_T
    mod = mod_ref[...]

    @pl.when(t == 0)
    def _():
        carry[...] = jnp.zeros(carry.shape, F32)

    row = lax.broadcasted_iota(jnp.int32, (T, 1), 0)
    prev = carry[...]
    xo, gs = _conv_ffn(x_ref[...], mod[3:4], mod[4:5], mod[5:6], prev[SUBLANES - 1:], prev[SUBLANES - 2:SUBLANES - 1],
                       row, n2g_ref, wfi_ref, cw_ref, cb_ref, wfo_ref)
    xo_ref[...] = xo
    tail = jnp.concatenate([g[T - SUBLANES:T, :] for g in gs], axis=1)
    carry[...] = tail
    cs_ref[...] = tail


def _ffn_prompt_call(l, x, mod_all, mod_row0, p):
    B, S, _ = x.shape
    T = FFN_TILE_T
    tok = pl.BlockSpec((None, T, D_MODEL), lambda b, t: (b, t, 0))
    return pl.pallas_call(
        _ffn_prompt_kernel,
        grid=(B, S // T),
        in_specs=[
            tok,
            pl.BlockSpec((None, None, 6, D_MODEL), lambda b, t: (l, mod_row0 + b, 0, 0)),
            _layer_spec(l, (1, D_MODEL)),
            _layer_spec(l, (D_MODEL, 2 * D_FF)),
            _layer_spec(l, (CONV_W, D_FF)),
            _layer_spec(l, (1, D_FF)),
            _layer_spec(l, (D_FF, D_MODEL)),
        ],
        out_specs=[tok, pl.BlockSpec((None, SUBLANES, D_FF), lambda b, t: (b, 0, 0))],
        out_shape=[
            jax.ShapeDtypeStruct((B, S, D_MODEL), F32),
            jax.ShapeDtypeStruct((B, SUBLANES, D_FF), F32),
        ],
        scratch_shapes=[pltpu.VMEM((SUBLANES, D_FF), F32)],
        compiler_params=pltpu.CompilerParams(
            dimension_semantics=("arbitrary", "arbitrary"), vmem_limit_bytes=VMEM_LIMIT),
        name="ffn_prompt",
    )(x, mod_all, p["n2g"], p["wfi"], p["cw"], p["cb"], p["wfo"])


def _mixer_sample_kernel(x_ref, mod_ref, n1g_ref, win_ref, qg_ref, kg_ref, e_ref, vg_ref, rel_ref,
                         kc_ref, vc_ref, wsp_ref, bsp_ref, woa_ref, wob_ref, wo_ref, *rest,
                         n_carried, n_batch, n_tok, cache_len):
    (xo_ref, ks_ref, vs_ref, sv_ref,
     q_scr, kn_scr, vn_scr, oa_scr, ub_scr, vb_scr, ob_scr, bias_scr) = rest[n_carried:]
    step = pl.program_id(0)
    n_rows = n_batch * n_tok
    HT = N_HEADS * n_tok

    def per_token(m):
        return jnp.broadcast_to(m, (n_batch, n_tok, m.shape[-1])).reshape(n_rows, m.shape[-1])

    def normed_input():
        mod = mod_ref[...]
        sh1, sc1 = per_token(mod[:, 0:1, :]), per_token(mod[:, 1:2, :])
        return (_rms(x_ref[...], n1g_ref[...]) * (1.0 + sc1) + sh1).astype(BF16)

    @pl.when(step == 0)
    def _():
        h = normed_input()
        q = _head_rms(_dot(h, win_ref[:, _Q0:_Q0 + D_A]), qg_ref[...], e_ref)
        k = _head_rms(_dot(h, win_ref[:, _K0:_K0 + D_A]), kg_ref[...], e_ref)
        v = _dot(h, win_ref[:, _V0:_V0 + D_A])
        ks_ref[...] = k
        vs_ref[...] = v
        kn_scr[...] = k.astype(BF16)
        vn_scr[...] = v.astype(BF16)
        q_scr[...] = (q * (HEAD_DIM ** -0.5)).astype(BF16)
        for hh in range(N_HEADS):
            bias_scr[hh * n_tok:(hh + 1) * n_tok, :] = _rel_rows(rel_ref, hh, n_tok)[:, 0:cache_len + LANES]

    lane_head = lax.broadcasted_iota(jnp.int32, (n_tok, D_A), 1) // HEAD_DIM
    for i in range(SAMPLE_BG):
        r0 = pl.multiple_of((step * SAMPLE_BG + i) * n_tok, n_tok)
        qb = q_scr[pl.ds(r0, n_tok), :]
        q_bd = jnp.concatenate([jnp.where(lane_head == hh, qb, jnp.zeros_like(qb)) for hh in range(N_HEADS)],
                               axis=0)
        kn = kn_scr[pl.ds(r0, n_tok), :]
        vn = vn_scr[pl.ds(r0, n_tok), :]
        kt = kc_ref[i].reshape(D_A, cache_len).astype(BF16)
        vt = vc_ref[i].reshape(D_A, cache_len).astype(BF16)
        s_c = _dot(q_bd, kt) + bias_scr[:, 0:cache_len]
        s_n = _dot_nt(q_bd, kn) + bias_scr[:, cache_len:cache_len + n_tok]
        m = jnp.maximum(jnp.max(s_c, axis=-1, keepdims=True), jnp.max(s_n, axis=-1, keepdims=True))
        e_c = jnp.exp(s_c - m)
        e_n = jnp.exp(s_n - m)
        l = jnp.sum(e_c, axis=-1, keepdims=True) + jnp.sum(e_n, axis=-1, keepdims=True)
        o_all = (_dot_nt(e_c.astype(BF16), vt) + _dot(e_n.astype(BF16), vn)) / l
        o = jnp.zeros((n_tok, D_A), F32)
        for hh in range(N_HEADS):
            o = jnp.where(lane_head == hh, o_all[hh * n_tok:(hh + 1) * n_tok, :], o)
        oa_scr[pl.ds(r0, n_tok), :] = o

    @pl.when(step == pl.num_programs(0) - 1)
    def _():
        h = normed_input()
        zb = jax.nn.gelu(_dot(h, win_ref[:, _B0:_B0 + 2 * D_B]))
        vbn = _rms(zb[:, D_B:2 * D_B], vg_ref[...])
        sv_ref[...] = vbn
        for g in range(N_GROUPS):
            ub_scr[g] = zb[:, g * GROUP_DIM:(g + 1) * GROUP_DIM]
            vb_scr[g] = vbn[:, g * GROUP_DIM:(g + 1) * GROUP_DIM]
        for g in range(N_GROUPS):
            at_pos = [vb_scr[g, pl.ds(s, n_batch, stride=n_tok), :] for s in range(n_tok)]
            for t in range(n_tok):
                mix = bsp_ref[g, t:t + 1, :]
                for s in range(t + 1):
                    mix = mix + wsp_ref[g, t, s:s + 1, :] * at_pos[s]
                ob_scr[g, pl.ds(t, n_batch, stride=n_tok), :] = ub_scr[g, pl.ds(t, n_batch, stride=n_tok), :] * mix
        ob = jnp.concatenate([ob_scr[g] for g in range(N_GROUPS)], axis=1)
        ga = jax.nn.sigmoid(_dot(h, win_ref[:, _G0:_G0 + D_MODEL]))
        gb = jax.nn.sigmoid(_dot(h, win_ref[:, _G0 + D_MODEL:_G0 + 2 * D_MODEL]))
        merged = ga * _dot(oa_scr[...].astype(BF16), woa_ref[...]) + gb * _dot(ob.astype(BF16), wob_ref[...])
        gt1 = per_token(mod_ref[...][:, 2:3, :])
        xo_ref[...] = x_ref[...] + gt1 * _dot(merged.astype(BF16), wo_ref[...])


def _mixer_sample_call(l, depth, x2, mod_all, kc_all, vc_all, p, n_batch, n_tok, carried):
    n_rows = n_batch * n_tok
    cache_len = kc_all.shape[-1]
    cache = pl.BlockSpec((None, SAMPLE_BG, N_HEADS, HEAD_DIM, cache_len), lambda s: (l, s, 0, 0, 0))
    full = lambda shape: pl.BlockSpec(shape, lambda s: (0,) * len(shape))
    state = lambda width: pl.BlockSpec((None, n_rows, width), lambda s: (l, 0, 0))
    in_specs = [
        _const_spec((n_rows, D_MODEL)),
        _layer_spec(l, (n_batch, 6, D_MODEL)),
        _layer_spec(l, (1, D_MODEL)),
        _layer_spec(l, (D_MODEL, D_IN)),
        _layer_spec(l, (1, D_A)),
        _layer_spec(l, (1, D_A)),
        _const_spec((MXU_DIM, MXU_DIM)),
        _layer_spec(l, (1, D_B)),
        _layer_spec(l, (N_HEADS, REL_SPAN)),
        cache,
        cache,
        _layer_spec(l, (N_GROUPS, n_tok, n_tok, GROUP_DIM)),
        _layer_spec(l, (N_GROUPS, n_tok, GROUP_DIM)),
        _layer_spec(l, (D_A, D_MODEL)),
        _layer_spec(l, (D_B, D_MODEL)),
        _layer_spec(l, (D_MODEL, D_MODEL)),
    ]
    return pl.pallas_call(
        functools.partial(_mixer_sample_kernel, n_carried=len(carried), n_batch=n_batch, n_tok=n_tok,
                          cache_len=cache_len),
        grid=(n_batch // SAMPLE_BG,),
        input_output_aliases={len(in_specs) + i: 1 + i for i in range(len(carried))},
        in_specs=in_specs + [pl.BlockSpec(memory_space=pl.ANY)] * len(carried),
        out_specs=[full((n_rows, D_MODEL)), state(D_A), state(D_A), state(D_B)],
        out_shape=[
            jax.ShapeDtypeStruct((n_rows, D_MODEL), F32),
            jax.ShapeDtypeStruct((depth, n_rows, D_A), F32),
            jax.ShapeDtypeStruct((depth, n_rows, D_A), F32),
            jax.ShapeDtypeStruct((depth, n_rows, D_B), F32),
        ],
        scratch_shapes=[
            pltpu.VMEM((n_rows, D_A), BF16),
            pltpu.VMEM((n_rows, D_A), BF16),
            pltpu.VMEM((n_rows, D_A), BF16),
            pltpu.VMEM((n_rows, D_A), F32),
            pltpu.VMEM((N_GROUPS, n_rows, GROUP_DIM), F32),
            pltpu.VMEM((N_GROUPS, n_rows, GROUP_DIM), F32),
            pltpu.VMEM((N_GROUPS, n_rows, GROUP_DIM), F32),
            pltpu.VMEM((N_HEADS * n_tok, cache_len + LANES), F32),
        ],
        compiler_params=pltpu.CompilerParams(
            dimension_semantics=("arbitrary",), vmem_limit_bytes=VMEM_LIMIT),
        name="mixer_sample",
    )(x2, mod_all, p["n1g"], p["win"], p["qg"], p["kg"], p["e_mat"], p["vg"], p["rel"],
      kc_all, vc_all, p["wsp_s"], p["bsp_s"], p["woa"], p["wob"], p["wo"], *carried)


def _ffn_sample_kernel(x_ref, mod_ref, n2g_ref, wfi_ref, cw_ref, cb_ref, cc_ref, wfo_ref,
                       xo_ref, cs_ref, *, n_batch, n_tok):
    n_rows = n_batch * n_tok

    def per_token(m):
        return jnp.broadcast_to(m, (n_batch, n_tok, m.shape[-1])).reshape(n_rows, m.shape[-1])

    mod = mod_ref[...]
    cc = cc_ref[...]
    row = lax.broadcasted_iota(jnp.int32, (n_rows, 1), 0) % n_tok
    xo, gs = _conv_ffn(x_ref[...], per_token(mod[:, 3:4, :]), per_token(mod[:, 4:5, :]), per_token(mod[:, 5:6, :]),
                       per_token(cc[:, 1:2, :]), per_token(cc[:, 0:1, :]), row,
                       n2g_ref, wfi_ref, cw_ref, cb_ref, wfo_ref)
    xo_ref[...] = xo
    for g, (c0, c1) in zip(gs, FF_CHUNKS):
        cs_ref[:, :, c0:c1] = g.reshape(n_batch, n_tok, c1 - c0)[:, n_tok - SUBLANES:, :]


def _ffn_sample_call(l, x2, mod_all, cc_all, p, n_batch, n_tok):
    n_rows = n_batch * n_tok
    return pl.pallas_call(
        functools.partial(_ffn_sample_kernel, n_batch=n_batch, n_tok=n_tok),
        grid=(1,),
        in_specs=[
            _const_spec((n_rows, D_MODEL)),
            _layer_spec(l, (n_batch, 6, D_MODEL)),
            _layer_spec(l, (1, D_MODEL)),
            _layer_spec(l, (D_MODEL, 2 * D_FF)),
            _layer_spec(l, (CONV_W, D_FF)),
            _layer_spec(l, (1, D_FF)),
            _layer_spec(l, (n_batch, CONV_W - 1, D_FF)),
            _layer_spec(l, (D_FF, D_MODEL)),
        ],
        out_specs=[
            pl.BlockSpec((n_rows, D_MODEL), lambda c: (0, 0)),
            pl.BlockSpec((n_batch, SUBLANES, D_FF), lambda c: (0, 0, 0)),
        ],
        out_shape=[
            jax.ShapeDtypeStruct((n_rows, D_MODEL), F32),
            jax.ShapeDtypeStruct((n_batch, SUBLANES, D_FF), F32),
        ],
        compiler_params=pltpu.CompilerParams(
            dimension_semantics=("arbitrary",), vmem_limit_bytes=VMEM_LIMIT),
        name="ffn_sample",
    )(x2, mod_all, p["n2g"], p["wfi"], p["cw"], p["cb"], cc_all, p["wfo"])


def _rel_vector(rel_bias):
    far = rel_bias[..., 2 * REL_CLIP:]
    lead = ATTN_REACH - REL_CLIP
    return jnp.concatenate([
        jnp.broadcast_to(far, (*rel_bias.shape[:-1], lead)),
        rel_bias[..., ::-1],
        jnp.broadcast_to(far, (*rel_bias.shape[:-1], REL_SPAN - lead - (2 * REL_CLIP + 1))),
    ], axis=-1)


def kernel(x_prompt, x_sample, cache_attn_k, cache_attn_v, cache_ffn_conv, c_prompt, c_sample, norm1_g, norm2_g, w_ada, b_ada, w_in, q_norm_g, k_norm_g, rel_bias, v_norm_g, w_spatial, b_spatial, w_out_a, w_out_b, w_out, w_ffn_in, ffn_conv_w, ffn_conv_b, w_ffn_out):
    depth = w_in.shape[0]
    B, S, _ = x_prompt.shape
    NB, NT, _ = x_sample.shape
    cache_len = cache_attn_k.shape[2]
    assert S % FFN_TILE_T == 0 and S % TILE_T == 0 and TILE_T % Q_BLOCK == 0 and TILE_T == ATTN_REACH and S >= ATTN_REACH
    assert NB % SAMPLE_BG == 0 and NT % SUBLANES == 0 and NT <= MLP_CHUNK and N_HEADS * NT == LANES
    assert cache_len == ATTN_REACH and rel_bias.shape[-1] == 2 * REL_CLIP + 1

    rows = NB + B
    rows_pad = -(-rows // SUBLANES) * SUBLANES
    c_all = jnp.concatenate([c_sample, c_prompt, jnp.zeros((rows_pad - rows, D_MODEL), F32)], axis=0)
    mod_all = _ada_call(c_all, w_ada, b_ada).reshape(depth, rows_pad, 6, D_MODEL)

    head_of = jnp.arange(MXU_DIM) // HEAD_DIM
    tril = jnp.tril(jnp.ones((NT, NT), F32))
    ws_small = w_spatial[:, :, :NT, :NT] * tril
    params = {
        "n1g": norm1_g[:, None, :], "n2g": norm2_g[:, None, :],
        "win": w_in.astype(BF16), "woa": w_out_a.astype(BF16), "wob": w_out_b.astype(BF16),
        "wo": w_out.astype(BF16), "wfi": w_ffn_in.astype(BF16), "wfo": w_ffn_out.astype(BF16),
        "qg": jnp.tile(q_norm_g, (1, N_HEADS))[:, None, :], "kg": jnp.tile(k_norm_g, (1, N_HEADS))[:, None, :],
        "vg": v_norm_g[:, None, :],
        "e_mat": jnp.where(head_of[:, None] == head_of[None, :], 1.0 / HEAD_DIM, 0.0).astype(BF16),
        "rel": _rel_vector(rel_bias),
        "wsp": w_spatial,
        "bsp_b": jnp.broadcast_to(b_spatial[:, :, :, None], (depth, N_GROUPS, MLP_CHUNK, GROUP_DIM)),
        "wsp_s": jnp.broadcast_to(ws_small[..., None], (depth, N_GROUPS, NT, NT, GROUP_DIM)),
        "bsp_s": jnp.broadcast_to(b_spatial[:, :, :NT, None], (depth, N_GROUPS, NT, GROUP_DIM)),
        "cw": ffn_conv_w, "cb": ffn_conv_b[:, None, :],
    }
    kc_all = cache_attn_k.transpose(0, 1, 3, 4, 2)
    vc_all = cache_attn_v.transpose(0, 1, 3, 4, 2)

    xp = x_prompt
    xs = x_sample.reshape(NB * NT, D_MODEL)
    conv_p, conv_s = [], []
    prompt_state, sample_state = (), ()
    for l in range(depth):
        xp, *prompt_state = _mixer_prompt_call(l, depth, xp, mod_all, NB, params, tuple(prompt_state))
        xp, cp = _ffn_prompt_call(l, xp, mod_all, NB, params)
        xs, *sample_state = _mixer_sample_call(l, depth, xs, mod_all, kc_all, vc_all, params, NB, NT,
                                               tuple(sample_state))
        xs, cs = _ffn_sample_call(l, xs, mod_all, cache_ffn_conv, params, NB, NT)
        conv_p.append(cp[:, SUBLANES - (CONV_W - 1):])
        conv_s.append(cs[:, SUBLANES - (CONV_W - 1):])
    new_k_prompt, new_v_prompt = (
        s.reshape(depth, B, N_HEADS, HEAD_DIM, ATTN_REACH).transpose(0, 1, 4, 2, 3) for s in prompt_state)
    ks, vs, sv = sample_state
    return (xp, xs.reshape(NB, NT, D_MODEL), new_k_prompt, new_v_prompt, jnp.stack(conv_p),
            ks.reshape(depth, NB, NT, N_HEADS, HEAD_DIM), vs.reshape(depth, NB, NT, N_HEADS, HEAD_DIM),
            sv.reshape(depth, NB, NT, N_GROUPS, GROUP_DIM), jnp.stack(conv_s))
```

```python
import functools

import jax
import jax.numpy as jnp
from jax import lax
from jax.experimental import pallas as pl
from jax.experimental.pallas import tpu as pltpu

F32 = jnp.float32
BF16 = jnp.bfloat16

D_MODEL = 1024
CHUNK = 64
ATTN_REACH = 512
N_HEADS = 8
HEAD_DIM = 64
D_A = N_HEADS * HEAD_DIM
REL_CLIP = 128
MLP_CHUNK = 128
N_GROUPS = 4
D_B = 512
GROUP_DIM = D_B // N_GROUPS
D_FF = 2816
CONV_W = 3
EPS = 1e-6

LANES = 128
SUBLANES = 8
MXU_DIM = 256
N_PAIRS = N_HEADS // 2
Q_BLOCK = 2 * CHUNK
KV_WINDOW = ATTN_REACH + Q_BLOCK
REL_SPAN = KV_WINDOW + Q_BLOCK
TILE_T = 512
FFN_TILE_T = 512
HIST = ATTN_REACH
FF_CHUNKS = ((0, 1024), (1024, 2048), (2048, 2560), (2560, D_FF))
SAMPLE_BG = 4
SAMPLE_MIN_STEPS = 8
VMEM_LIMIT = 56 * 1024 * 1024
_Q0, _K0, _V0, _B0, _G0 = 0, D_A, 2 * D_A, 3 * D_A, 3 * D_A + 2 * D_B
D_IN = _G0 + 2 * D_MODEL


def _dot(a, b):
    return jnp.dot(a, b, preferred_element_type=F32)


def _dot_nt(a, b):
    return lax.dot_general(a, b, (((1,), (1,)), ((), ())), preferred_element_type=F32)


def _dot_tn(a, b):
    return lax.dot_general(a, b, (((0,), (0,)), ((), ())), preferred_element_type=F32)


def _rms(x, g):
    return (x * lax.rsqrt(jnp.mean(x * x, axis=-1, keepdims=True) + EPS)) * g


def _head_rms(a, g, e_ref):
    sq = a * a
    hi = sq.astype(BF16)
    lo = (sq - hi.astype(F32)).astype(BF16)
    e = e_ref[...]
    w = e.shape[0]
    ms = jnp.concatenate([_dot(hi[:, c:c + w], e) + _dot(lo[:, c:c + w], e) for c in range(0, a.shape[1], w)],
                         axis=1)
    return (a * lax.rsqrt(ms + EPS)) * g


def _rel_rows(rel_ref, h, n_rows):
    wb = jnp.broadcast_to(rel_ref[h:h + 1, :], (n_rows, REL_SPAN))
    return pltpu.roll(wb, 0, axis=1, stride=1, stride_axis=0)


def _const_spec(shape):
    n = len(shape)
    return pl.BlockSpec(shape, lambda *_: (0,) * n, pipeline_mode=pl.Buffered(1))


def _layer_spec(l, shape):
    n = len(shape)
    return pl.BlockSpec((None, *shape), lambda *_: (l,) + (0,) * n, pipeline_mode=pl.Buffered(1))


def _ada_kernel(c_ref, w_ref, b_ref, o_ref):
    s = jax.nn.silu(c_ref[...]).astype(BF16)
    o_ref[...] = _dot(s, w_ref[...].astype(BF16)) + b_ref[...]


def _ada_call(c_all, w_ada, b_ada):
    depth = w_ada.shape[0]
    rows = c_all.shape[0]
    n_out = w_ada.shape[2]
    bn = 1536
    return pl.pallas_call(
        _ada_kernel,
        grid=(depth, n_out // bn),
        in_specs=[
            pl.BlockSpec((rows, D_MODEL), lambda l, n: (0, 0)),
            pl.BlockSpec((None, D_MODEL, bn), lambda l, n: (l, 0, n)),
            pl.BlockSpec((None, 1, bn), lambda l, n: (l, 0, n)),
        ],
        out_specs=pl.BlockSpec((None, rows, bn), lambda l, n: (l, 0, n)),
        out_shape=jax.ShapeDtypeStruct((depth, rows, n_out), F32),
        compiler_params=pltpu.CompilerParams(
            dimension_semantics=("arbitrary", "arbitrary"), vmem_limit_bytes=VMEM_LIMIT),
        name="ada_mod",
    )(c_all, w_ada, b_ada.reshape(depth, 1, n_out))


def _mixer_prompt_kernel(x_ref, mod_ref, n1g_ref, win_ref, qg_ref, kg_ref, e_ref, vg_ref, rel_ref,
                         wsp_ref, bsp_ref, woa_ref, wob_ref, wo_ref, *rest, n_carried):
    (xo_ref, kst_ref, vst_ref,
     kbuf, vtbuf, oat_scr, ob_scr, bias_scr, st_scr, e_scr) = rest[n_carried:]
    b = pl.program_id(0)
    t = pl.program_id(1)
    T = TILE_T

    @pl.when((b == 0) & (t == 0))
    def _():
        qi = lax.broadcasted_iota(jnp.int32, (Q_BLOCK, KV_WINDOW), 0) // CHUNK
        ki = lax.broadcasted_iota(jnp.int32, (Q_BLOCK, KV_WINDOW), 1) // CHUNK
        band = (ki >= qi) & (ki <= qi + ATTN_REACH // CHUNK)
        for h in range(N_HEADS):
            tab = jnp.where(band, _rel_rows(rel_ref, h, Q_BLOCK)[:, 0:KV_WINDOW], -jnp.inf)
            bias_scr[h // 2, 0:KV_WINDOW, (h % 2) * Q_BLOCK:(h % 2 + 1) * Q_BLOCK] = tab.T
        bias_scr[:, KV_WINDOW:, :] = jnp.full((N_PAIRS, Q_BLOCK, 2 * Q_BLOCK), -jnp.inf, F32)

    @pl.when(t == 0)
    def _():
        kbuf[0:HIST, :] = jnp.zeros((HIST, D_A), BF16)
        vtbuf[:, 0:HIST] = jnp.zeros((D_A, HIST), BF16)

    @pl.when(t > 0)
    def _():
        for r in range(0, HIST, T):
            kbuf[r:r + T, :] = kbuf[r + T:r + 2 * T, :]
            vtbuf[:, r:r + T] = vtbuf[:, r + T:r + 2 * T]

    mod = mod_ref[...]
    sh1, sc1, gt1 = mod[0:1], mod[1:2], mod[2:3]

    x = x_ref[...]
    h = (_rms(x, n1g_ref[...]) * (1.0 + sc1) + sh1).astype(BF16)

    q = _head_rms(_dot(h, win_ref[:, _Q0:_Q0 + D_A]), qg_ref[...], e_ref)
    k = _head_rms(_dot(h, win_ref[:, _K0:_K0 + D_A]), kg_ref[...], e_ref)
    v = _dot(h, win_ref[:, _V0:_V0 + D_A])
    vt = v.T
    kbuf[HIST:HIST + T, :] = k.astype(BF16)
    vtbuf[:, HIST:HIST + T] = vt.astype(BF16)

    kst_ref[...] = k.T
    vst_ref[...] = vt

    qb = (q * (HEAD_DIM ** -0.5)).astype(BF16)

    lane = lax.broadcasted_iota(jnp.int32, (1, LANES), 1)
    low_half = lane < HEAD_DIM

    n_qb = T // Q_BLOCK

    def scores(p, j, slot):
        r0, c0 = j * Q_BLOCK, p * LANES
        qp = qb[r0:r0 + Q_BLOCK, c0:c0 + LANES]
        q2 = jnp.concatenate([jnp.where(low_half, qp, jnp.zeros_like(qp)),
                              jnp.where(low_half, jnp.zeros_like(qp), qp)], axis=0)
        st = _dot_nt(kbuf[r0:r0 + KV_WINDOW, c0:c0 + LANES], q2)
        first_valid = HIST - t * T - r0
        m = None
        for r in range(0, KV_WINDOW, Q_BLOCK):
            src = jnp.where(first_valid > r, KV_WINDOW, r) if r < HIST else r
            blk = st[r:r + Q_BLOCK] + bias_scr[p, pl.ds(pl.multiple_of(src, Q_BLOCK), Q_BLOCK), :]
            st_scr[slot, j, r:r + Q_BLOCK, :] = blk
            bm = jnp.max(blk, axis=0, keepdims=True)
            m = bm if m is None else jnp.maximum(m, bm)
        return m

    def weights(j, slot, m):
        e = jnp.exp(st_scr[slot, j] - m)
        e_scr[slot, j] = e.astype(BF16)
        return jnp.sum(e, axis=0, keepdims=True)

    def outputs(p, j, slot, total):
        r0, c0 = j * Q_BLOCK, p * LANES
        ot = _dot(vtbuf[c0:c0 + LANES, r0:r0 + KV_WINDOW], e_scr[slot, j]) / total
        oat_scr[c0:c0 + HEAD_DIM, r0:r0 + Q_BLOCK] = ot[0:HEAD_DIM, 0:Q_BLOCK]
        oat_scr[c0 + HEAD_DIM:c0 + LANES, r0:r0 + Q_BLOCK] = ot[HEAD_DIM:, Q_BLOCK:]

    side = {}

    def project(name, act, c0):
        side[name] = act(_dot(h, win_ref[:, c0:c0 + MXU_DIM]))

    side_work = ([functools.partial(project, ("b", i), jax.nn.gelu, _B0 + i * MXU_DIM)
                  for i in range(2 * D_B // MXU_DIM)]
                 + [functools.partial(project, ("g", i), jax.nn.sigmoid, _G0 + i * MXU_DIM)
                    for i in range(2 * D_MODEL // MXU_DIM)])
    maxima = [scores(0, j, 0) for j in range(n_qb)]
    for p in range(N_PAIRS):
        slot = p % 2
        if p + 1 < N_PAIRS:
            next_maxima = [scores(p + 1, j, 1 - slot) for j in range(n_qb)]
        for work in side_work[len(side_work) * p // N_PAIRS:len(side_work) * (p + 1) // N_PAIRS]:
            work()
        totals = [weights(j, slot, maxima[j]) for j in range(n_qb)]
        for j in range(n_qb):
            outputs(p, j, slot, totals[j])
        maxima = next_maxima
    n_b, n_g = D_B // MXU_DIM, D_MODEL // MXU_DIM
    ub = jnp.concatenate([side["b", i] for i in range(n_b)], axis=1)
    vbn = _rms(jnp.concatenate([side["b", n_b + i] for i in range(n_b)], axis=1), vg_ref[...]).astype(BF16)
    ga = jnp.concatenate([side["g", i] for i in range(n_g)], axis=1)
    gb = jnp.concatenate([side["g", n_g + i] for i in range(n_g)], axis=1)
    row_i = lax.broadcasted_iota(jnp.int32, (MLP_CHUNK, MLP_CHUNK), 0)
    col_i = lax.broadcasted_iota(jnp.int32, (MLP_CHUNK, MLP_CHUNK), 1)
    for g in range(N_GROUPS):
        wc = jnp.where(row_i >= col_i, wsp_ref[g], 0.0).astype(BF16)
        g0 = g * GROUP_DIM
        for c in range(T // MLP_CHUNK):
            r0 = c * MLP_CHUNK
            mix = _dot(wc, vbn[r0:r0 + MLP_CHUNK, g0:g0 + GROUP_DIM]) + bsp_ref[g]
            ob_scr[r0:r0 + MLP_CHUNK, g0:g0 + GROUP_DIM] = ub[r0:r0 + MLP_CHUNK, g0:g0 + GROUP_DIM] * mix

    merged = (ga * _dot_tn(oat_scr[...].astype(BF16), woa_ref[...])
              + gb * _dot(ob_scr[...].astype(BF16), wob_ref[...]))
    xo_ref[...] = x + gt1 * _dot(merged.astype(BF16), wo_ref[...])


def _mixer_prompt_call(l, depth, x, mod_all, mod_row0, p, carried):
    B, S, _ = x.shape
    T = TILE_T
    n_t = S // T
    tok = pl.BlockSpec((None, T, D_MODEL), lambda b, t: (b, t, 0))
    state = pl.BlockSpec((None, None, D_A, ATTN_REACH), lambda b, t: (l, b, 0, 0))
    in_specs = [
        tok,
        pl.BlockSpec((None, None, 6, D_MODEL), lambda b, t: (l, mod_row0 + b, 0, 0)),
        _layer_spec(l, (1, D_MODEL)),
        _layer_spec(l, (D_MODEL, D_IN)),
        _layer_spec(l, (1, D_A)),
        _layer_spec(l, (1, D_A)),
        _const_spec((MXU_DIM, MXU_DIM)),
        _layer_spec(l, (1, D_B)),
        _layer_spec(l, (N_HEADS, REL_SPAN)),
        _layer_spec(l, (N_GROUPS, MLP_CHUNK, MLP_CHUNK)),
        _layer_spec(l, (N_GROUPS, MLP_CHUNK, GROUP_DIM)),
        _layer_spec(l, (D_A, D_MODEL)),
        _layer_spec(l, (D_B, D_MODEL)),
        _layer_spec(l, (D_MODEL, D_MODEL)),
    ]
    return pl.pallas_call(
        functools.partial(_mixer_prompt_kernel, n_carried=len(carried)),
        grid=(B, n_t),
        input_output_aliases={len(in_specs) + i: 1 + i for i in range(len(carried))},
        in_specs=in_specs + [pl.BlockSpec(memory_space=pl.ANY)] * len(carried),
        out_specs=[tok, state, state],
        out_shape=[
            jax.ShapeDtypeStruct((B, S, D_MODEL), F32),
            jax.ShapeDtypeStruct((depth, B, D_A, ATTN_REACH), F32),
            jax.ShapeDtypeStruct((depth, B, D_A, ATTN_REACH), F32),
        ],
        scratch_shapes=[
            pltpu.VMEM((HIST + T, D_A), BF16),
            pltpu.VMEM((D_A, HIST + T), BF16),
            pltpu.VMEM((D_A, T), F32),
            pltpu.VMEM((T, D_B), F32),
            pltpu.VMEM((N_PAIRS, KV_WINDOW + Q_BLOCK, 2 * Q_BLOCK), F32),
            pltpu.VMEM((2, T // Q_BLOCK, KV_WINDOW, 2 * Q_BLOCK), F32),
            pltpu.VMEM((2, T // Q_BLOCK, KV_WINDOW, 2 * Q_BLOCK), BF16),
        ],
        compiler_params=pltpu.CompilerParams(
            dimension_semantics=("arbitrary", "arbitrary"), vmem_limit_bytes=VMEM_LIMIT),
        name="mixer_prompt",
    )(x, mod_all, p["n1g"], p["win"], p["qg"], p["kg"], p["e_mat"], p["vg"], p["rel"],
      p["wsp"], p["bsp_b"], p["woa"], p["wob"], p["wo"], *carried)


def _conv_gate(g, u, prev1, prev2, row, cw_ref, cb_ref, c0, c1):
    gm1 = jnp.where(row == 0, prev1, pltpu.roll(g, 1, axis=0))
    gm2 = jnp.where(row == 0, prev2, jnp.where(row == 1, prev1, pltpu.roll(g, 2, axis=0)))
    gc = cb_ref[:, c0:c1] + cw_ref[0:1, c0:c1] * gm2
    gc = gc + cw_ref[1:2, c0:c1] * gm1
    gc = gc + cw_ref[2:3, c0:c1] * g
    return jax.nn.gelu(gc) * u


def _conv_ffn(x, sh2, sc2, gt2, prev1, prev2, row, n2g_ref, wfi_ref, cw_ref, cb_ref, wfo_ref):
    h2 = (_rms(x, n2g_ref[...]) * (1.0 + sc2) + sh2).astype(BF16)

    def up_project(c0, c1):
        return _dot(h2, wfi_ref[:, c0:c1]), _dot(h2, wfi_ref[:, D_FF + c0:D_FF + c1])

    f = jnp.zeros(x.shape, F32)
    gs = []
    ahead = up_project(*FF_CHUNKS[0])
    for i, (c0, c1) in enumerate(FF_CHUNKS):
        g, u = ahead
        if i + 1 < len(FF_CHUNKS):
            ahead = up_project(*FF_CHUNKS[i + 1])
        act = _conv_gate(g, u, prev1[:, c0:c1], prev2[:, c0:c1], row, cw_ref, cb_ref, c0, c1)
        f = f + _dot(act.astype(BF16), wfo_ref[c0:c1, :])
        gs.append(g)
    return x + gt2 * f, gs


def _ffn_prompt_kernel(x_ref, mod_ref, n2g_ref, wfi_ref, cw_ref, cb_ref, wfo_ref,
                       xo_ref, cs_ref, carry):
    t = pl.program_id(1)
    T = FFN_TILE_T
    mod = mod_ref[...]

    @pl.when(t == 0)
    def _():
        carry[...] = jnp.zeros(carry.shape, F32)

    row = lax.broadcasted_iota(jnp.int32, (T, 1), 0)
    prev = carry[...]
    xo, gs = _conv_ffn(x_ref[...], mod[3:4], mod[4:5], mod[5:6], prev[SUBLANES - 1:], prev[SUBLANES - 2:SUBLANES - 1],
                       row, n2g_ref, wfi_ref, cw_ref, cb_ref, wfo_ref)
    xo_ref[...] = xo
    tail = jnp.concatenate([g[T - SUBLANES:T, :] for g in gs], axis=1)
    carry[...] = tail
    cs_ref[...] = tail


def _ffn_prompt_call(l, x, mod_all, mod_row0, p):
    B, S, _ = x.shape
    T = FFN_TILE_T
    tok = pl.BlockSpec((None, T, D_MODEL), lambda b, t: (b, t, 0))
    return pl.pallas_call(
        _ffn_prompt_kernel,
        grid=(B, S // T),
        in_specs=[
            tok,
            pl.BlockSpec((None, None, 6, D_MODEL), lambda b, t: (l, mod_row0 + b, 0, 0)),
            _layer_spec(l, (1, D_MODEL)),
            _layer_spec(l, (D_MODEL, 2 * D_FF)),
            _layer_spec(l, (CONV_W, D_FF)),
            _layer_spec(l, (1, D_FF)),
            _layer_spec(l, (D_FF, D_MODEL)),
        ],
        out_specs=[tok, pl.BlockSpec((None, SUBLANES, D_FF), lambda b, t: (b, 0, 0))],
        out_shape=[
            jax.ShapeDtypeStruct((B, S, D_MODEL), F32),
            jax.ShapeDtypeStruct((B, SUBLANES, D_FF), F32),
        ],
        scratch_shapes=[pltpu.VMEM((SUBLANES, D_FF), F32)],
        compiler_params=pltpu.CompilerParams(
            dimension_semantics=("arbitrary", "arbitrary"), vmem_limit_bytes=VMEM_LIMIT),
        name="ffn_prompt",
    )(x, mod_all, p["n2g"], p["wfi"], p["cw"], p["cb"], p["wfo"])


def _mixer_sample_kernel(x_ref, mod_ref, n1g_ref, win_ref, qg_ref, kg_ref, e_ref, vg_ref, rel_ref,
                         kc_ref, vc_ref, wsp_ref, bsp_ref, woa_ref, wob_ref, wo_ref, *rest,
                         n_carried, n_batch, n_tok, cache_len):
    (xo_ref, ks_ref, vs_ref, sv_ref,
     h_scr, q_scr, kn_scr, vn_scr, oa_scr, ub_scr, vb_scr, ob_scr, gate_scr, bias_scr) = rest[n_carried:]
    step = pl.program_id(0)
    n_rows = n_batch * n_tok

    def per_token(m):
        return jnp.broadcast_to(m, (n_batch, n_tok, m.shape[-1])).reshape(n_rows, m.shape[-1])

    @pl.when(step == 0)
    def _():
        mod = mod_ref[...]
        sh1, sc1 = per_token(mod[:, 0:1, :]), per_token(mod[:, 1:2, :])
        h = (_rms(x_ref[...], n1g_ref[...]) * (1.0 + sc1) + sh1).astype(BF16)
        h_scr[...] = h
        q = _head_rms(_dot(h, win_ref[:, _Q0:_Q0 + D_A]), qg_ref[...], e_ref)
        k = _head_rms(_dot(h, win_ref[:, _K0:_K0 + D_A]), kg_ref[...], e_ref)
        v = _dot(h, win_ref[:, _V0:_V0 + D_A])
        ks_ref[...] = k
        vs_ref[...] = v
        kn_scr[...] = k.astype(BF16)
        vn_scr[...] = v.astype(BF16)
        q_scr[...] = (q * (HEAD_DIM ** -0.5)).astype(BF16)
        for hh in range(N_HEADS):
            bias_scr[hh * n_tok:(hh + 1) * n_tok, :] = _rel_rows(rel_ref, hh, n_tok)[:, 0:cache_len + LANES]

    lane_head = lax.broadcasted_iota(jnp.int32, (n_tok, D_A), 1) // HEAD_DIM
    for i in range(SAMPLE_BG):
        r0 = pl.multiple_of((step * SAMPLE_BG + i) * n_tok, n_tok)
        qb = q_scr[pl.ds(r0, n_tok), :]
        q_bd = jnp.concatenate([jnp.where(lane_head == hh, qb, jnp.zeros_like(qb)) for hh in range(N_HEADS)],
                               axis=0)
        kn = kn_scr[pl.ds(r0, n_tok), :]
        vn = vn_scr[pl.ds(r0, n_tok), :]
        kt = kc_ref[i].reshape(D_A, cache_len).astype(BF16)
        vt = vc_ref[i].reshape(D_A, cache_len).astype(BF16)
        s_c = _dot(q_bd, kt) + bias_scr[:, 0:cache_len]
        s_n = _dot_nt(q_bd, kn) + bias_scr[:, cache_len:cache_len + n_tok]
        m = jnp.maximum(jnp.max(s_c, axis=-1, keepdims=True), jnp.max(s_n, axis=-1, keepdims=True))
        e_c = jnp.exp(s_c - m)
        e_n = jnp.exp(s_n - m)
        l = jnp.sum(e_c, axis=-1, keepdims=True) + jnp.sum(e_n, axis=-1, keepdims=True)
        o_all = (_dot_nt(e_c.astype(BF16), vt) + _dot(e_n.astype(BF16), vn)) / l
        o = jnp.zeros((n_tok, D_A), F32)
        for hh in range(N_HEADS):
            o = jnp.where(lane_head == hh, o_all[hh * n_tok:(hh + 1) * n_tok, :], o)
        oa_scr[pl.ds(r0, n_tok), :] = o

    def spatial_half(half_scr, c0):
        z = jax.nn.gelu(_dot(h_scr[...], win_ref[:, c0:c0 + D_B]))
        for g in range(N_GROUPS):
            half_scr[g] = z[:, g * GROUP_DIM:(g + 1) * GROUP_DIM]

    def spatial_mix():
        vbn = _rms(jnp.concatenate([vb_scr[g] for g in range(N_GROUPS)], axis=1), vg_ref[...])
        sv_ref[...] = vbn
        for g in range(N_GROUPS):
            vb_scr[g] = vbn[:, g * GROUP_DIM:(g + 1) * GROUP_DIM]
        for g in range(N_GROUPS):
            at_pos = [vb_scr[g, pl.ds(s, n_batch, stride=n_tok), :] for s in range(n_tok)]
            for t in range(n_tok):
                mix = bsp_ref[g, t:t + 1, :]
                for s in range(t + 1):
                    mix = mix + wsp_ref[g, t, s:s + 1, :] * at_pos[s]
                ob_scr[g, pl.ds(t, n_batch, stride=n_tok), :] = ub_scr[g, pl.ds(t, n_batch, stride=n_tok), :] * mix

    def gates(c0, width):
        gate_scr[:, c0:c0 + width] = jax.nn.sigmoid(_dot(h_scr[...], win_ref[:, _G0 + c0:_G0 + c0 + width]))

    stages = [functools.partial(spatial_half, ub_scr, _B0), functools.partial(spatial_half, vb_scr, _B0 + D_B),
              spatial_mix] + [functools.partial(gates, c0, D_B) for c0 in range(0, 2 * D_MODEL, D_B)]
    for i, stage in enumerate(stages):
        pl.when(step == i + 1)(stage)

    @pl.when(step == pl.num_programs(0) - 1)
    def _():
        ob = jnp.concatenate([ob_scr[g] for g in range(N_GROUPS)], axis=1)
        merged = (gate_scr[:, 0:D_MODEL] * _dot(oa_scr[...].astype(BF16), woa_ref[...])
                  + gate_scr[:, D_MODEL:] * _dot(ob.astype(BF16), wob_ref[...]))
        gt1 = per_token(mod_ref[...][:, 2:3, :])
        xo_ref[...] = x_ref[...] + gt1 * _dot(merged.astype(BF16), wo_ref[...])


def _mixer_sample_call(l, depth, x2, mod_all, kc_all, vc_all, p, n_batch, n_tok, carried):
    n_rows = n_batch * n_tok
    cache_len = kc_all.shape[-1]
    cache = pl.BlockSpec((None, SAMPLE_BG, N_HEADS, HEAD_DIM, cache_len), lambda s: (l, s, 0, 0, 0))
    full = lambda shape: pl.BlockSpec(shape, lambda s: (0,) * len(shape))
    state = lambda width: pl.BlockSpec((None, n_rows, width), lambda s: (l, 0, 0))
    in_specs = [
        _const_spec((n_rows, D_MODEL)),
        _layer_spec(l, (n_batch, 6, D_MODEL)),
        _layer_spec(l, (1, D_MODEL)),
        _layer_spec(l, (D_MODEL, D_IN)),
        _layer_spec(l, (1, D_A)),
        _layer_spec(l, (1, D_A)),
        _const_spec((MXU_DIM, MXU_DIM)),
        _layer_spec(l, (1, D_B)),
        _layer_spec(l, (N_HEADS, REL_SPAN)),
        cache,
        cache,
        _layer_spec(l, (N_GROUPS, n_tok, n_tok, GROUP_DIM)),
        _layer_spec(l, (N_GROUPS, n_tok, GROUP_DIM)),
        _layer_spec(l, (D_A, D_MODEL)),
        _layer_spec(l, (D_B, D_MODEL)),
        _layer_spec(l, (D_MODEL, D_MODEL)),
    ]
    return pl.pallas_call(
        functools.partial(_mixer_sample_kernel, n_carried=len(carried), n_batch=n_batch, n_tok=n_tok,
                          cache_len=cache_len),
        grid=(n_batch // SAMPLE_BG,),
        input_output_aliases={len(in_specs) + i: 1 + i for i in range(len(carried))},
        in_specs=in_specs + [pl.BlockSpec(memory_space=pl.ANY)] * len(carried),
        out_specs=[full((n_rows, D_MODEL)), state(D_A), state(D_A), state(D_B)],
        out_shape=[
            jax.ShapeDtypeStruct((n_rows, D_MODEL), F32),
            jax.ShapeDtypeStruct((depth, n_rows, D_A), F32),
            jax.ShapeDtypeStruct((depth, n_rows, D_A), F32),
            jax.ShapeDtypeStruct((depth, n_rows, D_B), F32),
        ],
        scratch_shapes=[
            pltpu.VMEM((n_rows, D_MODEL), BF16),
            pltpu.VMEM((n_rows, D_A), BF16),
            pltpu.VMEM((n_rows, D_A), BF16),
            pltpu.VMEM((n_rows, D_A), BF16),
            pltpu.VMEM((n_rows, D_A), F32),
            pltpu.VMEM((N_GROUPS, n_rows, GROUP_DIM), F32),
            pltpu.VMEM((N_GROUPS, n_rows, GROUP_DIM), F32),
            pltpu.VMEM((N_GROUPS, n_rows, GROUP_DIM), F32),
            pltpu.VMEM((n_rows, 2 * D_MODEL), F32),
            pltpu.VMEM((N_HEADS * n_tok, cache_len + LANES), F32),
        ],
        compiler_params=pltpu.CompilerParams(
            dimension_semantics=("arbitrary",), vmem_limit_bytes=VMEM_LIMIT),
        name="mixer_sample",
    )(x2, mod_all, p["n1g"], p["win"], p["qg"], p["kg"], p["e_mat"], p["vg"], p["rel"],
      kc_all, vc_all, p["wsp_s"], p["bsp_s"], p["woa"], p["wob"], p["wo"], *carried)


def _ffn_sample_kernel(x_ref, mod_ref, n2g_ref, wfi_ref, cw_ref, cb_ref, cc_ref, wfo_ref,
                       xo_ref, cs_ref, *, n_batch, n_tok):
    n_rows = n_batch * n_tok

    def per_token(m):
        return jnp.broadcast_to(m, (n_batch, n_tok, m.shape[-1])).reshape(n_rows, m.shape[-1])

    mod = mod_ref[...]
    cc = cc_ref[...]
    row = lax.broadcasted_iota(jnp.int32, (n_rows, 1), 0) % n_tok
    xo, gs = _conv_ffn(x_ref[...], per_token(mod[:, 3:4, :]), per_token(mod[:, 4:5, :]), per_token(mod[:, 5:6, :]),
                       per_token(cc[:, 1:2, :]), per_token(cc[:, 0:1, :]), row,
                       n2g_ref, wfi_ref, cw_ref, cb_ref, wfo_ref)
    xo_ref[...] = xo
    for g, (c0, c1) in zip(gs, FF_CHUNKS):
        cs_ref[:, :, c0:c1] = g.reshape(n_batch, n_tok, c1 - c0)[:, n_tok - SUBLANES:, :]


def _ffn_sample_call(l, x2, mod_all, cc_all, p, n_batch, n_tok):
    n_rows = n_batch * n_tok
    return pl.pallas_call(
        functools.partial(_ffn_sample_kernel, n_batch=n_batch, n_tok=n_tok),
        grid=(1,),
        in_specs=[
            _const_spec((n_rows, D_MODEL)),
            _layer_spec(l, (n_batch, 6, D_MODEL)),
            _layer_spec(l, (1, D_MODEL)),
            _layer_spec(l, (D_MODEL, 2 * D_FF)),
            _layer_spec(l, (CONV_W, D_FF)),
            _layer_spec(l, (1, D_FF)),
            _layer_spec(l, (n_batch, CONV_W - 1, D_FF)),
            _layer_spec(l, (D_FF, D_MODEL)),
        ],
        out_specs=[
            pl.BlockSpec((n_rows, D_MODEL), lambda c: (0, 0)),
            pl.BlockSpec((n_batch, SUBLANES, D_FF), lambda c: (0, 0, 0)),
        ],
        out_shape=[
            jax.ShapeDtypeStruct((n_rows, D_MODEL), F32),
            jax.ShapeDtypeStruct((n_batch, SUBLANES, D_FF), F32),
        ],
        compiler_params=pltpu.CompilerParams(
            dimension_semantics=("arbitrary",), vmem_limit_bytes=VMEM_LIMIT),
        name="ffn_sample",
    )(x2, mod_all, p["n2g"], p["wfi"], p["cw"], p["cb"], cc_all, p["wfo"])


def _rel_vector(rel_bias):
    far = rel_bias[..., 2 * REL_CLIP:]
    lead = ATTN_REACH - REL_CLIP
    return jnp.concatenate([
        jnp.broadcast_to(far, (*rel_bias.shape[:-1], lead)),
        rel_bias[..., ::-1],
        jnp.broadcast_to(far, (*rel_bias.shape[:-1], REL_SPAN - lead - (2 * REL_CLIP + 1))),
    ], axis=-1)


def kernel(x_prompt, x_sample, cache_attn_k, cache_attn_v, cache_ffn_conv, c_prompt, c_sample, norm1_g, norm2_g, w_ada, b_ada, w_in, q_norm_g, k_norm_g, rel_bias, v_norm_g, w_spatial, b_spatial, w_out_a, w_out_b, w_out, w_ffn_in, ffn_conv_w, ffn_conv_b, w_ffn_out):
    depth = w_in.shape[0]
    B, S, _ = x_prompt.shape
    NB, NT, _ = x_sample.shape
    cache_len = cache_attn_k.shape[2]
    assert S % FFN_TILE_T == 0 and S % TILE_T == 0 and TILE_T % Q_BLOCK == 0 and TILE_T == ATTN_REACH and S >= ATTN_REACH
    assert NB % SAMPLE_BG == 0 and NB // SAMPLE_BG >= SAMPLE_MIN_STEPS and NT % SUBLANES == 0 and NT <= MLP_CHUNK and N_HEADS * NT == LANES
    assert cache_len == ATTN_REACH and rel_bias.shape[-1] == 2 * REL_CLIP + 1

    rows = NB + B
    rows_pad = -(-rows // SUBLANES) * SUBLANES
    c_all = jnp.concatenate([c_sample, c_prompt, jnp.zeros((rows_pad - rows, D_MODEL), F32)], axis=0)
    mod_all = _ada_call(c_all, w_ada, b_ada).reshape(depth, rows_pad, 6, D_MODEL)

    head_of = jnp.arange(MXU_DIM) // HEAD_DIM
    tril = jnp.tril(jnp.ones((NT, NT), F32))
    ws_small = w_spatial[:, :, :NT, :NT] * tril
    params = {
        "n1g": norm1_g[:, None, :], "n2g": norm2_g[:, None, :],
        "win": w_in.astype(BF16), "woa": w_out_a.astype(BF16), "wob": w_out_b.astype(BF16),
        "wo": w_out.astype(BF16), "wfi": w_ffn_in.astype(BF16), "wfo": w_ffn_out.astype(BF16),
        "qg": jnp.tile(q_norm_g, (1, N_HEADS))[:, None, :], "kg": jnp.tile(k_norm_g, (1, N_HEADS))[:, None, :],
        "vg": v_norm_g[:, None, :],
        "e_mat": jnp.where(head_of[:, None] == head_of[None, :], 1.0 / HEAD_DIM, 0.0).astype(BF16),
        "rel": _rel_vector(rel_bias),
        "wsp": w_spatial,
        "bsp_b": jnp.broadcast_to(b_spatial[:, :, :, None], (depth, N_GROUPS, MLP_CHUNK, GROUP_DIM)),
        "wsp_s": jnp.broadcast_to(ws_small[..., None], (depth, N_GROUPS, NT, NT, GROUP_DIM)),
        "bsp_s": jnp.broadcast_to(b_spatial[:, :, :NT, None], (depth, N_GROUPS, NT, GROUP_DIM)),
        "cw": ffn_conv_w, "cb": ffn_conv_b[:, None, :],
    }
    kc_all = cache_attn_k.transpose(0, 1, 3, 4, 2)
    vc_all = cache_attn_v.transpose(0, 1, 3, 4, 2)

    xp = x_prompt
    xs = x_sample.reshape(NB * NT, D_MODEL)
    conv_p, conv_s = [], []
    prompt_state, sample_state = (), ()
    for l in range(depth):
        xp, *prompt_state = _mixer_prompt_call(l, depth, xp, mod_all, NB, params, tuple(prompt_state))
        xp, cp = _ffn_prompt_call(l, xp, mod_all, NB, params)
        xs, *sample_state = _mixer_sample_call(l, depth, xs, mod_all, kc_all, vc_all, params, NB, NT,
                                               tuple(sample_state))
        xs, cs = _ffn_sample_call(l, xs, mod_all, cache_ffn_conv, params, NB, NT)
        conv_p.append(cp[:, SUBLANES - (CONV_W - 1):])
        conv_s.append(cs[:, SUBLANES - (CONV_W - 1):])
    new_k_prompt, new_v_prompt = (
        s.reshape(depth, B, N_HEADS, HEAD_DIM, ATTN_REACH).transpose(0, 1, 4, 2, 3) for s in prompt_state)
    ks, vs, sv = sample_state
    return (xp, xs.reshape(NB, NT, D_MODEL), new_k_prompt, new_v_prompt, jnp.stack(conv_p),
            ks.reshape(depth, NB, NT, N_HEADS, HEAD_DIM), vs.reshape(depth, NB, NT, N_HEADS, HEAD_DIM),
            sv.reshape(depth, NB, NT, N_GROUPS, GROUP_DIM), jnp.stack(conv_s))
```

```python
import functools

import jax
import jax.numpy as jnp
from jax import lax
from jax.experimental import pallas as pl
from jax.experimental.pallas import tpu as pltpu

F32 = jnp.float32
BF16 = jnp.bfloat16

D_MODEL = 1024
CHUNK = 64
ATTN_REACH = 512
N_HEADS = 8
HEAD_DIM = 64
D_A = N_HEADS * HEAD_DIM
REL_CLIP = 128
MLP_CHUNK = 128
N_GROUPS = 4
D_B = 512
GROUP_DIM = D_B // N_GROUPS
D_FF = 2816
CONV_W = 3
EPS = 1e-6

LANES = 128
SUBLANES = 8
BF16_ROWS = 16
MXU_DIM = 256
N_PAIRS = N_HEADS // 2
Q_BLOCK = 2 * CHUNK
KV_WINDOW = ATTN_REACH + Q_BLOCK
REL_SPAN = KV_WINDOW + Q_BLOCK
TILE_T = 512
FFN_TILE_T = 512
HIST = ATTN_REACH
FF_CHUNKS = ((0, 1024), (1024, 2048), (2048, 2560), (2560, D_FF))
SAMPLE_BG = 4
SAMPLE_MIN_STEPS = 8
VMEM_LIMIT = 56 * 1024 * 1024
_Q0, _K0, _V0, _B0, _G0 = 0, D_A, 2 * D_A, 3 * D_A, 3 * D_A + 2 * D_B
D_IN = _G0 + 2 * D_MODEL


def _dot(a, b):
    return jnp.dot(a, b, preferred_element_type=F32)


def _dot_nt(a, b):
    return lax.dot_general(a, b, (((1,), (1,)), ((), ())), preferred_element_type=F32)


def _dot_tn(a, b):
    return lax.dot_general(a, b, (((0,), (0,)), ((), ())), preferred_element_type=F32)


def _rms(x, g):
    return (x * lax.rsqrt(jnp.mean(x * x, axis=-1, keepdims=True) + EPS)) * g


def _head_rms(a, g, e_ref):
    sq = a * a
    hi = sq.astype(BF16)
    lo = (sq - hi.astype(F32)).astype(BF16)
    e = e_ref[...]
    w = e.shape[0]
    ms = jnp.concatenate([_dot(hi[:, c:c + w], e) + _dot(lo[:, c:c + w], e) for c in range(0, a.shape[1], w)],
                         axis=1)
    return (a * lax.rsqrt(ms + EPS)) * g


def _rel_rows(rel_ref, h, n_rows):
    wb = jnp.broadcast_to(rel_ref[h:h + 1, :], (n_rows, REL_SPAN))
    return pltpu.roll(wb, 0, axis=1, stride=1, stride_axis=0)


def _const_spec(shape):
    n = len(shape)
    return pl.BlockSpec(shape, lambda *_: (0,) * n, pipeline_mode=pl.Buffered(1))


def _cast_specs(l, stacked, n_steps, step_of):
    in_specs, out_specs, out_shapes = [], [], []
    for a in stacked:
        rows, cols = a.shape[1:]
        blk = next(r for r in range(BF16_ROWS, rows + 1, BF16_ROWS) if rows % r == 0 and rows // r <= n_steps)
        last = rows // blk - 1
        in_specs.append(pl.BlockSpec((None, blk, cols), lambda *ids, last=last: (l, jnp.minimum(step_of(*ids), last), 0)))
        out_specs.append(pl.BlockSpec((blk, cols), lambda *ids, last=last: (jnp.minimum(step_of(*ids), last), 0)))
        out_shapes.append(jax.ShapeDtypeStruct((rows, cols), BF16))
    return in_specs, out_specs, out_shapes


def _layer_spec(l, shape):
    n = len(shape)
    return pl.BlockSpec((None, *shape), lambda *_: (l,) + (0,) * n, pipeline_mode=pl.Buffered(1))


def _ada_kernel(c_ref, w_ref, b_ref, o_ref):
    s = jax.nn.silu(c_ref[...]).astype(BF16)
    o_ref[...] = _dot(s, w_ref[...].astype(BF16)) + b_ref[...]


def _ada_call(c_all, w_ada, b_ada):
    depth = w_ada.shape[0]
    rows = c_all.shape[0]
    n_out = w_ada.shape[2]
    bn = 1536
    return pl.pallas_call(
        _ada_kernel,
        grid=(depth, n_out // bn),
        in_specs=[
            pl.BlockSpec((rows, D_MODEL), lambda l, n: (0, 0)),
            pl.BlockSpec((None, D_MODEL, bn), lambda l, n: (l, 0, n)),
            pl.BlockSpec((None, 1, bn), lambda l, n: (l, 0, n)),
        ],
        out_specs=pl.BlockSpec((None, rows, bn), lambda l, n: (l, 0, n)),
        out_shape=jax.ShapeDtypeStruct((depth, rows, n_out), F32),
        compiler_params=pltpu.CompilerParams(
            dimension_semantics=("arbitrary", "arbitrary"), vmem_limit_bytes=VMEM_LIMIT),
        name="ada_mod",
    )(c_all, w_ada, b_ada.reshape(depth, 1, n_out))


def _mixer_prompt_kernel(x_ref, mod_ref, n1g_ref, win_ref, qg_ref, kg_ref, e_ref, vg_ref, rel_ref,
                         wsp_ref, bsp_ref, woa_ref, wob_ref, wo_ref, *rest, n_cast, n_carried):
    cast_in, rest = rest[:n_cast], rest[n_cast + n_carried:]
    xo_ref, kst_ref, vst_ref = rest[:3]
    cast_out = rest[3:3 + n_cast]
    kbuf, vtbuf, oat_scr, ob_scr, bias_scr, st_scr, e_scr = rest[3 + n_cast:]
    b = pl.program_id(0)
    t = pl.program_id(1)
    T = TILE_T
    for src, dst in zip(cast_in, cast_out):
        dst[...] = src[...].astype(BF16)

    @pl.when((b == 0) & (t == 0))
    def _():
        qi = lax.broadcasted_iota(jnp.int32, (Q_BLOCK, KV_WINDOW), 0) // CHUNK
        ki = lax.broadcasted_iota(jnp.int32, (Q_BLOCK, KV_WINDOW), 1) // CHUNK
        band = (ki >= qi) & (ki <= qi + ATTN_REACH // CHUNK)
        for h in range(N_HEADS):
            tab = jnp.where(band, _rel_rows(rel_ref, h, Q_BLOCK)[:, 0:KV_WINDOW], -jnp.inf)
            bias_scr[h // 2, 0:KV_WINDOW, (h % 2) * Q_BLOCK:(h % 2 + 1) * Q_BLOCK] = tab.T
        bias_scr[:, KV_WINDOW:, :] = jnp.full((N_PAIRS, Q_BLOCK, 2 * Q_BLOCK), -jnp.inf, F32)

    @pl.when(t == 0)
    def _():
        kbuf[0:HIST, :] = jnp.zeros((HIST, D_A), BF16)
        vtbuf[:, 0:HIST] = jnp.zeros((D_A, HIST), BF16)

    @pl.when(t > 0)
    def _():
        for r in range(0, HIST, T):
            kbuf[r:r + T, :] = kbuf[r + T:r + 2 * T, :]
            vtbuf[:, r:r + T] = vtbuf[:, r + T:r + 2 * T]

    mod = mod_ref[...]
    sh1, sc1, gt1 = mod[0:1], mod[1:2], mod[2:3]

    x = x_ref[...]
    h = (_rms(x, n1g_ref[...]) * (1.0 + sc1) + sh1).astype(BF16)

    q = _head_rms(_dot(h, win_ref[:, _Q0:_Q0 + D_A]), qg_ref[...], e_ref)
    k = _head_rms(_dot(h, win_ref[:, _K0:_K0 + D_A]), kg_ref[...], e_ref)
    v = _dot(h, win_ref[:, _V0:_V0 + D_A])
    vt = v.T
    kbuf[HIST:HIST + T, :] = k.astype(BF16)
    vtbuf[:, HIST:HIST + T] = vt.astype(BF16)

    kst_ref[...] = k.T
    vst_ref[...] = vt

    qb = (q * (HEAD_DIM ** -0.5)).astype(BF16)

    lane = lax.broadcasted_iota(jnp.int32, (1, LANES), 1)
    low_half = lane < HEAD_DIM

    n_qb = T // Q_BLOCK

    def scores(p, j, slot):
        r0, c0 = j * Q_BLOCK, p * LANES
        qp = qb[r0:r0 + Q_BLOCK, c0:c0 + LANES]
        q2 = jnp.concatenate([jnp.where(low_half, qp, jnp.zeros_like(qp)),
                              jnp.where(low_half, jnp.zeros_like(qp), qp)], axis=0)
        st = _dot_nt(kbuf[r0:r0 + KV_WINDOW, c0:c0 + LANES], q2)
        first_valid = HIST - t * T - r0
        m = None
        for r in range(0, KV_WINDOW, Q_BLOCK):
            src = jnp.where(first_valid > r, KV_WINDOW, r) if r < HIST else r
            blk = st[r:r + Q_BLOCK] + bias_scr[p, pl.ds(pl.multiple_of(src, Q_BLOCK), Q_BLOCK), :]
            st_scr[slot, j, r:r + Q_BLOCK, :] = blk
            bm = jnp.max(blk, axis=0, keepdims=True)
            m = bm if m is None else jnp.maximum(m, bm)
        return m

    def weights(j, slot, m):
        e = jnp.exp(st_scr[slot, j] - m)
        e_scr[slot, j] = e.astype(BF16)
        return jnp.sum(e, axis=0, keepdims=True)

    def outputs(p, j, slot, total):
        r0, c0 = j * Q_BLOCK, p * LANES
        ot = _dot(vtbuf[c0:c0 + LANES, r0:r0 + KV_WINDOW], e_scr[slot, j]) / total
        oat_scr[c0:c0 + HEAD_DIM, r0:r0 + Q_BLOCK] = ot[0:HEAD_DIM, 0:Q_BLOCK]
        oat_scr[c0 + HEAD_DIM:c0 + LANES, r0:r0 + Q_BLOCK] = ot[HEAD_DIM:, Q_BLOCK:]

    side = {}

    def project(name, act, c0):
        side[name] = act(_dot(h, win_ref[:, c0:c0 + MXU_DIM]))

    side_work = ([functools.partial(project, ("b", i), jax.nn.gelu, _B0 + i * MXU_DIM)
                  for i in range(2 * D_B // MXU_DIM)]
                 + [functools.partial(project, ("g", i), jax.nn.sigmoid, _G0 + i * MXU_DIM)
                    for i in range(2 * D_MODEL // MXU_DIM)])
    maxima = [scores(0, j, 0) for j in range(n_qb)]
    for p in range(N_PAIRS):
        slot = p % 2
        if p + 1 < N_PAIRS:
            next_maxima = [scores(p + 1, j, 1 - slot) for j in range(n_qb)]
        for work in side_work[len(side_work) * p // N_PAIRS:len(side_work) * (p + 1) // N_PAIRS]:
            work()
        totals = [weights(j, slot, maxima[j]) for j in range(n_qb)]
        for j in range(n_qb):
            outputs(p, j, slot, totals[j])
        maxima = next_maxima
    n_b, n_g = D_B // MXU_DIM, D_MODEL // MXU_DIM
    ub = jnp.concatenate([side["b", i] for i in range(n_b)], axis=1)
    vbn = _rms(jnp.concatenate([side["b", n_b + i] for i in range(n_b)], axis=1), vg_ref[...]).astype(BF16)
    ga = jnp.concatenate([side["g", i] for i in range(n_g)], axis=1)
    gb = jnp.concatenate([side["g", n_g + i] for i in range(n_g)], axis=1)
    row_i = lax.broadcasted_iota(jnp.int32, (MLP_CHUNK, MLP_CHUNK), 0)
    col_i = lax.broadcasted_iota(jnp.int32, (MLP_CHUNK, MLP_CHUNK), 1)
    for g in range(N_GROUPS):
        wc = jnp.where(row_i >= col_i, wsp_ref[g], 0.0).astype(BF16)
        g0 = g * GROUP_DIM
        for c in range(T // MLP_CHUNK):
            r0 = c * MLP_CHUNK
            mix = _dot(wc, vbn[r0:r0 + MLP_CHUNK, g0:g0 + GROUP_DIM]) + bsp_ref[g]
            ob_scr[r0:r0 + MLP_CHUNK, g0:g0 + GROUP_DIM] = ub[r0:r0 + MLP_CHUNK, g0:g0 + GROUP_DIM] * mix

    merged = (ga * _dot_tn(oat_scr[...].astype(BF16), woa_ref[...])
              + gb * _dot(ob_scr[...].astype(BF16), wob_ref[...]))
    xo_ref[...] = x + gt1 * _dot(merged.astype(BF16), wo_ref[...])


def _mixer_prompt_call(l, depth, x, mod_all, mod_row0, p, w, to_cast, carried):
    B, S, _ = x.shape
    T = TILE_T
    n_t = S // T
    tok = pl.BlockSpec((None, T, D_MODEL), lambda b, t: (b, t, 0))
    state = pl.BlockSpec((None, None, D_A, ATTN_REACH), lambda b, t: (l, b, 0, 0))
    cast_in, cast_out, cast_shapes = _cast_specs(l, to_cast, B * n_t, lambda b, t: b * n_t + t)
    in_specs = [
        tok,
        pl.BlockSpec((None, None, 6, D_MODEL), lambda b, t: (l, mod_row0 + b, 0, 0)),
        _layer_spec(l, (1, D_MODEL)),
        _const_spec((D_MODEL, D_IN)),
        _layer_spec(l, (1, D_A)),
        _layer_spec(l, (1, D_A)),
        _const_spec((MXU_DIM, MXU_DIM)),
        _layer_spec(l, (1, D_B)),
        _layer_spec(l, (N_HEADS, REL_SPAN)),
        _layer_spec(l, (N_GROUPS, MLP_CHUNK, MLP_CHUNK)),
        _layer_spec(l, (N_GROUPS, MLP_CHUNK, GROUP_DIM)),
        _const_spec((D_A, D_MODEL)),
        _const_spec((D_B, D_MODEL)),
        _const_spec((D_MODEL, D_MODEL)),
    ]
    in_specs += cast_in
    return pl.pallas_call(
        functools.partial(_mixer_prompt_kernel, n_cast=len(to_cast), n_carried=len(carried)),
        grid=(B, n_t),
        input_output_aliases={len(in_specs) + i: 1 + i for i in range(len(carried))},
        in_specs=in_specs + [pl.BlockSpec(memory_space=pl.ANY)] * len(carried),
        out_specs=[tok, state, state] + cast_out,
        out_shape=[
            jax.ShapeDtypeStruct((B, S, D_MODEL), F32),
            jax.ShapeDtypeStruct((depth, B, D_A, ATTN_REACH), F32),
            jax.ShapeDtypeStruct((depth, B, D_A, ATTN_REACH), F32),
        ] + cast_shapes,
        scratch_shapes=[
            pltpu.VMEM((HIST + T, D_A), BF16),
            pltpu.VMEM((D_A, HIST + T), BF16),
            pltpu.VMEM((D_A, T), F32),
            pltpu.VMEM((T, D_B), F32),
            pltpu.VMEM((N_PAIRS, KV_WINDOW + Q_BLOCK, 2 * Q_BLOCK), F32),
            pltpu.VMEM((2, T // Q_BLOCK, KV_WINDOW, 2 * Q_BLOCK), F32),
            pltpu.VMEM((2, T // Q_BLOCK, KV_WINDOW, 2 * Q_BLOCK), BF16),
        ],
        compiler_params=pltpu.CompilerParams(
            dimension_semantics=("arbitrary", "arbitrary"), vmem_limit_bytes=VMEM_LIMIT),
        name="mixer_prompt",
    )(x, mod_all, p["n1g"], w["win"], p["qg"], p["kg"], p["e_mat"], p["vg"], p["rel"],
      p["wsp"], p["bsp_b"], w["woa"], w["wob"], w["wo"], *to_cast, *carried)


def _conv_gate(g, u, prev1, prev2, row, cw_ref, cb_ref, c0, c1):
    gm1 = jnp.where(row == 0, prev1, pltpu.roll(g, 1, axis=0))
    gm2 = jnp.where(row == 0, prev2, jnp.where(row == 1, prev1, pltpu.roll(g, 2, axis=0)))
    gc = cb_ref[:, c0:c1] + cw_ref[0:1, c0:c1] * gm2
    gc = gc + cw_ref[1:2, c0:c1] * gm1
    gc = gc + cw_ref[2:3, c0:c1] * g
    return jax.nn.gelu(gc) * u


def _conv_ffn(x, sh2, sc2, gt2, prev1, prev2, row, n2g_ref, wfi_ref, cw_ref, cb_ref, wfo_ref):
    h2 = (_rms(x, n2g_ref[...]) * (1.0 + sc2) + sh2).astype(BF16)

    def up_project(c0, c1):
        return _dot(h2, wfi_ref[:, c0:c1]), _dot(h2, wfi_ref[:, D_FF + c0:D_FF + c1])

    f = jnp.zeros(x.shape, F32)
    gs = []
    ahead = up_project(*FF_CHUNKS[0])
    for i, (c0, c1) in enumerate(FF_CHUNKS):
        g, u = ahead
        if i + 1 < len(FF_CHUNKS):
            ahead = up_project(*FF_CHUNKS[i + 1])
        act = _conv_gate(g, u, prev1[:, c0:c1], prev2[:, c0:c1], row, cw_ref, cb_ref, c0, c1)
        f = f + _dot(act.astype(BF16), wfo_ref[c0:c1, :])
        gs.append(g)
    return x + gt2 * f, gs


def _ffn_prompt_kernel(x_ref, mod_ref, n2g_ref, wfi_ref, cw_ref, cb_ref, wfo_ref, *rest, n_cast):
    cast_in, (xo_ref, cs_ref), cast_out, carry = rest[:n_cast], rest[n_cast:n_cast + 2], rest[n_cast + 2:-1], rest[-1]
    t = pl.program_id(1)
    T = FFN_TILE_T
    mod = mod_ref[...]
    for src, dst in zip(cast_in, cast_out):
        dst[...] = src[...].astype(BF16)

    @pl.when(t == 0)
    def _():
        carry[...] = jnp.zeros(carry.shape, F32)

    row = lax.broadcasted_iota(jnp.int32, (T, 1), 0)
    prev = carry[...]
    xo, gs = _conv_ffn(x_ref[...], mod[3:4], mod[4:5], mod[5:6], prev[SUBLANES - 1:], prev[SUBLANES - 2:SUBLANES - 1],
                       row, n2g_ref, wfi_ref, cw_ref, cb_ref, wfo_ref)
    xo_ref[...] = xo
    tail = jnp.concatenate([g[T - SUBLANES:T, :] for g in gs], axis=1)
    carry[...] = tail
    cs_ref[...] = tail


def _ffn_prompt_call(l, cast_l, x, mod_all, mod_row0, p, w, to_cast):
    B, S, _ = x.shape
    T = FFN_TILE_T
    n_t = S // T
    tok = pl.BlockSpec((None, T, D_MODEL), lambda b, t: (b, t, 0))
    cast_in, cast_out, cast_shapes = _cast_specs(cast_l, to_cast, B * n_t, lambda b, t: b * n_t + t)
    return pl.pallas_call(
        functools.partial(_ffn_prompt_kernel, n_cast=len(to_cast)),
        grid=(B, n_t),
        in_specs=[
            tok,
            pl.BlockSpec((None, None, 6, D_MODEL), lambda b, t: (l, mod_row0 + b, 0, 0)),
            _layer_spec(l, (1, D_MODEL)),
            _const_spec((D_MODEL, 2 * D_FF)),
            _layer_spec(l, (CONV_W, D_FF)),
            _layer_spec(l, (1, D_FF)),
            _const_spec((D_FF, D_MODEL)),
        ] + cast_in,
        out_specs=[tok, pl.BlockSpec((None, SUBLANES, D_FF), lambda b, t: (b, 0, 0))] + cast_out,
        out_shape=[
            jax.ShapeDtypeStruct((B, S, D_MODEL), F32),
            jax.ShapeDtypeStruct((B, SUBLANES, D_FF), F32),
        ] + cast_shapes,
        scratch_shapes=[pltpu.VMEM((SUBLANES, D_FF), F32)],
        compiler_params=pltpu.CompilerParams(
            dimension_semantics=("arbitrary", "arbitrary"), vmem_limit_bytes=VMEM_LIMIT),
        name="ffn_prompt",
    )(x, mod_all, p["n2g"], w["wfi"], p["cw"], p["cb"], w["wfo"], *to_cast)


def _mixer_sample_kernel(x_ref, mod_ref, n1g_ref, win_ref, qg_ref, kg_ref, e_ref, vg_ref, rel_ref,
                         kc_ref, vc_ref, wsp_ref, bsp_ref, woa_ref, wob_ref, wo_ref, *rest,
                         n_carried, n_batch, n_tok, cache_len):
    (xo_ref, ks_ref, vs_ref, sv_ref,
     h_scr, q_scr, kn_scr, vn_scr, oa_scr, ub_scr, vb_scr, ob_scr, gate_scr, bias_scr) = rest[n_carried:]
    step = pl.program_id(0)
    n_rows = n_batch * n_tok

    def per_token(m):
        return jnp.broadcast_to(m, (n_batch, n_tok, m.shape[-1])).reshape(n_rows, m.shape[-1])

    @pl.when(step == 0)
    def _():
        mod = mod_ref[...]
        sh1, sc1 = per_token(mod[:, 0:1, :]), per_token(mod[:, 1:2, :])
        h = (_rms(x_ref[...], n1g_ref[...]) * (1.0 + sc1) + sh1).astype(BF16)
        h_scr[...] = h
        q = _head_rms(_dot(h, win_ref[:, _Q0:_Q0 + D_A]), qg_ref[...], e_ref)
        k = _head_rms(_dot(h, win_ref[:, _K0:_K0 + D_A]), kg_ref[...], e_ref)
        v = _dot(h, win_ref[:, _V0:_V0 + D_A])
        ks_ref[...] = k
        vs_ref[...] = v
        kn_scr[...] = k.astype(BF16)
        vn_scr[...] = v.astype(BF16)
        q_scr[...] = (q * (HEAD_DIM ** -0.5)).astype(BF16)
        for hh in range(N_HEADS):
            bias_scr[hh * n_tok:(hh + 1) * n_tok, :] = _rel_rows(rel_ref, hh, n_tok)[:, 0:cache_len + LANES]

    lane_head = lax.broadcasted_iota(jnp.int32, (n_tok, D_A), 1) // HEAD_DIM
    for i in range(SAMPLE_BG):
        r0 = pl.multiple_of((step * SAMPLE_BG + i) * n_tok, n_tok)
        qb = q_scr[pl.ds(r0, n_tok), :]
        q_bd = jnp.concatenate([jnp.where(lane_head == hh, qb, jnp.zeros_like(qb)) for hh in range(N_HEADS)],
                               axis=0)
        kn = kn_scr[pl.ds(r0, n_tok), :]
        vn = vn_scr[pl.ds(r0, n_tok), :]
        kt = kc_ref[i].reshape(D_A, cache_len).astype(BF16)
        vt = vc_ref[i].reshape(D_A, cache_len).astype(BF16)
        s_c = _dot(q_bd, kt) + bias_scr[:, 0:cache_len]
        s_n = _dot_nt(q_bd, kn) + bias_scr[:, cache_len:cache_len + n_tok]
        m = jnp.maximum(jnp.max(s_c, axis=-1, keepdims=True), jnp.max(s_n, axis=-1, keepdims=True))
        e_c = jnp.exp(s_c - m)
        e_n = jnp.exp(s_n - m)
        l = jnp.sum(e_c, axis=-1, keepdims=True) + jnp.sum(e_n, axis=-1, keepdims=True)
        o_all = (_dot_nt(e_c.astype(BF16), vt) + _dot(e_n.astype(BF16), vn)) / l
        o = jnp.zeros((n_tok, D_A), F32)
        for hh in range(N_HEADS):
            o = jnp.where(lane_head == hh, o_all[hh * n_tok:(hh + 1) * n_tok, :], o)
        oa_scr[pl.ds(r0, n_tok), :] = o

    def spatial_half(half_scr, c0):
        z = jax.nn.gelu(_dot(h_scr[...], win_ref[:, c0:c0 + D_B]))
        for g in range(N_GROUPS):
            half_scr[g] = z[:, g * GROUP_DIM:(g + 1) * GROUP_DIM]

    def spatial_mix():
        vbn = _rms(jnp.concatenate([vb_scr[g] for g in range(N_GROUPS)], axis=1), vg_ref[...])
        sv_ref[...] = vbn
        for g in range(N_GROUPS):
            vb_scr[g] = vbn[:, g * GROUP_DIM:(g + 1) * GROUP_DIM]
        for g in range(N_GROUPS):
            at_pos = [vb_scr[g, pl.ds(s, n_batch, stride=n_tok), :] for s in range(n_tok)]
            for t in range(n_tok):
                mix = bsp_ref[g, t:t + 1, :]
                for s in range(t + 1):
                    mix = mix + wsp_ref[g, t, s:s + 1, :] * at_pos[s]
                ob_scr[g, pl.ds(t, n_batch, stride=n_tok), :] = ub_scr[g, pl.ds(t, n_batch, stride=n_tok), :] * mix

    def gates(c0, width):
        gate_scr[:, c0:c0 + width] = jax.nn.sigmoid(_dot(h_scr[...], win_ref[:, _G0 + c0:_G0 + c0 + width]))

    stages = [functools.partial(spatial_half, ub_scr, _B0), functools.partial(spatial_half, vb_scr, _B0 + D_B),
              spatial_mix] + [functools.partial(gates, c0, D_B) for c0 in range(0, 2 * D_MODEL, D_B)]
    for i, stage in enumerate(stages):
        pl.when(step == i + 1)(stage)

    @pl.when(step == pl.num_programs(0) - 1)
    def _():
        ob = jnp.concatenate([ob_scr[g] for g in range(N_GROUPS)], axis=1)
        merged = (gate_scr[:, 0:D_MODEL] * _dot(oa_scr[...].astype(BF16), woa_ref[...])
                  + gate_scr[:, D_MODEL:] * _dot(ob.astype(BF16), wob_ref[...]))
        gt1 = per_token(mod_ref[...][:, 2:3, :])
        xo_ref[...] = x_ref[...] + gt1 * _dot(merged.astype(BF16), wo_ref[...])


def _mixer_sample_call(l, depth, x2, mod_all, kc_all, vc_all, p, w, n_batch, n_tok, carried):
    n_rows = n_batch * n_tok
    cache_len = kc_all.shape[-1]
    cache = pl.BlockSpec((None, SAMPLE_BG, N_HEADS, HEAD_DIM, cache_len), lambda s: (l, s, 0, 0, 0))
    full = lambda shape: pl.BlockSpec(shape, lambda s: (0,) * len(shape))
    state = lambda width: pl.BlockSpec((None, n_rows, width), lambda s: (l, 0, 0))
    in_specs = [
        _const_spec((n_rows, D_MODEL)),
        _layer_spec(l, (n_batch, 6, D_MODEL)),
        _layer_spec(l, (1, D_MODEL)),
        _const_spec((D_MODEL, D_IN)),
        _layer_spec(l, (1, D_A)),
        _layer_spec(l, (1, D_A)),
        _const_spec((MXU_DIM, MXU_DIM)),
        _layer_spec(l, (1, D_B)),
        _layer_spec(l, (N_HEADS, REL_SPAN)),
        cache,
        cache,
        _layer_spec(l, (N_GROUPS, n_tok, n_tok, GROUP_DIM)),
        _layer_spec(l, (N_GROUPS, n_tok, GROUP_DIM)),
        _const_spec((D_A, D_MODEL)),
        _const_spec((D_B, D_MODEL)),
        _const_spec((D_MODEL, D_MODEL)),
    ]
    return pl.pallas_call(
        functools.partial(_mixer_sample_kernel, n_carried=len(carried), n_batch=n_batch, n_tok=n_tok,
                          cache_len=cache_len),
        grid=(n_batch // SAMPLE_BG,),
        input_output_aliases={len(in_specs) + i: 1 + i for i in range(len(carried))},
        in_specs=in_specs + [pl.BlockSpec(memory_space=pl.ANY)] * len(carried),
        out_specs=[full((n_rows, D_MODEL)), state(D_A), state(D_A), state(D_B)],
        out_shape=[
            jax.ShapeDtypeStruct((n_rows, D_MODEL), F32),
            jax.ShapeDtypeStruct((depth, n_rows, D_A), F32),
            jax.ShapeDtypeStruct((depth, n_rows, D_A), F32),
            jax.ShapeDtypeStruct((depth, n_rows, D_B), F32),
        ],
        scratch_shapes=[
            pltpu.VMEM((n_rows, D_MODEL), BF16),
            pltpu.VMEM((n_rows, D_A), BF16),
            pltpu.VMEM((n_rows, D_A), BF16),
            pltpu.VMEM((n_rows, D_A), BF16),
            pltpu.VMEM((n_rows, D_A), F32),
            pltpu.VMEM((N_GROUPS, n_rows, GROUP_DIM), F32),
            pltpu.VMEM((N_GROUPS, n_rows, GROUP_DIM), F32),
            pltpu.VMEM((N_GROUPS, n_rows, GROUP_DIM), F32),
            pltpu.VMEM((n_rows, 2 * D_MODEL), F32),
            pltpu.VMEM((N_HEADS * n_tok, cache_len + LANES), F32),
        ],
        compiler_params=pltpu.CompilerParams(
            dimension_semantics=("arbitrary",), vmem_limit_bytes=VMEM_LIMIT),
        name="mixer_sample",
    )(x2, mod_all, p["n1g"], w["win"], p["qg"], p["kg"], p["e_mat"], p["vg"], p["rel"],
      kc_all, vc_all, p["wsp_s"], p["bsp_s"], w["woa"], w["wob"], w["wo"], *carried)


def _ffn_sample_kernel(x_ref, mod_ref, n2g_ref, wfi_ref, cw_ref, cb_ref, cc_ref, wfo_ref,
                       xo_ref, cs_ref, *, n_batch, n_tok):
    n_rows = n_batch * n_tok

    def per_token(m):
        return jnp.broadcast_to(m, (n_batch, n_tok, m.shape[-1])).reshape(n_rows, m.shape[-1])

    mod = mod_ref[...]
    cc = cc_ref[...]
    row = lax.broadcasted_iota(jnp.int32, (n_rows, 1), 0) % n_tok
    xo, gs = _conv_ffn(x_ref[...], per_token(mod[:, 3:4, :]), per_token(mod[:, 4:5, :]), per_token(mod[:, 5:6, :]),
                       per_token(cc[:, 1:2, :]), per_token(cc[:, 0:1, :]), row,
                       n2g_ref, wfi_ref, cw_ref, cb_ref, wfo_ref)
    xo_ref[...] = xo
    for g, (c0, c1) in zip(gs, FF_CHUNKS):
        cs_ref[:, :, c0:c1] = g.reshape(n_batch, n_tok, c1 - c0)[:, n_tok - SUBLANES:, :]


def _ffn_sample_call(l, x2, mod_all, cc_all, p, w, n_batch, n_tok):
    n_rows = n_batch * n_tok
    return pl.pallas_call(
        functools.partial(_ffn_sample_kernel, n_batch=n_batch, n_tok=n_tok),
        grid=(1,),
        in_specs=[
            _const_spec((n_rows, D_MODEL)),
            _layer_spec(l, (n_batch, 6, D_MODEL)),
            _layer_spec(l, (1, D_MODEL)),
            _const_spec((D_MODEL, 2 * D_FF)),
            _layer_spec(l, (CONV_W, D_FF)),
            _layer_spec(l, (1, D_FF)),
            _layer_spec(l, (n_batch, CONV_W - 1, D_FF)),
            _const_spec((D_FF, D_MODEL)),
        ],
        out_specs=[
            pl.BlockSpec((n_rows, D_MODEL), lambda c: (0, 0)),
            pl.BlockSpec((n_batch, SUBLANES, D_FF), lambda c: (0, 0, 0)),
        ],
        out_shape=[
            jax.ShapeDtypeStruct((n_rows, D_MODEL), F32),
            jax.ShapeDtypeStruct((n_batch, SUBLANES, D_FF), F32),
        ],
        compiler_params=pltpu.CompilerParams(
            dimension_semantics=("arbitrary",), vmem_limit_bytes=VMEM_LIMIT),
        name="ffn_sample",
    )(x2, mod_all, p["n2g"], w["wfi"], p["cw"], p["cb"], cc_all, w["wfo"])


def _rel_vector(rel_bias):
    far = rel_bias[..., 2 * REL_CLIP:]
    lead = ATTN_REACH - REL_CLIP
    return jnp.concatenate([
        jnp.broadcast_to(far, (*rel_bias.shape[:-1], lead)),
        rel_bias[..., ::-1],
        jnp.broadcast_to(far, (*rel_bias.shape[:-1], REL_SPAN - lead - (2 * REL_CLIP + 1))),
    ], axis=-1)


def kernel(x_prompt, x_sample, cache_attn_k, cache_attn_v, cache_ffn_conv, c_prompt, c_sample, norm1_g, norm2_g, w_ada, b_ada, w_in, q_norm_g, k_norm_g, rel_bias, v_norm_g, w_spatial, b_spatial, w_out_a, w_out_b, w_out, w_ffn_in, ffn_conv_w, ffn_conv_b, w_ffn_out):
    depth = w_in.shape[0]
    B, S, _ = x_prompt.shape
    NB, NT, _ = x_sample.shape
    cache_len = cache_attn_k.shape[2]
    assert S % FFN_TILE_T == 0 and S % TILE_T == 0 and TILE_T % Q_BLOCK == 0 and TILE_T == ATTN_REACH and S >= ATTN_REACH
    assert NB % SAMPLE_BG == 0 and NB // SAMPLE_BG >= SAMPLE_MIN_STEPS and NT % SUBLANES == 0 and NT <= MLP_CHUNK and N_HEADS * NT == LANES
    assert cache_len == ATTN_REACH and rel_bias.shape[-1] == 2 * REL_CLIP + 1

    rows = NB + B
    rows_pad = -(-rows // SUBLANES) * SUBLANES
    c_all = jnp.concatenate([c_sample, c_prompt, jnp.zeros((rows_pad - rows, D_MODEL), F32)], axis=0)
    mod_all = _ada_call(c_all, w_ada, b_ada).reshape(depth, rows_pad, 6, D_MODEL)

    head_of = jnp.arange(MXU_DIM) // HEAD_DIM
    tril = jnp.tril(jnp.ones((NT, NT), F32))
    ws_small = w_spatial[:, :, :NT, :NT] * tril
    params = {
        "n1g": norm1_g[:, None, :], "n2g": norm2_g[:, None, :],
        "qg": jnp.tile(q_norm_g, (1, N_HEADS))[:, None, :], "kg": jnp.tile(k_norm_g, (1, N_HEADS))[:, None, :],
        "vg": v_norm_g[:, None, :],
        "e_mat": jnp.where(head_of[:, None] == head_of[None, :], 1.0 / HEAD_DIM, 0.0).astype(BF16),
        "rel": _rel_vector(rel_bias),
        "wsp": w_spatial,
        "bsp_b": jnp.broadcast_to(b_spatial[:, :, :, None], (depth, N_GROUPS, MLP_CHUNK, GROUP_DIM)),
        "wsp_s": jnp.broadcast_to(ws_small[..., None], (depth, N_GROUPS, NT, NT, GROUP_DIM)),
        "bsp_s": jnp.broadcast_to(b_spatial[:, :, :NT, None], (depth, N_GROUPS, NT, GROUP_DIM)),
        "cw": ffn_conv_w, "cb": ffn_conv_b[:, None, :],
    }
    kc_all = cache_attn_k.transpose(0, 1, 3, 4, 2)
    vc_all = cache_attn_v.transpose(0, 1, 3, 4, 2)

    xp = x_prompt
    xs = x_sample.reshape(NB * NT, D_MODEL)
    conv_p, conv_s = [], []
    prompt_state, sample_state = (), ()
    mixer_names, ffn_names = ("win", "woa", "wob", "wo"), ("wfi", "wfo")
    mixer_f32, ffn_f32 = (w_in, w_out_a, w_out_b, w_out), (w_ffn_in, w_ffn_out)
    w = {name: a[0].astype(BF16) for name, a in zip(mixer_names, mixer_f32)}
    for l in range(depth):
        xp, *outs = _mixer_prompt_call(l, depth, xp, mod_all, NB, params, w, ffn_f32, tuple(prompt_state))
        prompt_state, w_ffn = outs[:2], dict(zip(ffn_names, outs[2:]))
        nxt = mixer_f32 if l + 1 < depth else ()
        xp, cp, *outs = _ffn_prompt_call(l, l + 1, xp, mod_all, NB, params, w_ffn, nxt)
        xs, *sample_state = _mixer_sample_call(l, depth, xs, mod_all, kc_all, vc_all, params, w, NB, NT,
                                               tuple(sample_state))
        xs, cs = _ffn_sample_call(l, xs, mod_all, cache_ffn_conv, params, w_ffn, NB, NT)
        w = dict(zip(mixer_names, outs))
        conv_p.append(cp[:, SUBLANES - (CONV_W - 1):])
        conv_s.append(cs[:, SUBLANES - (CONV_W - 1):])
    new_k_prompt, new_v_prompt = (
        s.reshape(depth, B, N_HEADS, HEAD_DIM, ATTN_REACH).transpose(0, 1, 4, 2, 3) for s in prompt_state)
    ks, vs, sv = sample_state
    return (xp, xs.reshape(NB, NT, D_MODEL), new_k_prompt, new_v_prompt, jnp.stack(conv_p),
            ks.reshape(depth, NB, NT, N_HEADS, HEAD_DIM), vs.reshape(depth, NB, NT, N_HEADS, HEAD_DIM),
            sv.reshape(depth, NB, NT, N_GROUPS, GROUP_DIM), jnp.stack(conv_s))
```

```python
import functools

import jax
import jax.numpy as jnp
from jax import lax
from jax.experimental import pallas as pl
from jax.experimental.pallas import tpu as pltpu

F32 = jnp.float32
BF16 = jnp.bfloat16

D_MODEL = 1024
CHUNK = 64
ATTN_REACH = 512
N_HEADS = 8
HEAD_DIM = 64
D_A = N_HEADS * HEAD_DIM
REL_CLIP = 128
MLP_CHUNK = 128
N_GROUPS = 4
D_B = 512
GROUP_DIM = D_B // N_GROUPS
D_FF = 2816
CONV_W = 3
EPS = 1e-6

LANES = 128
SUBLANES = 8
BF16_ROWS = 16
MXU_DIM = 256
N_PAIRS = N_HEADS // 2
Q_BLOCK = 2 * CHUNK
KV_WINDOW = ATTN_REACH + Q_BLOCK
REL_SPAN = KV_WINDOW + Q_BLOCK
TILE_T = 512
FFN_TILE_T = 512
HIST = ATTN_REACH
FF_CHUNKS = ((0, 1024), (1024, 2048), (2048, 2560), (2560, D_FF))
ADA_BLOCK_N = 1536
SAMPLE_BG = 4
SAMPLE_MIN_STEPS = 8
VMEM_LIMIT = 56 * 1024 * 1024
_Q0, _K0, _V0, _B0, _G0 = 0, D_A, 2 * D_A, 3 * D_A, 3 * D_A + 2 * D_B
D_IN = _G0 + 2 * D_MODEL


def _dot(a, b):
    return jnp.dot(a, b, preferred_element_type=F32)


def _dot_nt(a, b):
    return lax.dot_general(a, b, (((1,), (1,)), ((), ())), preferred_element_type=F32)


def _dot_tn(a, b):
    return lax.dot_general(a, b, (((0,), (0,)), ((), ())), preferred_element_type=F32)


def _rms(x, g):
    return (x * lax.rsqrt(jnp.mean(x * x, axis=-1, keepdims=True) + EPS)) * g


def _head_rms(a, g, e_ref):
    sq = a * a
    hi = sq.astype(BF16)
    lo = (sq - hi.astype(F32)).astype(BF16)
    e = e_ref[...]
    w = e.shape[0]
    ms = jnp.concatenate([_dot(hi[:, c:c + w], e) + _dot(lo[:, c:c + w], e) for c in range(0, a.shape[1], w)],
                         axis=1)
    return (a * lax.rsqrt(ms + EPS)) * g


def _rel_rows(rel_ref, h, n_rows):
    wb = jnp.broadcast_to(rel_ref[h:h + 1, :], (n_rows, REL_SPAN))
    return pltpu.roll(wb, 0, axis=1, stride=1, stride_axis=0)


def _const_spec(shape):
    n = len(shape)
    return pl.BlockSpec(shape, lambda *_: (0,) * n, pipeline_mode=pl.Buffered(1))


def _cast_specs(l, stacked, n_steps, step_of):
    in_specs, out_specs, out_shapes = [], [], []
    for a in stacked:
        rows, cols = a.shape[1:]
        blk = next(r for r in range(BF16_ROWS, rows + 1, BF16_ROWS) if rows % r == 0 and rows // r <= n_steps)
        last = rows // blk - 1
        in_specs.append(pl.BlockSpec((None, blk, cols), lambda *ids, last=last: (l, jnp.minimum(step_of(*ids), last), 0)))
        out_specs.append(pl.BlockSpec((blk, cols), lambda *ids, last=last: (jnp.minimum(step_of(*ids), last), 0)))
        out_shapes.append(jax.ShapeDtypeStruct((rows, cols), BF16))
    return in_specs, out_specs, out_shapes


def _layer_spec(l, shape):
    n = len(shape)
    return pl.BlockSpec((None, *shape), lambda *_: (l,) + (0,) * n, pipeline_mode=pl.Buffered(1))


def _ada_kernel(c_ref, w_ref, b_ref, *rest, n_cast):
    cast_in, o_ref, cast_out = rest[:n_cast], rest[n_cast], rest[n_cast + 1:]
    s = jax.nn.silu(c_ref[...]).astype(BF16)
    o_ref[...] = _dot(s, w_ref[...].astype(BF16)) + b_ref[...]
    for src, dst in zip(cast_in, cast_out):
        dst[...] = src[...].astype(BF16)


def _ada_call(c_all, w_ada, b_ada, to_cast):
    depth = w_ada.shape[0]
    rows = c_all.shape[0]
    n_out = w_ada.shape[2]
    n_blk = n_out // ADA_BLOCK_N
    cast_in, cast_out, cast_shapes = _cast_specs(0, to_cast, depth * n_blk, lambda l, n: l * n_blk + n)
    return pl.pallas_call(
        functools.partial(_ada_kernel, n_cast=len(to_cast)),
        grid=(depth, n_blk),
        in_specs=[
            pl.BlockSpec((rows, D_MODEL), lambda l, n: (0, 0)),
            pl.BlockSpec((None, D_MODEL, ADA_BLOCK_N), lambda l, n: (l, 0, n)),
            pl.BlockSpec((None, 1, ADA_BLOCK_N), lambda l, n: (l, 0, n)),
        ] + cast_in,
        out_specs=[pl.BlockSpec((None, rows, ADA_BLOCK_N), lambda l, n: (l, 0, n))] + cast_out,
        out_shape=[jax.ShapeDtypeStruct((depth, rows, n_out), F32)] + cast_shapes,
        compiler_params=pltpu.CompilerParams(
            dimension_semantics=("arbitrary", "arbitrary"), vmem_limit_bytes=VMEM_LIMIT),
        name="ada_mod",
    )(c_all, w_ada, b_ada.reshape(depth, 1, n_out), *to_cast)


def _mixer_prompt_kernel(x_ref, mod_ref, n1g_ref, win_ref, qg_ref, kg_ref, e_ref, vg_ref, rel_ref,
                         wsp_ref, bsp_ref, woa_ref, wob_ref, wo_ref, *rest, n_cast, n_carried):
    cast_in, rest = rest[:n_cast], rest[n_cast + n_carried:]
    xo_ref, kst_ref, vst_ref = rest[:3]
    cast_out = rest[3:3 + n_cast]
    kbuf, vtbuf, oat_scr, ob_scr, bias_scr, st_scr, e_scr = rest[3 + n_cast:]
    b = pl.program_id(0)
    t = pl.program_id(1)
    T = TILE_T
    for src, dst in zip(cast_in, cast_out):
        dst[...] = src[...].astype(BF16)

    @pl.when((b == 0) & (t == 0))
    def _():
        qi = lax.broadcasted_iota(jnp.int32, (Q_BLOCK, KV_WINDOW), 0) // CHUNK
        ki = lax.broadcasted_iota(jnp.int32, (Q_BLOCK, KV_WINDOW), 1) // CHUNK
        band = (ki >= qi) & (ki <= qi + ATTN_REACH // CHUNK)
        for h in range(N_HEADS):
            tab = jnp.where(band, _rel_rows(rel_ref, h, Q_BLOCK)[:, 0:KV_WINDOW], -jnp.inf)
            bias_scr[h // 2, 0:KV_WINDOW, (h % 2) * Q_BLOCK:(h % 2 + 1) * Q_BLOCK] = tab.T
        bias_scr[:, KV_WINDOW:, :] = jnp.full((N_PAIRS, Q_BLOCK, 2 * Q_BLOCK), -jnp.inf, F32)

    @pl.when(t == 0)
    def _():
        kbuf[0:HIST, :] = jnp.zeros((HIST, D_A), BF16)
        vtbuf[:, 0:HIST] = jnp.zeros((D_A, HIST), BF16)

    @pl.when(t > 0)
    def _():
        for r in range(0, HIST, T):
            kbuf[r:r + T, :] = kbuf[r + T:r + 2 * T, :]
            vtbuf[:, r:r + T] = vtbuf[:, r + T:r + 2 * T]

    mod = mod_ref[...]
    sh1, sc1, gt1 = mod[0:1], mod[1:2], mod[2:3]

    x = x_ref[...]
    h = (_rms(x, n1g_ref[...]) * (1.0 + sc1) + sh1).astype(BF16)

    q = _head_rms(_dot(h, win_ref[:, _Q0:_Q0 + D_A]), qg_ref[...], e_ref)
    k = _head_rms(_dot(h, win_ref[:, _K0:_K0 + D_A]), kg_ref[...], e_ref)
    v = _dot(h, win_ref[:, _V0:_V0 + D_A])
    vt = v.T
    kbuf[HIST:HIST + T, :] = k.astype(BF16)
    vtbuf[:, HIST:HIST + T] = vt.astype(BF16)

    kst_ref[...] = k.T
    vst_ref[...] = vt

    qb = (q * (HEAD_DIM ** -0.5)).astype(BF16)

    lane = lax.broadcasted_iota(jnp.int32, (1, LANES), 1)
    low_half = lane < HEAD_DIM

    n_qb = T // Q_BLOCK

    def scores(p, j, slot):
        r0, c0 = j * Q_BLOCK, p * LANES
        qp = qb[r0:r0 + Q_BLOCK, c0:c0 + LANES]
        q2 = jnp.concatenate([jnp.where(low_half, qp, jnp.zeros_like(qp)),
                              jnp.where(low_half, jnp.zeros_like(qp), qp)], axis=0)
        st = _dot_nt(kbuf[r0:r0 + KV_WINDOW, c0:c0 + LANES], q2)
        first_valid = HIST - t * T - r0
        m = None
        for r in range(0, KV_WINDOW, Q_BLOCK):
            src = jnp.where(first_valid > r, KV_WINDOW, r) if r < HIST else r
            blk = st[r:r + Q_BLOCK] + bias_scr[p, pl.ds(pl.multiple_of(src, Q_BLOCK), Q_BLOCK), :]
            st_scr[slot, j, r:r + Q_BLOCK, :] = blk
            bm = jnp.max(blk, axis=0, keepdims=True)
            m = bm if m is None else jnp.maximum(m, bm)
        return m

    def weights(j, slot, m):
        e = jnp.exp(st_scr[slot, j] - m)
        e_scr[slot, j] = e.astype(BF16)
        return jnp.sum(e, axis=0, keepdims=True)

    def outputs(p, j, slot, total):
        r0, c0 = j * Q_BLOCK, p * LANES
        ot = _dot(vtbuf[c0:c0 + LANES, r0:r0 + KV_WINDOW], e_scr[slot, j]) / total
        oat_scr[c0:c0 + HEAD_DIM, r0:r0 + Q_BLOCK] = ot[0:HEAD_DIM, 0:Q_BLOCK]
        oat_scr[c0 + HEAD_DIM:c0 + LANES, r0:r0 + Q_BLOCK] = ot[HEAD_DIM:, Q_BLOCK:]

    side = {}

    def project(name, act, c0):
        side[name] = act(_dot(h, win_ref[:, c0:c0 + MXU_DIM]))

    side_work = ([functools.partial(project, ("b", i), jax.nn.gelu, _B0 + i * MXU_DIM)
                  for i in range(2 * D_B // MXU_DIM)]
                 + [functools.partial(project, ("g", i), jax.nn.sigmoid, _G0 + i * MXU_DIM)
                    for i in range(2 * D_MODEL // MXU_DIM)])
    maxima = [scores(0, j, 0) for j in range(n_qb)]
    for p in range(N_PAIRS):
        slot = p % 2
        if p + 1 < N_PAIRS:
            next_maxima = [scores(p + 1, j, 1 - slot) for j in range(n_qb)]
        for work in side_work[len(side_work) * p // N_PAIRS:len(side_work) * (p + 1) // N_PAIRS]:
            work()
        totals = [weights(j, slot, maxima[j]) for j in range(n_qb)]
        for j in range(n_qb):
            outputs(p, j, slot, totals[j])
        maxima = next_maxima
    n_b, n_g = D_B // MXU_DIM, D_MODEL // MXU_DIM
    ub = jnp.concatenate([side["b", i] for i in range(n_b)], axis=1)
    vbn = _rms(jnp.concatenate([side["b", n_b + i] for i in range(n_b)], axis=1), vg_ref[...]).astype(BF16)
    ga = jnp.concatenate([side["g", i] for i in range(n_g)], axis=1)
    gb = jnp.concatenate([side["g", n_g + i] for i in range(n_g)], axis=1)
    row_i = lax.broadcasted_iota(jnp.int32, (MLP_CHUNK, MLP_CHUNK), 0)
    col_i = lax.broadcasted_iota(jnp.int32, (MLP_CHUNK, MLP_CHUNK), 1)
    for g in range(N_GROUPS):
        wc = jnp.where(row_i >= col_i, wsp_ref[g], 0.0).astype(BF16)
        g0 = g * GROUP_DIM
        for c in range(T // MLP_CHUNK):
            r0 = c * MLP_CHUNK
            mix = _dot(wc, vbn[r0:r0 + MLP_CHUNK, g0:g0 + GROUP_DIM]) + bsp_ref[g]
            ob_scr[r0:r0 + MLP_CHUNK, g0:g0 + GROUP_DIM] = ub[r0:r0 + MLP_CHUNK, g0:g0 + GROUP_DIM] * mix

    merged = (ga * _dot_tn(oat_scr[...].astype(BF16), woa_ref[...])
              + gb * _dot(ob_scr[...].astype(BF16), wob_ref[...]))
    xo_ref[...] = x + gt1 * _dot(merged.astype(BF16), wo_ref[...])


def _mixer_prompt_call(l, depth, x, mod_all, mod_row0, p, w, to_cast, carried):
    B, S, _ = x.shape
    T = TILE_T
    n_t = S // T
    tok = pl.BlockSpec((None, T, D_MODEL), lambda b, t: (b, t, 0))
    state = pl.BlockSpec((None, None, D_A, ATTN_REACH), lambda b, t: (l, b, 0, 0))
    cast_in, cast_out, cast_shapes = _cast_specs(l, to_cast, B * n_t, lambda b, t: b * n_t + t)
    in_specs = [
        tok,
        pl.BlockSpec((None, None, 6, D_MODEL), lambda b, t: (l, mod_row0 + b, 0, 0)),
        _layer_spec(l, (1, D_MODEL)),
        _const_spec((D_MODEL, D_IN)),
        _layer_spec(l, (1, D_A)),
        _layer_spec(l, (1, D_A)),
        _const_spec((MXU_DIM, MXU_DIM)),
        _layer_spec(l, (1, D_B)),
        _layer_spec(l, (N_HEADS, REL_SPAN)),
        _layer_spec(l, (N_GROUPS, MLP_CHUNK, MLP_CHUNK)),
        _layer_spec(l, (N_GROUPS, MLP_CHUNK, GROUP_DIM)),
        _const_spec((D_A, D_MODEL)),
        _const_spec((D_B, D_MODEL)),
        _const_spec((D_MODEL, D_MODEL)),
    ]
    in_specs += cast_in
    return pl.pallas_call(
        functools.partial(_mixer_prompt_kernel, n_cast=len(to_cast), n_carried=len(carried)),
        grid=(B, n_t),
        input_output_aliases={len(in_specs) + i: 1 + i for i in range(len(carried))},
        in_specs=in_specs + [pl.BlockSpec(memory_space=pl.ANY)] * len(carried),
        out_specs=[tok, state, state] + cast_out,
        out_shape=[
            jax.ShapeDtypeStruct((B, S, D_MODEL), F32),
            jax.ShapeDtypeStruct((depth, B, D_A, ATTN_REACH), F32),
            jax.ShapeDtypeStruct((depth, B, D_A, ATTN_REACH), F32),
        ] + cast_shapes,
        scratch_shapes=[
            pltpu.VMEM((HIST + T, D_A), BF16),
            pltpu.VMEM((D_A, HIST + T), BF16),
            pltpu.VMEM((D_A, T), F32),
            pltpu.VMEM((T, D_B), F32),
            pltpu.VMEM((N_PAIRS, KV_WINDOW + Q_BLOCK, 2 * Q_BLOCK), F32),
            pltpu.VMEM((2, T // Q_BLOCK, KV_WINDOW, 2 * Q_BLOCK), F32),
            pltpu.VMEM((2, T // Q_BLOCK, KV_WINDOW, 2 * Q_BLOCK), BF16),
        ],
        compiler_params=pltpu.CompilerParams(
            dimension_semantics=("arbitrary", "arbitrary"), vmem_limit_bytes=VMEM_LIMIT),
        name="mixer_prompt",
    )(x, mod_all, p["n1g"], w["win"], p["qg"], p["kg"], p["e_mat"], p["vg"], p["rel"],
      p["wsp"], p["bsp_b"], w["woa"], w["wob"], w["wo"], *to_cast, *carried)


def _conv_gate(g, u, prev1, prev2, row, cw_ref, cb_ref, c0, c1):
    gm1 = jnp.where(row == 0, prev1, pltpu.roll(g, 1, axis=0))
    gm2 = jnp.where(row == 0, prev2, jnp.where(row == 1, prev1, pltpu.roll(g, 2, axis=0)))
    gc = cb_ref[:, c0:c1] + cw_ref[0:1, c0:c1] * gm2
    gc = gc + cw_ref[1:2, c0:c1] * gm1
    gc = gc + cw_ref[2:3, c0:c1] * g
    return jax.nn.gelu(gc) * u


def _conv_ffn(x, sh2, sc2, gt2, prev1, prev2, row, n2g_ref, wfi_ref, cw_ref, cb_ref, wfo_ref):
    h2 = (_rms(x, n2g_ref[...]) * (1.0 + sc2) + sh2).astype(BF16)

    def up_project(c0, c1):
        return _dot(h2, wfi_ref[:, c0:c1]), _dot(h2, wfi_ref[:, D_FF + c0:D_FF + c1])

    f = jnp.zeros(x.shape, F32)
    gs = []
    ahead = up_project(*FF_CHUNKS[0])
    for i, (c0, c1) in enumerate(FF_CHUNKS):
        g, u = ahead
        if i + 1 < len(FF_CHUNKS):
            ahead = up_project(*FF_CHUNKS[i + 1])
        act = _conv_gate(g, u, prev1[:, c0:c1], prev2[:, c0:c1], row, cw_ref, cb_ref, c0, c1)
        f = f + _dot(act.astype(BF16), wfo_ref[c0:c1, :])
        gs.append(g)
    return x + gt2 * f, gs


def _ffn_prompt_kernel(x_ref, mod_ref, n2g_ref, wfi_ref, cw_ref, cb_ref, wfo_ref, *rest, n_cast):
    cast_in, (xo_ref, cs_ref), cast_out, carry = rest[:n_cast], rest[n_cast:n_cast + 2], rest[n_cast + 2:-1], rest[-1]
    t = pl.program_id(1)
    T = FFN_TILE_T
    mod = mod_ref[...]
    for src, dst in zip(cast_in, cast_out):
        dst[...] = src[...].astype(BF16)

    @pl.when(t == 0)
    def _():
        carry[...] = jnp.zeros(carry.shape, F32)

    row = lax.broadcasted_iota(jnp.int32, (T, 1), 0)
    prev = carry[...]
    xo, gs = _conv_ffn(x_ref[...], mod[3:4], mod[4:5], mod[5:6], prev[SUBLANES - 1:], prev[SUBLANES - 2:SUBLANES - 1],
                       row, n2g_ref, wfi_ref, cw_ref, cb_ref, wfo_ref)
    xo_ref[...] = xo
    tail = jnp.concatenate([g[T - SUBLANES:T, :] for g in gs], axis=1)
    carry[...] = tail
    cs_ref[...] = tail


def _ffn_prompt_call(l, cast_l, x, mod_all, mod_row0, p, w, to_cast):
    B, S, _ = x.shape
    T = FFN_TILE_T
    n_t = S // T
    tok = pl.BlockSpec((None, T, D_MODEL), lambda b, t: (b, t, 0))
    cast_in, cast_out, cast_shapes = _cast_specs(cast_l, to_cast, B * n_t, lambda b, t: b * n_t + t)
    return pl.pallas_call(
        functools.partial(_ffn_prompt_kernel, n_cast=len(to_cast)),
        grid=(B, n_t),
        in_specs=[
            tok,
            pl.BlockSpec((None, None, 6, D_MODEL), lambda b, t: (l, mod_row0 + b, 0, 0)),
            _layer_spec(l, (1, D_MODEL)),
            _const_spec((D_MODEL, 2 * D_FF)),
            _layer_spec(l, (CONV_W, D_FF)),
            _layer_spec(l, (1, D_FF)),
            _const_spec((D_FF, D_MODEL)),
        ] + cast_in,
        out_specs=[tok, pl.BlockSpec((None, SUBLANES, D_FF), lambda b, t: (b, 0, 0))] + cast_out,
        out_shape=[
            jax.ShapeDtypeStruct((B, S, D_MODEL), F32),
            jax.ShapeDtypeStruct((B, SUBLANES, D_FF), F32),
        ] + cast_shapes,
        scratch_shapes=[pltpu.VMEM((SUBLANES, D_FF), F32)],
        compiler_params=pltpu.CompilerParams(
            dimension_semantics=("arbitrary", "arbitrary"), vmem_limit_bytes=VMEM_LIMIT),
        name="ffn_prompt",
    )(x, mod_all, p["n2g"], w["wfi"], p["cw"], p["cb"], w["wfo"], *to_cast)


def _mixer_sample_kernel(x_ref, mod_ref, n1g_ref, win_ref, qg_ref, kg_ref, e_ref, vg_ref, rel_ref,
                         kc_ref, vc_ref, wsp_ref, bsp_ref, woa_ref, wob_ref, wo_ref, *rest,
                         n_carried, n_batch, n_tok, cache_len):
    (xo_ref, ks_ref, vs_ref, sv_ref,
     h_scr, q_scr, kn_scr, vn_scr, oa_scr, ub_scr, vb_scr, ob_scr, gate_scr, bias_scr) = rest[n_carried:]
    step = pl.program_id(0)
    n_rows = n_batch * n_tok

    def per_token(m):
        return jnp.broadcast_to(m, (n_batch, n_tok, m.shape[-1])).reshape(n_rows, m.shape[-1])

    @pl.when(step == 0)
    def _():
        mod = mod_ref[...]
        sh1, sc1 = per_token(mod[:, 0:1, :]), per_token(mod[:, 1:2, :])
        h = (_rms(x_ref[...], n1g_ref[...]) * (1.0 + sc1) + sh1).astype(BF16)
        h_scr[...] = h
        q = _head_rms(_dot(h, win_ref[:, _Q0:_Q0 + D_A]), qg_ref[...], e_ref)
        k = _head_rms(_dot(h, win_ref[:, _K0:_K0 + D_A]), kg_ref[...], e_ref)
        v = _dot(h, win_ref[:, _V0:_V0 + D_A])
        ks_ref[...] = k
        vs_ref[...] = v
        kn_scr[...] = k.astype(BF16)
        vn_scr[...] = v.astype(BF16)
        q_scr[...] = (q * (HEAD_DIM ** -0.5)).astype(BF16)
        for hh in range(N_HEADS):
            bias_scr[hh * n_tok:(hh + 1) * n_tok, :] = _rel_rows(rel_ref, hh, n_tok)[:, 0:cache_len + LANES]

    lane_head = lax.broadcasted_iota(jnp.int32, (n_tok, D_A), 1) // HEAD_DIM
    for i in range(SAMPLE_BG):
        r0 = pl.multiple_of((step * SAMPLE_BG + i) * n_tok, n_tok)
        qb = q_scr[pl.ds(r0, n_tok), :]
        q_bd = jnp.concatenate([jnp.where(lane_head == hh, qb, jnp.zeros_like(qb)) for hh in range(N_HEADS)],
                               axis=0)
        kn = kn_scr[pl.ds(r0, n_tok), :]
        vn = vn_scr[pl.ds(r0, n_tok), :]
        kt = kc_ref[i].reshape(D_A, cache_len).astype(BF16)
        vt = vc_ref[i].reshape(D_A, cache_len).astype(BF16)
        s_c = _dot(q_bd, kt) + bias_scr[:, 0:cache_len]
        s_n = _dot_nt(q_bd, kn) + bias_scr[:, cache_len:cache_len + n_tok]
        m = jnp.maximum(jnp.max(s_c, axis=-1, keepdims=True), jnp.max(s_n, axis=-1, keepdims=True))
        e_c = jnp.exp(s_c - m)
        e_n = jnp.exp(s_n - m)
        l = jnp.sum(e_c, axis=-1, keepdims=True) + jnp.sum(e_n, axis=-1, keepdims=True)
        o_all = (_dot_nt(e_c.astype(BF16), vt) + _dot(e_n.astype(BF16), vn)) / l
        o = jnp.zeros((n_tok, D_A), F32)
        for hh in range(N_HEADS):
            o = jnp.where(lane_head == hh, o_all[hh * n_tok:(hh + 1) * n_tok, :], o)
        oa_scr[pl.ds(r0, n_tok), :] = o

    def spatial_half(half_scr, c0):
        z = jax.nn.gelu(_dot(h_scr[...], win_ref[:, c0:c0 + D_B]))
        for g in range(N_GROUPS):
            half_scr[g] = z[:, g * GROUP_DIM:(g + 1) * GROUP_DIM]

    def spatial_mix():
        vbn = _rms(jnp.concatenate([vb_scr[g] for g in range(N_GROUPS)], axis=1), vg_ref[...])
        sv_ref[...] = vbn
        for g in range(N_GROUPS):
            vb_scr[g] = vbn[:, g * GROUP_DIM:(g + 1) * GROUP_DIM]
        for g in range(N_GROUPS):
            at_pos = [vb_scr[g, pl.ds(s, n_batch, stride=n_tok), :] for s in range(n_tok)]
            for t in range(n_tok):
                mix = bsp_ref[g, t:t + 1, :]
                for s in range(t + 1):
                    mix = mix + wsp_ref[g, t, s:s + 1, :] * at_pos[s]
                ob_scr[g, pl.ds(t, n_batch, stride=n_tok), :] = ub_scr[g, pl.ds(t, n_batch, stride=n_tok), :] * mix

    def gates(c0, width):
        gate_scr[:, c0:c0 + width] = jax.nn.sigmoid(_dot(h_scr[...], win_ref[:, _G0 + c0:_G0 + c0 + width]))

    stages = [functools.partial(spatial_half, ub_scr, _B0), functools.partial(spatial_half, vb_scr, _B0 + D_B),
              spatial_mix] + [functools.partial(gates, c0, D_B) for c0 in range(0, 2 * D_MODEL, D_B)]
    for i, stage in enumerate(stages):
        pl.when(step == i + 1)(stage)

    @pl.when(step == pl.num_programs(0) - 1)
    def _():
        ob = jnp.concatenate([ob_scr[g] for g in range(N_GROUPS)], axis=1)
        merged = (gate_scr[:, 0:D_MODEL] * _dot(oa_scr[...].astype(BF16), woa_ref[...])
                  + gate_scr[:, D_MODEL:] * _dot(ob.astype(BF16), wob_ref[...]))
        gt1 = per_token(mod_ref[...][:, 2:3, :])
        xo_ref[...] = x_ref[...] + gt1 * _dot(merged.astype(BF16), wo_ref[...])


def _mixer_sample_call(l, depth, x2, mod_all, kc_all, vc_all, p, w, n_batch, n_tok, carried):
    n_rows = n_batch * n_tok
    cache_len = kc_all.shape[-1]
    cache = pl.BlockSpec((None, SAMPLE_BG, N_HEADS, HEAD_DIM, cache_len), lambda s: (l, s, 0, 0, 0))
    full = lambda shape: pl.BlockSpec(shape, lambda s: (0,) * len(shape))
    state = lambda width: pl.BlockSpec((None, n_rows, width), lambda s: (l, 0, 0))
    in_specs = [
        _const_spec((n_rows, D_MODEL)),
        _layer_spec(l, (n_batch, 6, D_MODEL)),
        _layer_spec(l, (1, D_MODEL)),
        _const_spec((D_MODEL, D_IN)),
        _layer_spec(l, (1, D_A)),
        _layer_spec(l, (1, D_A)),
        _const_spec((MXU_DIM, MXU_DIM)),
        _layer_spec(l, (1, D_B)),
        _layer_spec(l, (N_HEADS, REL_SPAN)),
        cache,
        cache,
        _layer_spec(l, (N_GROUPS, n_tok, n_tok, GROUP_DIM)),
        _layer_spec(l, (N_GROUPS, n_tok, GROUP_DIM)),
        _const_spec((D_A, D_MODEL)),
        _const_spec((D_B, D_MODEL)),
        _const_spec((D_MODEL, D_MODEL)),
    ]
    return pl.pallas_call(
        functools.partial(_mixer_sample_kernel, n_carried=len(carried), n_batch=n_batch, n_tok=n_tok,
                          cache_len=cache_len),
        grid=(n_batch // SAMPLE_BG,),
        input_output_aliases={len(in_specs) + i: 1 + i for i in range(len(carried))},
        in_specs=in_specs + [pl.BlockSpec(memory_space=pl.ANY)] * len(carried),
        out_specs=[full((n_rows, D_MODEL)), state(D_A), state(D_A), state(D_B)],
        out_shape=[
            jax.ShapeDtypeStruct((n_rows, D_MODEL), F32),
            jax.ShapeDtypeStruct((depth, n_rows, D_A), F32),
            jax.ShapeDtypeStruct((depth, n_rows, D_A), F32),
            jax.ShapeDtypeStruct((depth, n_rows, D_B), F32),
        ],
        scratch_shapes=[
            pltpu.VMEM((n_rows, D_MODEL), BF16),
            pltpu.VMEM((n_rows, D_A), BF16),
            pltpu.VMEM((n_rows, D_A), BF16),
            pltpu.VMEM((n_rows, D_A), BF16),
            pltpu.VMEM((n_rows, D_A), F32),
            pltpu.VMEM((N_GROUPS, n_rows, GROUP_DIM), F32),
            pltpu.VMEM((N_GROUPS, n_rows, GROUP_DIM), F32),
            pltpu.VMEM((N_GROUPS, n_rows, GROUP_DIM), F32),
            pltpu.VMEM((n_rows, 2 * D_MODEL), F32),
            pltpu.VMEM((N_HEADS * n_tok, cache_len + LANES), F32),
        ],
        compiler_params=pltpu.CompilerParams(
            dimension_semantics=("arbitrary",), vmem_limit_bytes=VMEM_LIMIT),
        name="mixer_sample",
    )(x2, mod_all, p["n1g"], w["win"], p["qg"], p["kg"], p["e_mat"], p["vg"], p["rel"],
      kc_all, vc_all, p["wsp_s"], p["bsp_s"], w["woa"], w["wob"], w["wo"], *carried)


def _ffn_sample_kernel(x_ref, mod_ref, n2g_ref, wfi_ref, cw_ref, cb_ref, cc_ref, wfo_ref,
                       xo_ref, cs_ref, *, n_batch, n_tok):
    n_rows = n_batch * n_tok

    def per_token(m):
        return jnp.broadcast_to(m, (n_batch, n_tok, m.shape[-1])).reshape(n_rows, m.shape[-1])

    mod = mod_ref[...]
    cc = cc_ref[...]
    row = lax.broadcasted_iota(jnp.int32, (n_rows, 1), 0) % n_tok
    xo, gs = _conv_ffn(x_ref[...], per_token(mod[:, 3:4, :]), per_token(mod[:, 4:5, :]), per_token(mod[:, 5:6, :]),
                       per_token(cc[:, 1:2, :]), per_token(cc[:, 0:1, :]), row,
                       n2g_ref, wfi_ref, cw_ref, cb_ref, wfo_ref)
    xo_ref[...] = xo
    for g, (c0, c1) in zip(gs, FF_CHUNKS):
        cs_ref[:, :, c0:c1] = g.reshape(n_batch, n_tok, c1 - c0)[:, n_tok - SUBLANES:, :]


def _ffn_sample_call(l, x2, mod_all, cc_all, p, w, n_batch, n_tok):
    n_rows = n_batch * n_tok
    return pl.pallas_call(
        functools.partial(_ffn_sample_kernel, n_batch=n_batch, n_tok=n_tok),
        grid=(1,),
        in_specs=[
            _const_spec((n_rows, D_MODEL)),
            _layer_spec(l, (n_batch, 6, D_MODEL)),
            _layer_spec(l, (1, D_MODEL)),
            _const_spec((D_MODEL, 2 * D_FF)),
            _layer_spec(l, (CONV_W, D_FF)),
            _layer_spec(l, (1, D_FF)),
            _layer_spec(l, (n_batch, CONV_W - 1, D_FF)),
            _const_spec((D_FF, D_MODEL)),
        ],
        out_specs=[
            pl.BlockSpec((n_rows, D_MODEL), lambda c: (0, 0)),
            pl.BlockSpec((n_batch, SUBLANES, D_FF), lambda c: (0, 0, 0)),
        ],
        out_shape=[
            jax.ShapeDtypeStruct((n_rows, D_MODEL), F32),
            jax.ShapeDtypeStruct((n_batch, SUBLANES, D_FF), F32),
        ],
        compiler_params=pltpu.CompilerParams(
            dimension_semantics=("arbitrary",), vmem_limit_bytes=VMEM_LIMIT),
        name="ffn_sample",
    )(x2, mod_all, p["n2g"], w["wfi"], p["cw"], p["cb"], cc_all, w["wfo"])


def _rel_vector(rel_bias):
    far = rel_bias[..., 2 * REL_CLIP:]
    lead = ATTN_REACH - REL_CLIP
    return jnp.concatenate([
        jnp.broadcast_to(far, (*rel_bias.shape[:-1], lead)),
        rel_bias[..., ::-1],
        jnp.broadcast_to(far, (*rel_bias.shape[:-1], REL_SPAN - lead - (2 * REL_CLIP + 1))),
    ], axis=-1)


def kernel(x_prompt, x_sample, cache_attn_k, cache_attn_v, cache_ffn_conv, c_prompt, c_sample, norm1_g, norm2_g, w_ada, b_ada, w_in, q_norm_g, k_norm_g, rel_bias, v_norm_g, w_spatial, b_spatial, w_out_a, w_out_b, w_out, w_ffn_in, ffn_conv_w, ffn_conv_b, w_ffn_out):
    depth = w_in.shape[0]
    B, S, _ = x_prompt.shape
    NB, NT, _ = x_sample.shape
    cache_len = cache_attn_k.shape[2]
    assert S % FFN_TILE_T == 0 and S % TILE_T == 0 and TILE_T % Q_BLOCK == 0 and TILE_T == ATTN_REACH and S >= ATTN_REACH
    assert NB % SAMPLE_BG == 0 and NB // SAMPLE_BG >= SAMPLE_MIN_STEPS and NT % SUBLANES == 0 and NT <= MLP_CHUNK and N_HEADS * NT == LANES
    assert cache_len == ATTN_REACH and rel_bias.shape[-1] == 2 * REL_CLIP + 1

    rows = NB + B
    rows_pad = -(-rows // SUBLANES) * SUBLANES
    c_all = jnp.concatenate([c_sample, c_prompt, jnp.zeros((rows_pad - rows, D_MODEL), F32)], axis=0)
    mixer_names, ffn_names = ("win", "woa", "wob", "wo"), ("wfi", "wfo")
    mixer_f32, ffn_f32 = (w_in, w_out_a, w_out_b, w_out), (w_ffn_in, w_ffn_out)
    mod_all, *first = _ada_call(c_all, w_ada, b_ada, mixer_f32)
    mod_all = mod_all.reshape(depth, rows_pad, 6, D_MODEL)
    w = dict(zip(mixer_names, first))

    head_of = jnp.arange(MXU_DIM) // HEAD_DIM
    tril = jnp.tril(jnp.ones((NT, NT), F32))
    ws_small = w_spatial[:, :, :NT, :NT] * tril
    params = {
        "n1g": norm1_g[:, None, :], "n2g": norm2_g[:, None, :],
        "qg": jnp.tile(q_norm_g, (1, N_HEADS))[:, None, :], "kg": jnp.tile(k_norm_g, (1, N_HEADS))[:, None, :],
        "vg": v_norm_g[:, None, :],
        "e_mat": jnp.where(head_of[:, None] == head_of[None, :], 1.0 / HEAD_DIM, 0.0).astype(BF16),
        "rel": _rel_vector(rel_bias),
        "wsp": w_spatial,
        "bsp_b": jnp.broadcast_to(b_spatial[:, :, :, None], (depth, N_GROUPS, MLP_CHUNK, GROUP_DIM)),
        "wsp_s": jnp.broadcast_to(ws_small[..., None], (depth, N_GROUPS, NT, NT, GROUP_DIM)),
        "bsp_s": jnp.broadcast_to(b_spatial[:, :, :NT, None], (depth, N_GROUPS, NT, GROUP_DIM)),
        "cw": ffn_conv_w, "cb": ffn_conv_b[:, None, :],
    }
    kc_all = cache_attn_k.transpose(0, 1, 3, 4, 2)
    vc_all = cache_attn_v.transpose(0, 1, 3, 4, 2)

    xp = x_prompt
    xs = x_sample.reshape(NB * NT, D_MODEL)
    conv_p, conv_s = [], []
    prompt_state, sample_state = (), ()
    for l in range(depth):
        xp, *outs = _mixer_prompt_call(l, depth, xp, mod_all, NB, params, w, ffn_f32, tuple(prompt_state))
        prompt_state, w_ffn = outs[:2], dict(zip(ffn_names, outs[2:]))
        nxt = mixer_f32 if l + 1 < depth else ()
        xp, cp, *outs = _ffn_prompt_call(l, l + 1, xp, mod_all, NB, params, w_ffn, nxt)
        xs, *sample_state = _mixer_sample_call(l, depth, xs, mod_all, kc_all, vc_all, params, w, NB, NT,
                                               tuple(sample_state))
        xs, cs = _ffn_sample_call(l, xs, mod_all, cache_ffn_conv, params, w_ffn, NB, NT)
        w = dict(zip(mixer_names, outs))
        conv_p.append(cp[:, SUBLANES - (CONV_W - 1):])
        conv_s.append(cs[:, SUBLANES - (CONV_W - 1):])
    new_k_prompt, new_v_prompt = (
        s.reshape(depth, B, N_HEADS, HEAD_DIM, ATTN_REACH).transpose(0, 1, 4, 2, 3) for s in prompt_state)
    ks, vs, sv = sample_state
    return (xp, xs.reshape(NB, NT, D_MODEL), new_k_prompt, new_v_prompt, jnp.stack(conv_p),
            ks.reshape(depth, NB, NT, N_HEADS, HEAD_DIM), vs.reshape(depth, NB, NT, N_HEADS, HEAD_DIM),
            sv.reshape(depth, NB, NT, N_GROUPS, GROUP_DIM), jnp.stack(conv_s))
```

```python
import functools

import jax
import jax.numpy as jnp
from jax import lax
from jax.experimental import pallas as pl
from jax.experimental.pallas import tpu as pltpu

F32 = jnp.float32
BF16 = jnp.bfloat16

D_MODEL = 1024
CHUNK = 64
ATTN_REACH = 512
N_HEADS = 8
HEAD_DIM = 64
D_A = N_HEADS * HEAD_DIM
REL_CLIP = 128
MLP_CHUNK = 128
N_GROUPS = 4
D_B = 512
GROUP_DIM = D_B // N_GROUPS
D_FF = 2816
CONV_W = 3
EPS = 1e-6

LANES = 128
SUBLANES = 8
BF16_ROWS = 16
MXU_DIM = 256
N_PAIRS = N_HEADS // 2
Q_BLOCK = 2 * CHUNK
KV_WINDOW = ATTN_REACH + Q_BLOCK
REL_SPAN = KV_WINDOW + Q_BLOCK
TILE_T = 512
FFN_TILE_T = 512
HIST = ATTN_REACH
FF_CHUNKS = ((0, 1024), (1024, 2048), (2048, 2560), (2560, D_FF))
ADA_BLOCK_N = 1536
SAMPLE_BG = 4
SAMPLE_MIN_STEPS = 8
VMEM_LIMIT = 56 * 1024 * 1024
_Q0, _K0, _V0, _B0, _G0 = 0, D_A, 2 * D_A, 3 * D_A, 3 * D_A + 2 * D_B
D_IN = _G0 + 2 * D_MODEL


def _dot(a, b):
    return jnp.dot(a, b, preferred_element_type=F32)


def _dot_nt(a, b):
    return lax.dot_general(a, b, (((1,), (1,)), ((), ())), preferred_element_type=F32)


def _dot_tn(a, b):
    return lax.dot_general(a, b, (((0,), (0,)), ((), ())), preferred_element_type=F32)


def _rms(x, g):
    return (x * lax.rsqrt(jnp.mean(x * x, axis=-1, keepdims=True) + EPS)) * g


def _head_rms(a, g, e_ref):
    sq = a * a
    hi = sq.astype(BF16)
    lo = (sq - hi.astype(F32)).astype(BF16)
    e = e_ref[...]
    w = e.shape[0]
    ms = jnp.concatenate([_dot(hi[:, c:c + w], e) + _dot(lo[:, c:c + w], e) for c in range(0, a.shape[1], w)],
                         axis=1)
    return (a * lax.rsqrt(ms + EPS)) * g


def _rel_rows(rel_ref, h, n_rows):
    wb = jnp.broadcast_to(rel_ref[h:h + 1, :], (n_rows, REL_SPAN))
    return pltpu.roll(wb, 0, axis=1, stride=1, stride_axis=0)


def _const_spec(shape):
    n = len(shape)
    return pl.BlockSpec(shape, lambda *_: (0,) * n, pipeline_mode=pl.Buffered(1))


def _cast_specs(l, stacked, n_steps, step_of):
    in_specs, out_specs, out_shapes = [], [], []
    for a in stacked:
        rows, cols = a.shape[1:]
        blk = next(r for r in range(BF16_ROWS, rows + 1, BF16_ROWS) if rows % r == 0 and rows // r <= n_steps)
        last = rows // blk - 1
        in_specs.append(pl.BlockSpec((None, blk, cols), lambda *ids, last=last: (l, jnp.minimum(step_of(*ids), last), 0)))
        out_specs.append(pl.BlockSpec((blk, cols), lambda *ids, last=last: (jnp.minimum(step_of(*ids), last), 0)))
        out_shapes.append(jax.ShapeDtypeStruct((rows, cols), BF16))
    return in_specs, out_specs, out_shapes


def _layer_spec(l, shape):
    n = len(shape)
    return pl.BlockSpec((None, *shape), lambda *_: (l,) + (0,) * n, pipeline_mode=pl.Buffered(1))


def _ada_kernel(c_ref, w_ref, b_ref, *rest, n_cast):
    cast_in, o_ref, cast_out = rest[:n_cast], rest[n_cast], rest[n_cast + 1:]
    s = jax.nn.silu(c_ref[...]).astype(BF16)
    o_ref[...] = _dot(s, w_ref[...].astype(BF16)) + b_ref[...]
    for src, dst in zip(cast_in, cast_out):
        dst[...] = src[...].astype(BF16)


def _ada_call(c_all, w_ada, b_ada, to_cast):
    depth = w_ada.shape[0]
    rows = c_all.shape[0]
    n_out = w_ada.shape[2]
    n_blk = n_out // ADA_BLOCK_N
    cast_in, cast_out, cast_shapes = _cast_specs(0, to_cast, depth * n_blk, lambda l, n: l * n_blk + n)
    return pl.pallas_call(
        functools.partial(_ada_kernel, n_cast=len(to_cast)),
        grid=(depth, n_blk),
        in_specs=[
            pl.BlockSpec((rows, D_MODEL), lambda l, n: (0, 0)),
            pl.BlockSpec((None, D_MODEL, ADA_BLOCK_N), lambda l, n: (l, 0, n)),
            pl.BlockSpec((None, 1, ADA_BLOCK_N), lambda l, n: (l, 0, n)),
        ] + cast_in,
        out_specs=[pl.BlockSpec((None, rows, ADA_BLOCK_N), lambda l, n: (l, 0, n))] + cast_out,
        out_shape=[jax.ShapeDtypeStruct((depth, rows, n_out), F32)] + cast_shapes,
        compiler_params=pltpu.CompilerParams(
            dimension_semantics=("arbitrary", "arbitrary"), vmem_limit_bytes=VMEM_LIMIT),
        name="ada_mod",
    )(c_all, w_ada, b_ada.reshape(depth, 1, n_out), *to_cast)


def _mixer_prompt_kernel(x_ref, mod_ref, n1g_ref, win_ref, qg_ref, kg_ref, e_ref, vg_ref, rel_ref,
                         wsp_ref, bsp_ref, woa_ref, wob_ref, wo_ref, *rest, n_cast, n_carried):
    cast_in, rest = rest[:n_cast], rest[n_cast + n_carried:]
    xo_ref, kst_ref, vst_ref = rest[:3]
    cast_out = rest[3:3 + n_cast]
    kbuf, vtbuf, oat_scr, ob_scr, bias_scr, st_scr, e_scr = rest[3 + n_cast:]
    b = pl.program_id(0)
    t = pl.program_id(1)
    T = TILE_T
    for src, dst in zip(cast_in, cast_out):
        dst[...] = src[...].astype(BF16)

    @pl.when((b == 0) & (t == 0))
    def _():
        qi = lax.broadcasted_iota(jnp.int32, (Q_BLOCK, KV_WINDOW), 0) // CHUNK
        ki = lax.broadcasted_iota(jnp.int32, (Q_BLOCK, KV_WINDOW), 1) // CHUNK
        band = (ki >= qi) & (ki <= qi + ATTN_REACH // CHUNK)
        for h in range(N_HEADS):
            tab = jnp.where(band, _rel_rows(rel_ref, h, Q_BLOCK)[:, 0:KV_WINDOW], -jnp.inf)
            bias_scr[h // 2, 0:KV_WINDOW, (h % 2) * Q_BLOCK:(h % 2 + 1) * Q_BLOCK] = tab.T
        bias_scr[:, KV_WINDOW:, :] = jnp.full((N_PAIRS, Q_BLOCK, 2 * Q_BLOCK), -jnp.inf, F32)

    @pl.when(t == 0)
    def _():
        kbuf[0:HIST, :] = jnp.zeros((HIST, D_A), BF16)
        vtbuf[:, 0:HIST] = jnp.zeros((D_A, HIST), BF16)

    @pl.when(t > 0)
    def _():
        for r in range(0, HIST, T):
            kbuf[r:r + T, :] = kbuf[r + T:r + 2 * T, :]
            vtbuf[:, r:r + T] = vtbuf[:, r + T:r + 2 * T]

    mod = mod_ref[...]
    sh1, sc1, gt1 = mod[0:1], mod[1:2], mod[2:3]

    x = x_ref[...]
    h = (_rms(x, n1g_ref[...]) * (1.0 + sc1) + sh1).astype(BF16)

    q = _head_rms(_dot(h, win_ref[:, _Q0:_Q0 + D_A]), qg_ref[...], e_ref)
    k = _head_rms(_dot(h, win_ref[:, _K0:_K0 + D_A]), kg_ref[...], e_ref)
    v = _dot(h, win_ref[:, _V0:_V0 + D_A])
    vt = v.T
    kbuf[HIST:HIST + T, :] = k.astype(BF16)
    vtbuf[:, HIST:HIST + T] = vt.astype(BF16)

    kst_ref[...] = k.T
    vst_ref[...] = vt

    qb = (q * (HEAD_DIM ** -0.5)).astype(BF16)

    lane = lax.broadcasted_iota(jnp.int32, (1, LANES), 1)
    low_half = lane < HEAD_DIM

    n_qb = T // Q_BLOCK

    def scores(p, j, slot):
        r0, c0 = j * Q_BLOCK, p * LANES
        qp = qb[r0:r0 + Q_BLOCK, c0:c0 + LANES]
        q2 = jnp.concatenate([jnp.where(low_half, qp, jnp.zeros_like(qp)),
                              jnp.where(low_half, jnp.zeros_like(qp), qp)], axis=0)
        st = _dot_nt(kbuf[r0:r0 + KV_WINDOW, c0:c0 + LANES], q2)
        first_valid = HIST - t * T - r0
        m = None
        for r in range(0, KV_WINDOW, Q_BLOCK):
            src = jnp.where(first_valid > r, KV_WINDOW, r) if r < HIST else r
            blk = st[r:r + Q_BLOCK] + bias_scr[p, pl.ds(pl.multiple_of(src, Q_BLOCK), Q_BLOCK), :]
            st_scr[slot, j, r:r + Q_BLOCK, :] = blk
            bm = jnp.max(blk, axis=0, keepdims=True)
            m = bm if m is None else jnp.maximum(m, bm)
        return m

    def weights(j, slot, m):
        e = jnp.exp(st_scr[slot, j] - m)
        e_scr[slot, j] = e.astype(BF16)
        return jnp.sum(e, axis=0, keepdims=True)

    def outputs(p, j, slot, total):
        r0, c0 = j * Q_BLOCK, p * LANES
        ot = _dot(vtbuf[c0:c0 + LANES, r0:r0 + KV_WINDOW], e_scr[slot, j]) / total
        oat_scr[c0:c0 + HEAD_DIM, r0:r0 + Q_BLOCK] = ot[0:HEAD_DIM, 0:Q_BLOCK]
        oat_scr[c0 + HEAD_DIM:c0 + LANES, r0:r0 + Q_BLOCK] = ot[HEAD_DIM:, Q_BLOCK:]

    side = {}

    def project(name, act, c0):
        side[name] = act(_dot(h, win_ref[:, c0:c0 + MXU_DIM]))

    side_work = ([functools.partial(project, ("b", i), jax.nn.gelu, _B0 + i * MXU_DIM)
                  for i in range(2 * D_B // MXU_DIM)]
                 + [functools.partial(project, ("g", i), jax.nn.sigmoid, _G0 + i * MXU_DIM)
                    for i in range(2 * D_MODEL // MXU_DIM)])
    maxima = [scores(0, j, 0) for j in range(n_qb)]
    for p in range(N_PAIRS):
        slot = p % 2
        if p + 1 < N_PAIRS:
            next_maxima = [scores(p + 1, j, 1 - slot) for j in range(n_qb)]
        for work in side_work[len(side_work) * p // N_PAIRS:len(side_work) * (p + 1) // N_PAIRS]:
            work()
        totals = [weights(j, slot, maxima[j]) for j in range(n_qb)]
        for j in range(n_qb):
            outputs(p, j, slot, totals[j])
        maxima = next_maxima
    n_b, n_g = D_B // MXU_DIM, D_MODEL // MXU_DIM
    ub = jnp.concatenate([side["b", i] for i in range(n_b)], axis=1)
    vbn = _rms(jnp.concatenate([side["b", n_b + i] for i in range(n_b)], axis=1), vg_ref[...]).astype(BF16)
    ga = jnp.concatenate([side["g", i] for i in range(n_g)], axis=1)
    gb = jnp.concatenate([side["g", n_g + i] for i in range(n_g)], axis=1)
    row_i = lax.broadcasted_iota(jnp.int32, (MLP_CHUNK, MLP_CHUNK), 0)
    col_i = lax.broadcasted_iota(jnp.int32, (MLP_CHUNK, MLP_CHUNK), 1)
    for g in range(N_GROUPS):
        wc = jnp.where(row_i >= col_i, wsp_ref[g], 0.0).astype(BF16)
        g0 = g * GROUP_DIM
        for c in range(T // MLP_CHUNK):
            r0 = c * MLP_CHUNK
            mix = _dot(wc, vbn[r0:r0 + MLP_CHUNK, g0:g0 + GROUP_DIM]) + bsp_ref[g]
            ob_scr[r0:r0 + MLP_CHUNK, g0:g0 + GROUP_DIM] = ub[r0:r0 + MLP_CHUNK, g0:g0 + GROUP_DIM] * mix

    merged = (ga * _dot_tn(oat_scr[...].astype(BF16), woa_ref[...])
              + gb * _dot(ob_scr[...].astype(BF16), wob_ref[...]))
    xo_ref[...] = x + gt1 * _dot(merged.astype(BF16), wo_ref[...])


def _mixer_prompt_call(l, depth, x, mod_all, mod_row0, p, w, to_cast, carried):
    B, S, _ = x.shape
    T = TILE_T
    n_t = S // T
    tok = pl.BlockSpec((None, T, D_MODEL), lambda b, t: (b, t, 0))
    state = pl.BlockSpec((None, None, D_A, ATTN_REACH), lambda b, t: (l, b, 0, 0))
    cast_in, cast_out, cast_shapes = _cast_specs(l, to_cast, B * n_t, lambda b, t: b * n_t + t)
    in_specs = [
        tok,
        pl.BlockSpec((None, None, 6, D_MODEL), lambda b, t: (l, mod_row0 + b, 0, 0)),
        _layer_spec(l, (1, D_MODEL)),
        _const_spec((D_MODEL, D_IN)),
        _layer_spec(l, (1, D_A)),
        _layer_spec(l, (1, D_A)),
        _const_spec((MXU_DIM, MXU_DIM)),
        _layer_spec(l, (1, D_B)),
        _layer_spec(l, (N_HEADS, REL_SPAN)),
        _layer_spec(l, (N_GROUPS, MLP_CHUNK, MLP_CHUNK)),
        _layer_spec(l, (N_GROUPS, MLP_CHUNK, GROUP_DIM)),
        _const_spec((D_A, D_MODEL)),
        _const_spec((D_B, D_MODEL)),
        _const_spec((D_MODEL, D_MODEL)),
    ]
    in_specs += cast_in
    return pl.pallas_call(
        functools.partial(_mixer_prompt_kernel, n_cast=len(to_cast), n_carried=len(carried)),
        grid=(B, n_t),
        input_output_aliases={len(in_specs) + i: 1 + i for i in range(len(carried))},
        in_specs=in_specs + [pl.BlockSpec(memory_space=pl.ANY)] * len(carried),
        out_specs=[tok, state, state] + cast_out,
        out_shape=[
            jax.ShapeDtypeStruct((B, S, D_MODEL), F32),
            jax.ShapeDtypeStruct((depth, B, D_A, ATTN_REACH), F32),
            jax.ShapeDtypeStruct((depth, B, D_A, ATTN_REACH), F32),
        ] + cast_shapes,
        scratch_shapes=[
            pltpu.VMEM((HIST + T, D_A), BF16),
            pltpu.VMEM((D_A, HIST + T), BF16),
            pltpu.VMEM((D_A, T), F32),
            pltpu.VMEM((T, D_B), F32),
            pltpu.VMEM((N_PAIRS, KV_WINDOW + Q_BLOCK, 2 * Q_BLOCK), F32),
            pltpu.VMEM((2, T // Q_BLOCK, KV_WINDOW, 2 * Q_BLOCK), F32),
            pltpu.VMEM((2, T // Q_BLOCK, KV_WINDOW, 2 * Q_BLOCK), BF16),
        ],
        compiler_params=pltpu.CompilerParams(
            dimension_semantics=("arbitrary", "arbitrary"), vmem_limit_bytes=VMEM_LIMIT),
        name="mixer_prompt",
    )(x, mod_all, p["n1g"], w["win"], p["qg"], p["kg"], p["e_mat"], p["vg"], p["rel"],
      p["wsp"], p["bsp_b"], w["woa"], w["wob"], w["wo"], *to_cast, *carried)


def _conv_gate(g, u, prev1, prev2, row, cw_ref, cb_ref, c0, c1):
    gm1 = jnp.where(row == 0, prev1, pltpu.roll(g, 1, axis=0))
    gm2 = jnp.where(row == 0, prev2, jnp.where(row == 1, prev1, pltpu.roll(g, 2, axis=0)))
    gc = cb_ref[:, c0:c1] + cw_ref[0:1, c0:c1] * gm2
    gc = gc + cw_ref[1:2, c0:c1] * gm1
    gc = gc + cw_ref[2:3, c0:c1] * g
    return jax.nn.gelu(gc) * u


def _conv_ffn(x, sh2, sc2, gt2, prev1, prev2, row, n2g_ref, wfi_ref, cw_ref, cb_ref, wfo_ref):
    h2 = (_rms(x, n2g_ref[...]) * (1.0 + sc2) + sh2).astype(BF16)

    def up_project(c0, c1):
        return _dot(h2, wfi_ref[:, c0:c1]), _dot(h2, wfi_ref[:, D_FF + c0:D_FF + c1])

    f = jnp.zeros(x.shape, F32)
    gs = []
    ahead = up_project(*FF_CHUNKS[0])
    for i, (c0, c1) in enumerate(FF_CHUNKS):
        g, u = ahead
        if i + 1 < len(FF_CHUNKS):
            ahead = up_project(*FF_CHUNKS[i + 1])
        act = _conv_gate(g, u, prev1[:, c0:c1], prev2[:, c0:c1], row, cw_ref, cb_ref, c0, c1)
        f = f + _dot(act.astype(BF16), wfo_ref[c0:c1, :])
        gs.append(g)
    return x + gt2 * f, gs


def _ffn_kernel(x_ref, mod_ref, xs_ref, mods_ref, cc_ref, n2g_ref, wfi_ref, cw_ref, cb_ref, wfo_ref, *rest,
                n_cast, n_prompt_steps, n_t, n_batch, n_tok):
    cast_in, rest = rest[:n_cast], rest[n_cast:]
    xo_ref, cs_ref, xso_ref, css_ref = rest[:4]
    cast_out, carry = rest[4:-1], rest[-1]
    step = pl.program_id(0)
    for src, dst in zip(cast_in, cast_out):
        dst[...] = src[...].astype(BF16)

    @pl.when(step < n_prompt_steps)
    def _():
        T = FFN_TILE_T
        mod = mod_ref[...]

        @pl.when(step % n_t == 0)
        def _():
            carry[...] = jnp.zeros(carry.shape, F32)

        row = lax.broadcasted_iota(jnp.int32, (T, 1), 0)
        prev = carry[...]
        xo, gs = _conv_ffn(x_ref[...], mod[3:4], mod[4:5], mod[5:6], prev[SUBLANES - 1:],
                           prev[SUBLANES - 2:SUBLANES - 1], row, n2g_ref, wfi_ref, cw_ref, cb_ref, wfo_ref)
        xo_ref[...] = xo
        tail = jnp.concatenate([g[T - SUBLANES:T, :] for g in gs], axis=1)
        carry[...] = tail
        cs_ref[...] = tail

    @pl.when(step == n_prompt_steps)
    def _():
        n_rows = n_batch * n_tok

        def per_token(m):
            return jnp.broadcast_to(m, (n_batch, n_tok, m.shape[-1])).reshape(n_rows, m.shape[-1])

        mod = mods_ref[...]
        cc = cc_ref[...]
        row = lax.broadcasted_iota(jnp.int32, (n_rows, 1), 0) % n_tok
        xo, gs = _conv_ffn(xs_ref[...], per_token(mod[:, 3:4, :]), per_token(mod[:, 4:5, :]),
                           per_token(mod[:, 5:6, :]), per_token(cc[:, 1:2, :]), per_token(cc[:, 0:1, :]), row,
                           n2g_ref, wfi_ref, cw_ref, cb_ref, wfo_ref)
        xso_ref[...] = xo
        for g, (c0, c1) in zip(gs, FF_CHUNKS):
            css_ref[:, :, c0:c1] = g.reshape(n_batch, n_tok, c1 - c0)[:, n_tok - SUBLANES:, :]


def _ffn_call(l, cast_l, x, xs, mod_all, mod_row0, cc_all, p, w, to_cast, n_batch, n_tok):
    B, S, _ = x.shape
    T = FFN_TILE_T
    n_t = S // T
    n_p = B * n_t
    n_rows = n_batch * n_tok
    tile = lambda s: jnp.minimum(s, n_p - 1)
    tok = pl.BlockSpec((None, T, D_MODEL), lambda s: (tile(s) // n_t, tile(s) % n_t, 0))
    cast_in, cast_out, cast_shapes = _cast_specs(cast_l, to_cast, n_p, lambda s: s)
    return pl.pallas_call(
        functools.partial(_ffn_kernel, n_cast=len(to_cast), n_prompt_steps=n_p, n_t=n_t, n_batch=n_batch,
                          n_tok=n_tok),
        grid=(n_p + 1,),
        in_specs=[
            tok,
            pl.BlockSpec((None, None, 6, D_MODEL), lambda s: (l, mod_row0 + tile(s) // n_t, 0, 0)),
            _const_spec((n_rows, D_MODEL)),
            _layer_spec(l, (n_batch, 6, D_MODEL)),
            _layer_spec(l, (n_batch, CONV_W - 1, D_FF)),
            _layer_spec(l, (1, D_MODEL)),
            _const_spec((D_MODEL, 2 * D_FF)),
            _layer_spec(l, (CONV_W, D_FF)),
            _layer_spec(l, (1, D_FF)),
            _const_spec((D_FF, D_MODEL)),
        ] + cast_in,
        out_specs=[
            tok,
            pl.BlockSpec((None, SUBLANES, D_FF), lambda s: (tile(s) // n_t, 0, 0)),
            pl.BlockSpec((n_rows, D_MODEL), lambda s: (0, 0)),
            pl.BlockSpec((n_batch, SUBLANES, D_FF), lambda s: (0, 0, 0)),
        ] + cast_out,
        out_shape=[
            jax.ShapeDtypeStruct((B, S, D_MODEL), F32),
            jax.ShapeDtypeStruct((B, SUBLANES, D_FF), F32),
            jax.ShapeDtypeStruct((n_rows, D_MODEL), F32),
            jax.ShapeDtypeStruct((n_batch, SUBLANES, D_FF), F32),
        ] + cast_shapes,
        scratch_shapes=[pltpu.VMEM((SUBLANES, D_FF), F32)],
        compiler_params=pltpu.CompilerParams(dimension_semantics=("arbitrary",), vmem_limit_bytes=VMEM_LIMIT),
        name="ffn",
    )(x, mod_all, xs, mod_all, cc_all, p["n2g"], w["wfi"], p["cw"], p["cb"], w["wfo"], *to_cast)


def _mixer_sample_kernel(x_ref, mod_ref, n1g_ref, win_ref, qg_ref, kg_ref, e_ref, vg_ref, rel_ref,
                         kc_ref, vc_ref, wsp_ref, bsp_ref, woa_ref, wob_ref, wo_ref, *rest,
                         n_carried, n_batch, n_tok, cache_len):
    (xo_ref, ks_ref, vs_ref, sv_ref,
     h_scr, q_scr, kn_scr, vn_scr, oa_scr, ub_scr, vb_scr, ob_scr, gate_scr, bias_scr) = rest[n_carried:]
    step = pl.program_id(0)
    n_rows = n_batch * n_tok

    def per_token(m):
        return jnp.broadcast_to(m, (n_batch, n_tok, m.shape[-1])).reshape(n_rows, m.shape[-1])

    @pl.when(step == 0)
    def _():
        mod = mod_ref[...]
        sh1, sc1 = per_token(mod[:, 0:1, :]), per_token(mod[:, 1:2, :])
        h = (_rms(x_ref[...], n1g_ref[...]) * (1.0 + sc1) + sh1).astype(BF16)
        h_scr[...] = h
        q = _head_rms(_dot(h, win_ref[:, _Q0:_Q0 + D_A]), qg_ref[...], e_ref)
        k = _head_rms(_dot(h, win_ref[:, _K0:_K0 + D_A]), kg_ref[...], e_ref)
        v = _dot(h, win_ref[:, _V0:_V0 + D_A])
        ks_ref[...] = k
        vs_ref[...] = v
        kn_scr[...] = k.astype(BF16)
        vn_scr[...] = v.astype(BF16)
        q_scr[...] = (q * (HEAD_DIM ** -0.5)).astype(BF16)
        for hh in range(N_HEADS):
            bias_scr[hh * n_tok:(hh + 1) * n_tok, :] = _rel_rows(rel_ref, hh, n_tok)[:, 0:cache_len + LANES]

    lane_head = lax.broadcasted_iota(jnp.int32, (n_tok, D_A), 1) // HEAD_DIM
    for i in range(SAMPLE_BG):
        r0 = pl.multiple_of((step * SAMPLE_BG + i) * n_tok, n_tok)
        qb = q_scr[pl.ds(r0, n_tok), :]
        q_bd = jnp.concatenate([jnp.where(lane_head == hh, qb, jnp.zeros_like(qb)) for hh in range(N_HEADS)],
                               axis=0)
        kn = kn_scr[pl.ds(r0, n_tok), :]
        vn = vn_scr[pl.ds(r0, n_tok), :]
        kt = kc_ref[i].reshape(D_A, cache_len).astype(BF16)
        vt = vc_ref[i].reshape(D_A, cache_len).astype(BF16)
        s_c = _dot(q_bd, kt) + bias_scr[:, 0:cache_len]
        s_n = _dot_nt(q_bd, kn) + bias_scr[:, cache_len:cache_len + n_tok]
        m = jnp.maximum(jnp.max(s_c, axis=-1, keepdims=True), jnp.max(s_n, axis=-1, keepdims=True))
        e_c = jnp.exp(s_c - m)
        e_n = jnp.exp(s_n - m)
        l = jnp.sum(e_c, axis=-1, keepdims=True) + jnp.sum(e_n, axis=-1, keepdims=True)
        o_all = (_dot_nt(e_c.astype(BF16), vt) + _dot(e_n.astype(BF16), vn)) / l
        o = jnp.zeros((n_tok, D_A), F32)
        for hh in range(N_HEADS):
            o = jnp.where(lane_head == hh, o_all[hh * n_tok:(hh + 1) * n_tok, :], o)
        oa_scr[pl.ds(r0, n_tok), :] = o

    def spatial_half(half_scr, c0):
        z = jax.nn.gelu(_dot(h_scr[...], win_ref[:, c0:c0 + D_B]))
        for g in range(N_GROUPS):
            half_scr[g] = z[:, g * GROUP_DIM:(g + 1) * GROUP_DIM]

    def spatial_mix():
        vbn = _rms(jnp.concatenate([vb_scr[g] for g in range(N_GROUPS)], axis=1), vg_ref[...])
        sv_ref[...] = vbn
        for g in range(N_GROUPS):
            vb_scr[g] = vbn[:, g * GROUP_DIM:(g + 1) * GROUP_DIM]
        for g in range(N_GROUPS):
            at_pos = [vb_scr[g, pl.ds(s, n_batch, stride=n_tok), :] for s in range(n_tok)]
            for t in range(n_tok):
                mix = bsp_ref[g, t:t + 1, :]
                for s in range(t + 1):
                    mix = mix + wsp_ref[g, t, s:s + 1, :] * at_pos[s]
                ob_scr[g, pl.ds(t, n_batch, stride=n_tok), :] = ub_scr[g, pl.ds(t, n_batch, stride=n_tok), :] * mix

    def gates(c0, width):
        gate_scr[:, c0:c0 + width] = jax.nn.sigmoid(_dot(h_scr[...], win_ref[:, _G0 + c0:_G0 + c0 + width]))

    stages = [functools.partial(spatial_half, ub_scr, _B0), functools.partial(spatial_half, vb_scr, _B0 + D_B),
              spatial_mix] + [functools.partial(gates, c0, D_B) for c0 in range(0, 2 * D_MODEL, D_B)]
    for i, stage in enumerate(stages):
        pl.when(step == i + 1)(stage)

    @pl.when(step == pl.num_programs(0) - 1)
    def _():
        ob = jnp.concatenate([ob_scr[g] for g in range(N_GROUPS)], axis=1)
        merged = (gate_scr[:, 0:D_MODEL] * _dot(oa_scr[...].astype(BF16), woa_ref[...])
                  + gate_scr[:, D_MODEL:] * _dot(ob.astype(BF16), wob_ref[...]))
        gt1 = per_token(mod_ref[...][:, 2:3, :])
        xo_ref[...] = x_ref[...] + gt1 * _dot(merged.astype(BF16), wo_ref[...])


def _mixer_sample_call(l, depth, x2, mod_all, kc_all, vc_all, p, w, n_batch, n_tok, carried):
    n_rows = n_batch * n_tok
    cache_len = kc_all.shape[-1]
    cache = pl.BlockSpec((None, SAMPLE_BG, N_HEADS, HEAD_DIM, cache_len), lambda s: (l, s, 0, 0, 0))
    full = lambda shape: pl.BlockSpec(shape, lambda s: (0,) * len(shape))
    state = lambda width: pl.BlockSpec((None, n_rows, width), lambda s: (l, 0, 0))
    in_specs = [
        _const_spec((n_rows, D_MODEL)),
        _layer_spec(l, (n_batch, 6, D_MODEL)),
        _layer_spec(l, (1, D_MODEL)),
        _const_spec((D_MODEL, D_IN)),
        _layer_spec(l, (1, D_A)),
        _layer_spec(l, (1, D_A)),
        _const_spec((MXU_DIM, MXU_DIM)),
        _layer_spec(l, (1, D_B)),
        _layer_spec(l, (N_HEADS, REL_SPAN)),
        cache,
        cache,
        _layer_spec(l, (N_GROUPS, n_tok, n_tok, GROUP_DIM)),
        _layer_spec(l, (N_GROUPS, n_tok, GROUP_DIM)),
        _const_spec((D_A, D_MODEL)),
        _const_spec((D_B, D_MODEL)),
        _const_spec((D_MODEL, D_MODEL)),
    ]
    return pl.pallas_call(
        functools.partial(_mixer_sample_kernel, n_carried=len(carried), n_batch=n_batch, n_tok=n_tok,
                          cache_len=cache_len),
        grid=(n_batch // SAMPLE_BG,),
        input_output_aliases={len(in_specs) + i: 1 + i for i in range(len(carried))},
        in_specs=in_specs + [pl.BlockSpec(memory_space=pl.ANY)] * len(carried),
        out_specs=[full((n_rows, D_MODEL)), state(D_A), state(D_A), state(D_B)],
        out_shape=[
            jax.ShapeDtypeStruct((n_rows, D_MODEL), F32),
            jax.ShapeDtypeStruct((depth, n_rows, D_A), F32),
            jax.ShapeDtypeStruct((depth, n_rows, D_A), F32),
            jax.ShapeDtypeStruct((depth, n_rows, D_B), F32),
        ],
        scratch_shapes=[
            pltpu.VMEM((n_rows, D_MODEL), BF16),
            pltpu.VMEM((n_rows, D_A), BF16),
            pltpu.VMEM((n_rows, D_A), BF16),
            pltpu.VMEM((n_rows, D_A), BF16),
            pltpu.VMEM((n_rows, D_A), F32),
            pltpu.VMEM((N_GROUPS, n_rows, GROUP_DIM), F32),
            pltpu.VMEM((N_GROUPS, n_rows, GROUP_DIM), F32),
            pltpu.VMEM((N_GROUPS, n_rows, GROUP_DIM), F32),
            pltpu.VMEM((n_rows, 2 * D_MODEL), F32),
            pltpu.VMEM((N_HEADS * n_tok, cache_len + LANES), F32),
        ],
        compiler_params=pltpu.CompilerParams(
            dimension_semantics=("arbitrary",), vmem_limit_bytes=VMEM_LIMIT),
        name="mixer_sample",
    )(x2, mod_all, p["n1g"], w["win"], p["qg"], p["kg"], p["e_mat"], p["vg"], p["rel"],
      kc_all, vc_all, p["wsp_s"], p["bsp_s"], w["woa"], w["wob"], w["wo"], *carried)


def _rel_vector(rel_bias):
    far = rel_bias[..., 2 * REL_CLIP:]
    lead = ATTN_REACH - REL_CLIP
    return jnp.concatenate([
        jnp.broadcast_to(far, (*rel_bias.shape[:-1], lead)),
        rel_bias[..., ::-1],
        jnp.broadcast_to(far, (*rel_bias.shape[:-1], REL_SPAN - lead - (2 * REL_CLIP + 1))),
    ], axis=-1)


def kernel(x_prompt, x_sample, cache_attn_k, cache_attn_v, cache_ffn_conv, c_prompt, c_sample, norm1_g, norm2_g, w_ada, b_ada, w_in, q_norm_g, k_norm_g, rel_bias, v_norm_g, w_spatial, b_spatial, w_out_a, w_out_b, w_out, w_ffn_in, ffn_conv_w, ffn_conv_b, w_ffn_out):
    depth = w_in.shape[0]
    B, S, _ = x_prompt.shape
    NB, NT, _ = x_sample.shape
    cache_len = cache_attn_k.shape[2]
    assert S % FFN_TILE_T == 0 and S % TILE_T == 0 and TILE_T % Q_BLOCK == 0 and TILE_T == ATTN_REACH and S >= ATTN_REACH
    assert NB % SAMPLE_BG == 0 and NB // SAMPLE_BG >= SAMPLE_MIN_STEPS and NT % SUBLANES == 0 and NT <= MLP_CHUNK and N_HEADS * NT == LANES
    assert cache_len == ATTN_REACH and rel_bias.shape[-1] == 2 * REL_CLIP + 1

    rows = NB + B
    rows_pad = -(-rows // SUBLANES) * SUBLANES
    c_all = jnp.concatenate([c_sample, c_prompt, jnp.zeros((rows_pad - rows, D_MODEL), F32)], axis=0)
    mixer_names, ffn_names = ("win", "woa", "wob", "wo"), ("wfi", "wfo")
    mixer_f32, ffn_f32 = (w_in, w_out_a, w_out_b, w_out), (w_ffn_in, w_ffn_out)
    mod_all, *first = _ada_call(c_all, w_ada, b_ada, mixer_f32)
    mod_all = mod_all.reshape(depth, rows_pad, 6, D_MODEL)
    w = dict(zip(mixer_names, first))

    head_of = jnp.arange(MXU_DIM) // HEAD_DIM
    tril = jnp.tril(jnp.ones((NT, NT), F32))
    ws_small = w_spatial[:, :, :NT, :NT] * tril
    params = {
        "n1g": norm1_g[:, None, :], "n2g": norm2_g[:, None, :],
        "qg": jnp.tile(q_norm_g, (1, N_HEADS))[:, None, :], "kg": jnp.tile(k_norm_g, (1, N_HEADS))[:, None, :],
        "vg": v_norm_g[:, None, :],
        "e_mat": jnp.where(head_of[:, None] == head_of[None, :], 1.0 / HEAD_DIM, 0.0).astype(BF16),
        "rel": _rel_vector(rel_bias),
        "wsp": w_spatial,
        "bsp_b": jnp.broadcast_to(b_spatial[:, :, :, None], (depth, N_GROUPS, MLP_CHUNK, GROUP_DIM)),
        "wsp_s": jnp.broadcast_to(ws_small[..., None], (depth, N_GROUPS, NT, NT, GROUP_DIM)),
        "bsp_s": jnp.broadcast_to(b_spatial[:, :, :NT, None], (depth, N_GROUPS, NT, GROUP_DIM)),
        "cw": ffn_conv_w, "cb": ffn_conv_b[:, None, :],
    }
    kc_all = cache_attn_k.transpose(0, 1, 3, 4, 2)
    vc_all = cache_attn_v.transpose(0, 1, 3, 4, 2)

    xp = x_prompt
    xs = x_sample.reshape(NB * NT, D_MODEL)
    conv_p, conv_s = [], []
    prompt_state, sample_state = (), ()
    for l in range(depth):
        xp, *outs = _mixer_prompt_call(l, depth, xp, mod_all, NB, params, w, ffn_f32, tuple(prompt_state))
        prompt_state, w_ffn = outs[:2], dict(zip(ffn_names, outs[2:]))
        xs, *sample_state = _mixer_sample_call(l, depth, xs, mod_all, kc_all, vc_all, params, w, NB, NT,
                                               tuple(sample_state))
        nxt = mixer_f32 if l + 1 < depth else ()
        xp, cp, xs, cs, *outs = _ffn_call(l, l + 1, xp, xs, mod_all, NB, cache_ffn_conv, params, w_ffn, nxt, NB, NT)
        w = dict(zip(mixer_names, outs))
        conv_p.append(cp[:, SUBLANES - (CONV_W - 1):])
        conv_s.append(cs[:, SUBLANES - (CONV_W - 1):])
    new_k_prompt, new_v_prompt = (
        s.reshape(depth, B, N_HEADS, HEAD_DIM, ATTN_REACH).transpose(0, 1, 4, 2, 3) for s in prompt_state)
    ks, vs, sv = sample_state
    return (xp, xs.reshape(NB, NT, D_MODEL), new_k_prompt, new_v_prompt, jnp.stack(conv_p),
            ks.reshape(depth, NB, NT, N_HEADS, HEAD_DIM), vs.reshape(depth, NB, NT, N_HEADS, HEAD_DIM),
            sv.reshape(depth, NB, NT, N_GROUPS, GROUP_DIM), jnp.stack(conv_s))
```

```python
import functools

import jax
import jax.numpy as jnp
from jax import lax
from jax.experimental import pallas as pl
from jax.experimental.pallas import tpu as pltpu

F32 = jnp.float32
BF16 = jnp.bfloat16

D_MODEL = 1024
CHUNK = 64
ATTN_REACH = 512
N_HEADS = 8
HEAD_DIM = 64
D_A = N_HEADS * HEAD_DIM
REL_CLIP = 128
MLP_CHUNK = 128
N_GROUPS = 4
D_B = 512
GROUP_DIM = D_B // N_GROUPS
D_FF = 2816
CONV_W = 3
EPS = 1e-6

LANES = 128
SUBLANES = 8
BF16_ROWS = 16
MXU_DIM = 256
N_PAIRS = N_HEADS // 2
Q_BLOCK = 2 * CHUNK
KV_WINDOW = ATTN_REACH + Q_BLOCK
REL_SPAN = KV_WINDOW + Q_BLOCK
TILE_T = 512
FFN_TILE_T = 512
HIST = ATTN_REACH
FF_CHUNKS = ((0, 1024), (1024, 2048), (2048, 2560), (2560, D_FF))
ADA_BLOCK_N = 1536
SAMPLE_BG = 4
SAMPLE_MIN_STEPS = 8
VMEM_LIMIT = 56 * 1024 * 1024
_Q0, _K0, _V0, _B0, _G0 = 0, D_A, 2 * D_A, 3 * D_A, 3 * D_A + 2 * D_B
D_IN = _G0 + 2 * D_MODEL


def _dot(a, b):
    return jnp.dot(a, b, preferred_element_type=F32)


def _dot_nt(a, b):
    return lax.dot_general(a, b, (((1,), (1,)), ((), ())), preferred_element_type=F32)


def _dot_tn(a, b):
    return lax.dot_general(a, b, (((0,), (0,)), ((), ())), preferred_element_type=F32)


_GELU_C = 0.7978845608028654
_GELU_K = 0.044715


def _gelu(x):
    half = 0.5 * x
    return half + half * jnp.tanh(x * (_GELU_C + (_GELU_C * _GELU_K) * (x * x)))


def _sigmoid(x):
    return 0.5 * jnp.tanh(0.5 * x) + 0.5


def _rms(x, g):
    return (x * lax.rsqrt(jnp.mean(x * x, axis=-1, keepdims=True) + EPS)) * g


def _head_rms(a, g, e_ref):
    sq = a * a
    hi = sq.astype(BF16)
    lo = (sq - hi.astype(F32)).astype(BF16)
    e = e_ref[...]
    w = e.shape[0]
    ms = jnp.concatenate([_dot(hi[:, c:c + w], e) + _dot(lo[:, c:c + w], e) for c in range(0, a.shape[1], w)],
                         axis=1)
    return (a * lax.rsqrt(ms + EPS)) * g


def _rel_rows(rel_ref, h, n_rows):
    wb = jnp.broadcast_to(rel_ref[h:h + 1, :], (n_rows, REL_SPAN))
    return pltpu.roll(wb, 0, axis=1, stride=1, stride_axis=0)


def _const_spec(shape):
    n = len(shape)
    return pl.BlockSpec(shape, lambda *_: (0,) * n, pipeline_mode=pl.Buffered(1))


def _cast_specs(l, stacked, n_steps, step_of):
    in_specs, out_specs, out_shapes = [], [], []
    for a in stacked:
        rows, cols = a.shape[1:]
        blk = next(r for r in range(BF16_ROWS, rows + 1, BF16_ROWS) if rows % r == 0 and rows // r <= n_steps)
        last = rows // blk - 1
        in_specs.append(pl.BlockSpec((None, blk, cols), lambda *ids, last=last: (l, jnp.minimum(step_of(*ids), last), 0)))
        out_specs.append(pl.BlockSpec((blk, cols), lambda *ids, last=last: (jnp.minimum(step_of(*ids), last), 0)))
        out_shapes.append(jax.ShapeDtypeStruct((rows, cols), BF16))
    return in_specs, out_specs, out_shapes


def _layer_spec(l, shape):
    n = len(shape)
    return pl.BlockSpec((None, *shape), lambda *_: (l,) + (0,) * n, pipeline_mode=pl.Buffered(1))


def _ada_kernel(c_ref, w_ref, b_ref, *rest, n_cast):
    cast_in, o_ref, cast_out = rest[:n_cast], rest[n_cast], rest[n_cast + 1:]
    s = jax.nn.silu(c_ref[...]).astype(BF16)
    o_ref[...] = _dot(s, w_ref[...].astype(BF16)) + b_ref[...]
    for src, dst in zip(cast_in, cast_out):
        dst[...] = src[...].astype(BF16)


def _ada_call(c_all, w_ada, b_ada, to_cast):
    depth = w_ada.shape[0]
    rows = c_all.shape[0]
    n_out = w_ada.shape[2]
    n_blk = n_out // ADA_BLOCK_N
    cast_in, cast_out, cast_shapes = _cast_specs(0, to_cast, depth * n_blk, lambda l, n: l * n_blk + n)
    return pl.pallas_call(
        functools.partial(_ada_kernel, n_cast=len(to_cast)),
        grid=(depth, n_blk),
        in_specs=[
            pl.BlockSpec((rows, D_MODEL), lambda l, n: (0, 0)),
            pl.BlockSpec((None, D_MODEL, ADA_BLOCK_N), lambda l, n: (l, 0, n)),
            pl.BlockSpec((None, 1, ADA_BLOCK_N), lambda l, n: (l, 0, n)),
        ] + cast_in,
        out_specs=[pl.BlockSpec((None, rows, ADA_BLOCK_N), lambda l, n: (l, 0, n))] + cast_out,
        out_shape=[jax.ShapeDtypeStruct((depth, rows, n_out), F32)] + cast_shapes,
        compiler_params=pltpu.CompilerParams(
            dimension_semantics=("arbitrary", "arbitrary"), vmem_limit_bytes=VMEM_LIMIT),
        name="ada_mod",
    )(c_all, w_ada, b_ada.reshape(depth, 1, n_out), *to_cast)


def _mixer_prompt_kernel(x_ref, mod_ref, n1g_ref, win_ref, qg_ref, kg_ref, e_ref, vg_ref, rel_ref,
                         wsp_ref, bsp_ref, woa_ref, wob_ref, wo_ref, *rest, n_cast, n_carried):
    cast_in, rest = rest[:n_cast], rest[n_cast + n_carried:]
    xo_ref, kst_ref, vst_ref = rest[:3]
    cast_out = rest[3:3 + n_cast]
    kbuf, vtbuf, oat_scr, ob_scr, bias_scr, st_scr, e_scr = rest[3 + n_cast:]
    b = pl.program_id(0)
    t = pl.program_id(1)
    T = TILE_T
    for src, dst in zip(cast_in, cast_out):
        dst[...] = src[...].astype(BF16)

    @pl.when((b == 0) & (t == 0))
    def _():
        qi = lax.broadcasted_iota(jnp.int32, (Q_BLOCK, KV_WINDOW), 0) // CHUNK
        ki = lax.broadcasted_iota(jnp.int32, (Q_BLOCK, KV_WINDOW), 1) // CHUNK
        band = (ki >= qi) & (ki <= qi + ATTN_REACH // CHUNK)
        for h in range(N_HEADS):
            tab = jnp.where(band, _rel_rows(rel_ref, h, Q_BLOCK)[:, 0:KV_WINDOW], -jnp.inf)
            bias_scr[h // 2, 0:KV_WINDOW, (h % 2) * Q_BLOCK:(h % 2 + 1) * Q_BLOCK] = tab.T
        bias_scr[:, KV_WINDOW:, :] = jnp.full((N_PAIRS, Q_BLOCK, 2 * Q_BLOCK), -jnp.inf, F32)

    @pl.when(t == 0)
    def _():
        kbuf[0:HIST, :] = jnp.zeros((HIST, D_A), BF16)
        vtbuf[:, 0:HIST] = jnp.zeros((D_A, HIST), BF16)

    @pl.when(t > 0)
    def _():
        for r in range(0, HIST, T):
            kbuf[r:r + T, :] = kbuf[r + T:r + 2 * T, :]
            vtbuf[:, r:r + T] = vtbuf[:, r + T:r + 2 * T]

    mod = mod_ref[...]
    sh1, sc1, gt1 = mod[0:1], mod[1:2], mod[2:3]

    x = x_ref[...]
    h = (_rms(x, n1g_ref[...]) * (1.0 + sc1) + sh1).astype(BF16)

    q = _head_rms(_dot(h, win_ref[:, _Q0:_Q0 + D_A]), qg_ref[...], e_ref)
    k = _head_rms(_dot(h, win_ref[:, _K0:_K0 + D_A]), kg_ref[...], e_ref)
    v = _dot(h, win_ref[:, _V0:_V0 + D_A])
    vt = v.T
    kbuf[HIST:HIST + T, :] = k.astype(BF16)
    vtbuf[:, HIST:HIST + T] = vt.astype(BF16)

    kst_ref[...] = k.T
    vst_ref[...] = vt

    qb = (q * (HEAD_DIM ** -0.5)).astype(BF16)

    lane = lax.broadcasted_iota(jnp.int32, (1, LANES), 1)
    low_half = lane < HEAD_DIM

    n_qb = T // Q_BLOCK

    def scores(p, j, slot):
        r0, c0 = j * Q_BLOCK, p * LANES
        qp = qb[r0:r0 + Q_BLOCK, c0:c0 + LANES]
        q2 = jnp.concatenate([jnp.where(low_half, qp, jnp.zeros_like(qp)),
                              jnp.where(low_half, jnp.zeros_like(qp), qp)], axis=0)
        st = _dot_nt(kbuf[r0:r0 + KV_WINDOW, c0:c0 + LANES], q2)
        first_valid = HIST - t * T - r0
        m = None
        for r in range(0, KV_WINDOW, Q_BLOCK):
            src = jnp.where(first_valid > r, KV_WINDOW, r) if r < HIST else r
            blk = st[r:r + Q_BLOCK] + bias_scr[p, pl.ds(pl.multiple_of(src, Q_BLOCK), Q_BLOCK), :]
            st_scr[slot, j, r:r + Q_BLOCK, :] = blk
            bm = jnp.max(blk, axis=0, keepdims=True)
            m = bm if m is None else jnp.maximum(m, bm)
        return m

    def weights(j, slot, m):
        e = jnp.exp(st_scr[slot, j] - m)
        e_scr[slot, j] = e.astype(BF16)
        return jnp.sum(e, axis=0, keepdims=True)

    def outputs(p, j, slot, total):
        r0, c0 = j * Q_BLOCK, p * LANES
        ot = _dot(vtbuf[c0:c0 + LANES, r0:r0 + KV_WINDOW], e_scr[slot, j]) / total
        oat_scr[c0:c0 + HEAD_DIM, r0:r0 + Q_BLOCK] = ot[0:HEAD_DIM, 0:Q_BLOCK]
        oat_scr[c0 + HEAD_DIM:c0 + LANES, r0:r0 + Q_BLOCK] = ot[HEAD_DIM:, Q_BLOCK:]

    side = {}

    def project(name, act, c0):
        side[name] = act(_dot(h, win_ref[:, c0:c0 + MXU_DIM]))

    side_work = ([functools.partial(project, ("b", i), _gelu, _B0 + i * MXU_DIM)
                  for i in range(2 * D_B // MXU_DIM)]
                 + [functools.partial(project, ("g", i), _sigmoid, _G0 + i * MXU_DIM)
                    for i in range(2 * D_MODEL // MXU_DIM)])
    maxima = [scores(0, j, 0) for j in range(n_qb)]
    for p in range(N_PAIRS):
        slot = p % 2
        if p + 1 < N_PAIRS:
            next_maxima = [scores(p + 1, j, 1 - slot) for j in range(n_qb)]
        for work in side_work[len(side_work) * p // N_PAIRS:len(side_work) * (p + 1) // N_PAIRS]:
            work()
        totals = [weights(j, slot, maxima[j]) for j in range(n_qb)]
        for j in range(n_qb):
            outputs(p, j, slot, totals[j])
        maxima = next_maxima
    n_b, n_g = D_B // MXU_DIM, D_MODEL // MXU_DIM
    ub = jnp.concatenate([side["b", i] for i in range(n_b)], axis=1)
    vbn = _rms(jnp.concatenate([side["b", n_b + i] for i in range(n_b)], axis=1), vg_ref[...]).astype(BF16)
    ga = jnp.concatenate([side["g", i] for i in range(n_g)], axis=1)
    gb = jnp.concatenate([side["g", n_g + i] for i in range(n_g)], axis=1)
    row_i = lax.broadcasted_iota(jnp.int32, (MLP_CHUNK, MLP_CHUNK), 0)
    col_i = lax.broadcasted_iota(jnp.int32, (MLP_CHUNK, MLP_CHUNK), 1)
    for g in range(N_GROUPS):
        wc = jnp.where(row_i >= col_i, wsp_ref[g], 0.0).astype(BF16)
        g0 = g * GROUP_DIM
        for c in range(T // MLP_CHUNK):
            r0 = c * MLP_CHUNK
            mix = _dot(wc, vbn[r0:r0 + MLP_CHUNK, g0:g0 + GROUP_DIM]) + bsp_ref[g]
            ob_scr[r0:r0 + MLP_CHUNK, g0:g0 + GROUP_DIM] = ub[r0:r0 + MLP_CHUNK, g0:g0 + GROUP_DIM] * mix

    merged = (ga * _dot_tn(oat_scr[...].astype(BF16), woa_ref[...])
              + gb * _dot(ob_scr[...].astype(BF16), wob_ref[...]))
    xo_ref[...] = x + gt1 * _dot(merged.astype(BF16), wo_ref[...])


def _mixer_prompt_call(l, depth, x, mod_all, mod_row0, p, w, to_cast, carried):
    B, S, _ = x.shape
    T = TILE_T
    n_t = S // T
    tok = pl.BlockSpec((None, T, D_MODEL), lambda b, t: (b, t, 0))
    state = pl.BlockSpec((None, None, D_A, ATTN_REACH), lambda b, t: (l, b, 0, 0))
    cast_in, cast_out, cast_shapes = _cast_specs(l, to_cast, B * n_t, lambda b, t: b * n_t + t)
    in_specs = [
        tok,
        pl.BlockSpec((None, None, 6, D_MODEL), lambda b, t: (l, mod_row0 + b, 0, 0)),
        _layer_spec(l, (1, D_MODEL)),
        _const_spec((D_MODEL, D_IN)),
        _layer_spec(l, (1, D_A)),
        _layer_spec(l, (1, D_A)),
        _const_spec((MXU_DIM, MXU_DIM)),
        _layer_spec(l, (1, D_B)),
        _layer_spec(l, (N_HEADS, REL_SPAN)),
        _layer_spec(l, (N_GROUPS, MLP_CHUNK, MLP_CHUNK)),
        _layer_spec(l, (N_GROUPS, MLP_CHUNK, GROUP_DIM)),
        _const_spec((D_A, D_MODEL)),
        _const_spec((D_B, D_MODEL)),
        _const_spec((D_MODEL, D_MODEL)),
    ]
    in_specs += cast_in
    return pl.pallas_call(
        functools.partial(_mixer_prompt_kernel, n_cast=len(to_cast), n_carried=len(carried)),
        grid=(B, n_t),
        input_output_aliases={len(in_specs) + i: 1 + i for i in range(len(carried))},
        in_specs=in_specs + [pl.BlockSpec(memory_space=pl.ANY)] * len(carried),
        out_specs=[tok, state, state] + cast_out,
        out_shape=[
            jax.ShapeDtypeStruct((B, S, D_MODEL), F32),
            jax.ShapeDtypeStruct((depth, B, D_A, ATTN_REACH), F32),
            jax.ShapeDtypeStruct((depth, B, D_A, ATTN_REACH), F32),
        ] + cast_shapes,
        scratch_shapes=[
            pltpu.VMEM((HIST + T, D_A), BF16),
            pltpu.VMEM((D_A, HIST + T), BF16),
            pltpu.VMEM((D_A, T), F32),
            pltpu.VMEM((T, D_B), F32),
            pltpu.VMEM((N_PAIRS, KV_WINDOW + Q_BLOCK, 2 * Q_BLOCK), F32),
            pltpu.VMEM((2, T // Q_BLOCK, KV_WINDOW, 2 * Q_BLOCK), F32),
            pltpu.VMEM((2, T // Q_BLOCK, KV_WINDOW, 2 * Q_BLOCK), BF16),
        ],
        compiler_params=pltpu.CompilerParams(
            dimension_semantics=("arbitrary", "arbitrary"), vmem_limit_bytes=VMEM_LIMIT),
        name="mixer_prompt",
    )(x, mod_all, p["n1g"], w["win"], p["qg"], p["kg"], p["e_mat"], p["vg"], p["rel"],
      p["wsp"], p["bsp_b"], w["woa"], w["wob"], w["wo"], *to_cast, *carried)


def _conv_gate(g, u, prev1, prev2, row, cw_ref, cb_ref, c0, c1):
    gm1 = jnp.where(row == 0, prev1, pltpu.roll(g, 1, axis=0))
    gm2 = jnp.where(row == 0, prev2, jnp.where(row == 1, prev1, pltpu.roll(g, 2, axis=0)))
    gc = cb_ref[:, c0:c1] + cw_ref[0:1, c0:c1] * gm2
    gc = gc + cw_ref[1:2, c0:c1] * gm1
    gc = gc + cw_ref[2:3, c0:c1] * g
    return _gelu(gc) * u


def _conv_ffn(x, sh2, sc2, gt2, prev1, prev2, row, n2g_ref, wfi_ref, cw_ref, cb_ref, wfo_ref):
    h2 = (_rms(x, n2g_ref[...]) * (1.0 + sc2) + sh2).astype(BF16)

    def up_project(c0, c1):
        return _dot(h2, wfi_ref[:, c0:c1]), _dot(h2, wfi_ref[:, D_FF + c0:D_FF + c1])

    f = jnp.zeros(x.shape, F32)
    gs = []
    ahead = up_project(*FF_CHUNKS[0])
    for i, (c0, c1) in enumerate(FF_CHUNKS):
        g, u = ahead
        if i + 1 < len(FF_CHUNKS):
            ahead = up_project(*FF_CHUNKS[i + 1])
        act = _conv_gate(g, u, prev1[:, c0:c1], prev2[:, c0:c1], row, cw_ref, cb_ref, c0, c1)
        f = f + _dot(act.astype(BF16), wfo_ref[c0:c1, :])
        gs.append(g)
    return x + gt2 * f, gs


def _ffn_prompt_kernel(x_ref, mod_ref, n2g_ref, wfi_ref, cw_ref, cb_ref, wfo_ref, *rest, n_cast):
    cast_in, (xo_ref, cs_ref), cast_out, carry = rest[:n_cast], rest[n_cast:n_cast + 2], rest[n_cast + 2:-1], rest[-1]
    t = pl.program_id(1)
    T = FFN_TILE_T
    mod = mod_ref[...]
    for src, dst in zip(cast_in, cast_out):
        dst[...] = src[...].astype(BF16)

    @pl.when(t == 0)
    def _():
        carry[...] = jnp.zeros(carry.shape, F32)

    row = lax.broadcasted_iota(jnp.int32, (T, 1), 0)
    prev = carry[...]
    xo, gs = _conv_ffn(x_ref[...], mod[3:4], mod[4:5], mod[5:6], prev[SUBLANES - 1:], prev[SUBLANES - 2:SUBLANES - 1],
                       row, n2g_ref, wfi_ref, cw_ref, cb_ref, wfo_ref)
    xo_ref[...] = xo
    tail = jnp.concatenate([g[T - SUBLANES:T, :] for g in gs], axis=1)
    carry[...] = tail
    cs_ref[...] = tail


def _ffn_prompt_call(l, cast_l, x, mod_all, mod_row0, p, w, to_cast):
    B, S, _ = x.shape
    T = FFN_TILE_T
    n_t = S // T
    tok = pl.BlockSpec((None, T, D_MODEL), lambda b, t: (b, t, 0))
    cast_in, cast_out, cast_shapes = _cast_specs(cast_l, to_cast, B * n_t, lambda b, t: b * n_t + t)
    return pl.pallas_call(
        functools.partial(_ffn_prompt_kernel, n_cast=len(to_cast)),
        grid=(B, n_t),
        in_specs=[
            tok,
            pl.BlockSpec((None, None, 6, D_MODEL), lambda b, t: (l, mod_row0 + b, 0, 0)),
            _layer_spec(l, (1, D_MODEL)),
            _const_spec((D_MODEL, 2 * D_FF)),
            _layer_spec(l, (CONV_W, D_FF)),
            _layer_spec(l, (1, D_FF)),
            _const_spec((D_FF, D_MODEL)),
        ] + cast_in,
        out_specs=[tok, pl.BlockSpec((None, SUBLANES, D_FF), lambda b, t: (b, 0, 0))] + cast_out,
        out_shape=[
            jax.ShapeDtypeStruct((B, S, D_MODEL), F32),
            jax.ShapeDtypeStruct((B, SUBLANES, D_FF), F32),
        ] + cast_shapes,
        scratch_shapes=[pltpu.VMEM((SUBLANES, D_FF), F32)],
        compiler_params=pltpu.CompilerParams(
            dimension_semantics=("arbitrary", "arbitrary"), vmem_limit_bytes=VMEM_LIMIT),
        name="ffn_prompt",
    )(x, mod_all, p["n2g"], w["wfi"], p["cw"], p["cb"], w["wfo"], *to_cast)


def _mixer_sample_kernel(x_ref, mod_ref, n1g_ref, win_ref, qg_ref, kg_ref, e_ref, vg_ref, rel_ref,
                         kc_ref, vc_ref, wsp_ref, bsp_ref, woa_ref, wob_ref, wo_ref, *rest,
                         n_carried, n_batch, n_tok, cache_len):
    (xo_ref, ks_ref, vs_ref, sv_ref,
     h_scr, q_scr, kn_scr, vn_scr, oa_scr, ub_scr, vb_scr, ob_scr, gate_scr, bias_scr) = rest[n_carried:]
    step = pl.program_id(0)
    n_rows = n_batch * n_tok

    def per_token(m):
        return jnp.broadcast_to(m, (n_batch, n_tok, m.shape[-1])).reshape(n_rows, m.shape[-1])

    @pl.when(step == 0)
    def _():
        mod = mod_ref[...]
        sh1, sc1 = per_token(mod[:, 0:1, :]), per_token(mod[:, 1:2, :])
        h = (_rms(x_ref[...], n1g_ref[...]) * (1.0 + sc1) + sh1).astype(BF16)
        h_scr[...] = h
        q = _head_rms(_dot(h, win_ref[:, _Q0:_Q0 + D_A]), qg_ref[...], e_ref)
        k = _head_rms(_dot(h, win_ref[:, _K0:_K0 + D_A]), kg_ref[...], e_ref)
        v = _dot(h, win_ref[:, _V0:_V0 + D_A])
        ks_ref[...] = k
        vs_ref[...] = v
        kn_scr[...] = k.astype(BF16)
        vn_scr[...] = v.astype(BF16)
        q_scr[...] = (q * (HEAD_DIM ** -0.5)).astype(BF16)
        for hh in range(N_HEADS):
            bias_scr[hh * n_tok:(hh + 1) * n_tok, :] = _rel_rows(rel_ref, hh, n_tok)[:, 0:cache_len + LANES]

    lane_head = lax.broadcasted_iota(jnp.int32, (n_tok, D_A), 1) // HEAD_DIM
    for i in range(SAMPLE_BG):
        r0 = pl.multiple_of((step * SAMPLE_BG + i) * n_tok, n_tok)
        qb = q_scr[pl.ds(r0, n_tok), :]
        q_bd = jnp.concatenate([jnp.where(lane_head == hh, qb, jnp.zeros_like(qb)) for hh in range(N_HEADS)],
                               axis=0)
        kn = kn_scr[pl.ds(r0, n_tok), :]
        vn = vn_scr[pl.ds(r0, n_tok), :]
        kt = kc_ref[i].reshape(D_A, cache_len).astype(BF16)
        vt = vc_ref[i].reshape(D_A, cache_len).astype(BF16)
        s_c = _dot(q_bd, kt) + bias_scr[:, 0:cache_len]
        s_n = _dot_nt(q_bd, kn) + bias_scr[:, cache_len:cache_len + n_tok]
        m = jnp.maximum(jnp.max(s_c, axis=-1, keepdims=True), jnp.max(s_n, axis=-1, keepdims=True))
        e_c = jnp.exp(s_c - m)
        e_n = jnp.exp(s_n - m)
        l = jnp.sum(e_c, axis=-1, keepdims=True) + jnp.sum(e_n, axis=-1, keepdims=True)
        o_all = (_dot_nt(e_c.astype(BF16), vt) + _dot(e_n.astype(BF16), vn)) / l
        o = jnp.zeros((n_tok, D_A), F32)
        for hh in range(N_HEADS):
            o = jnp.where(lane_head == hh, o_all[hh * n_tok:(hh + 1) * n_tok, :], o)
        oa_scr[pl.ds(r0, n_tok), :] = o

    def spatial_half(half_scr, c0):
        z = _gelu(_dot(h_scr[...], win_ref[:, c0:c0 + D_B]))
        for g in range(N_GROUPS):
            half_scr[g] = z[:, g * GROUP_DIM:(g + 1) * GROUP_DIM]

    def spatial_mix():
        vbn = _rms(jnp.concatenate([vb_scr[g] for g in range(N_GROUPS)], axis=1), vg_ref[...])
        sv_ref[...] = vbn
        for g in range(N_GROUPS):
            vb_scr[g] = vbn[:, g * GROUP_DIM:(g + 1) * GROUP_DIM]
        for g in range(N_GROUPS):
            at_pos = [vb_scr[g, pl.ds(s, n_batch, stride=n_tok), :] for s in range(n_tok)]
            for t in range(n_tok):
                mix = bsp_ref[g, t:t + 1, :]
                for s in range(t + 1):
                    mix = mix + wsp_ref[g, t, s:s + 1, :] * at_pos[s]
                ob_scr[g, pl.ds(t, n_batch, stride=n_tok), :] = ub_scr[g, pl.ds(t, n_batch, stride=n_tok), :] * mix

    def gates(c0, width):
        gate_scr[:, c0:c0 + width] = _sigmoid(_dot(h_scr[...], win_ref[:, _G0 + c0:_G0 + c0 + width]))

    stages = [functools.partial(spatial_half, ub_scr, _B0), functools.partial(spatial_half, vb_scr, _B0 + D_B),
              spatial_mix] + [functools.partial(gates, c0, D_B) for c0 in range(0, 2 * D_MODEL, D_B)]
    for i, stage in enumerate(stages):
        pl.when(step == i + 1)(stage)

    @pl.when(step == pl.num_programs(0) - 1)
    def _():
        ob = jnp.concatenate([ob_scr[g] for g in range(N_GROUPS)], axis=1)
        merged = (gate_scr[:, 0:D_MODEL] * _dot(oa_scr[...].astype(BF16), woa_ref[...])
                  + gate_scr[:, D_MODEL:] * _dot(ob.astype(BF16), wob_ref[...]))
        gt1 = per_token(mod_ref[...][:, 2:3, :])
        xo_ref[...] = x_ref[...] + gt1 * _dot(merged.astype(BF16), wo_ref[...])


def _mixer_sample_call(l, depth, x2, mod_all, kc_all, vc_all, p, w, n_batch, n_tok, carried):
    n_rows = n_batch * n_tok
    cache_len = kc_all.shape[-1]
    cache = pl.BlockSpec((None, SAMPLE_BG, N_HEADS, HEAD_DIM, cache_len), lambda s: (l, s, 0, 0, 0))
    full = lambda shape: pl.BlockSpec(shape, lambda s: (0,) * len(shape))
    state = lambda width: pl.BlockSpec((None, n_rows, width), lambda s: (l, 0, 0))
    in_specs = [
        _const_spec((n_rows, D_MODEL)),
        _layer_spec(l, (n_batch, 6, D_MODEL)),
        _layer_spec(l, (1, D_MODEL)),
        _const_spec((D_MODEL, D_IN)),
        _layer_spec(l, (1, D_A)),
        _layer_spec(l, (1, D_A)),
        _const_spec((MXU_DIM, MXU_DIM)),
        _layer_spec(l, (1, D_B)),
        _layer_spec(l, (N_HEADS, REL_SPAN)),
        cache,
        cache,
        _layer_spec(l, (N_GROUPS, n_tok, n_tok, GROUP_DIM)),
        _layer_spec(l, (N_GROUPS, n_tok, GROUP_DIM)),
        _const_spec((D_A, D_MODEL)),
        _const_spec((D_B, D_MODEL)),
        _const_spec((D_MODEL, D_MODEL)),
    ]
    return pl.pallas_call(
        functools.partial(_mixer_sample_kernel, n_carried=len(carried), n_batch=n_batch, n_tok=n_tok,
                          cache_len=cache_len),
        grid=(n_batch // SAMPLE_BG,),
        input_output_aliases={len(in_specs) + i: 1 + i for i in range(len(carried))},
        in_specs=in_specs + [pl.BlockSpec(memory_space=pl.ANY)] * len(carried),
        out_specs=[full((n_rows, D_MODEL)), state(D_A), state(D_A), state(D_B)],
        out_shape=[
            jax.ShapeDtypeStruct((n_rows, D_MODEL), F32),
            jax.ShapeDtypeStruct((depth, n_rows, D_A), F32),
            jax.ShapeDtypeStruct((depth, n_rows, D_A), F32),
            jax.ShapeDtypeStruct((depth, n_rows, D_B), F32),
        ],
        scratch_shapes=[
            pltpu.VMEM((n_rows, D_MODEL), BF16),
            pltpu.VMEM((n_rows, D_A), BF16),
            pltpu.VMEM((n_rows, D_A), BF16),
            pltpu.VMEM((n_rows, D_A), BF16),
            pltpu.VMEM((n_rows, D_A), F32),
            pltpu.VMEM((N_GROUPS, n_rows, GROUP_DIM), F32),
            pltpu.VMEM((N_GROUPS, n_rows, GROUP_DIM), F32),
            pltpu.VMEM((N_GROUPS, n_rows, GROUP_DIM), F32),
            pltpu.VMEM((n_rows, 2 * D_MODEL), F32),
            pltpu.VMEM((N_HEADS * n_tok, cache_len + LANES), F32),
        ],
        compiler_params=pltpu.CompilerParams(
            dimension_semantics=("arbitrary",), vmem_limit_bytes=VMEM_LIMIT),
        name="mixer_sample",
    )(x2, mod_all, p["n1g"], w["win"], p["qg"], p["kg"], p["e_mat"], p["vg"], p["rel"],
      kc_all, vc_all, p["wsp_s"], p["bsp_s"], w["woa"], w["wob"], w["wo"], *carried)


def _ffn_sample_kernel(x_ref, mod_ref, n2g_ref, wfi_ref, cw_ref, cb_ref, cc_ref, wfo_ref,
                       xo_ref, cs_ref, *, n_batch, n_tok):
    n_rows = n_batch * n_tok

    def per_token(m):
        return jnp.broadcast_to(m, (n_batch, n_tok, m.shape[-1])).reshape(n_rows, m.shape[-1])

    mod = mod_ref[...]
    cc = cc_ref[...]
    row = lax.broadcasted_iota(jnp.int32, (n_rows, 1), 0) % n_tok
    xo, gs = _conv_ffn(x_ref[...], per_token(mod[:, 3:4, :]), per_token(mod[:, 4:5, :]), per_token(mod[:, 5:6, :]),
                       per_token(cc[:, 1:2, :]), per_token(cc[:, 0:1, :]), row,
                       n2g_ref, wfi_ref, cw_ref, cb_ref, wfo_ref)
    xo_ref[...] = xo
    for g, (c0, c1) in zip(gs, FF_CHUNKS):
        cs_ref[:, :, c0:c1] = g.reshape(n_batch, n_tok, c1 - c0)[:, n_tok - SUBLANES:, :]


def _ffn_sample_call(l, x2, mod_all, cc_all, p, w, n_batch, n_tok):
    n_rows = n_batch * n_tok
    return pl.pallas_call(
        functools.partial(_ffn_sample_kernel, n_batch=n_batch, n_tok=n_tok),
        grid=(1,),
        in_specs=[
            _const_spec((n_rows, D_MODEL)),
            _layer_spec(l, (n_batch, 6, D_MODEL)),
            _layer_spec(l, (1, D_MODEL)),
            _const_spec((D_MODEL, 2 * D_FF)),
            _layer_spec(l, (CONV_W, D_FF)),
            _layer_spec(l, (1, D_FF)),
            _layer_spec(l, (n_batch, CONV_W - 1, D_FF)),
            _const_spec((D_FF, D_MODEL)),
        ],
        out_specs=[
            pl.BlockSpec((n_rows, D_MODEL), lambda c: (0, 0)),
            pl.BlockSpec((n_batch, SUBLANES, D_FF), lambda c: (0, 0, 0)),
        ],
        out_shape=[
            jax.ShapeDtypeStruct((n_rows, D_MODEL), F32),
            jax.ShapeDtypeStruct((n_batch, SUBLANES, D_FF), F32),
        ],
        compiler_params=pltpu.CompilerParams(
            dimension_semantics=("arbitrary",), vmem_limit_bytes=VMEM_LIMIT),
        name="ffn_sample",
    )(x2, mod_all, p["n2g"], w["wfi"], p["cw"], p["cb"], cc_all, w["wfo"])


def _rel_vector(rel_bias):
    far = rel_bias[..., 2 * REL_CLIP:]
    lead = ATTN_REACH - REL_CLIP
    return jnp.concatenate([
        jnp.broadcast_to(far, (*rel_bias.shape[:-1], lead)),
        rel_bias[..., ::-1],
        jnp.broadcast_to(far, (*rel_bias.shape[:-1], REL_SPAN - lead - (2 * REL_CLIP + 1))),
    ], axis=-1)


def kernel(x_prompt, x_sample, cache_attn_k, cache_attn_v, cache_ffn_conv, c_prompt, c_sample, norm1_g, norm2_g, w_ada, b_ada, w_in, q_norm_g, k_norm_g, rel_bias, v_norm_g, w_spatial, b_spatial, w_out_a, w_out_b, w_out, w_ffn_in, ffn_conv_w, ffn_conv_b, w_ffn_out):
    depth = w_in.shape[0]
    B, S, _ = x_prompt.shape
    NB, NT, _ = x_sample.shape
    cache_len = cache_attn_k.shape[2]
    assert S % FFN_TILE_T == 0 and S % TILE_T == 0 and TILE_T % Q_BLOCK == 0 and TILE_T == ATTN_REACH and S >= ATTN_REACH
    assert NB % SAMPLE_BG == 0 and NB // SAMPLE_BG >= SAMPLE_MIN_STEPS and NT % SUBLANES == 0 and NT <= MLP_CHUNK and N_HEADS * NT == LANES
    assert cache_len == ATTN_REACH and rel_bias.shape[-1] == 2 * REL_CLIP + 1

    rows = NB + B
    rows_pad = -(-rows // SUBLANES) * SUBLANES
    c_all = jnp.concatenate([c_sample, c_prompt, jnp.zeros((rows_pad - rows, D_MODEL), F32)], axis=0)
    mixer_names, ffn_names = ("win", "woa", "wob", "wo"), ("wfi", "wfo")
    mixer_f32, ffn_f32 = (w_in, w_out_a, w_out_b, w_out), (w_ffn_in, w_ffn_out)
    mod_all, *first = _ada_call(c_all, w_ada, b_ada, mixer_f32)
    mod_all = mod_all.reshape(depth, rows_pad, 6, D_MODEL)
    w = dict(zip(mixer_names, first))

    head_of = jnp.arange(MXU_DIM) // HEAD_DIM
    tril = jnp.tril(jnp.ones((NT, NT), F32))
    ws_small = w_spatial[:, :, :NT, :NT] * tril
    params = {
        "n1g": norm1_g[:, None, :], "n2g": norm2_g[:, None, :],
        "qg": jnp.tile(q_norm_g, (1, N_HEADS))[:, None, :], "kg": jnp.tile(k_norm_g, (1, N_HEADS))[:, None, :],
        "vg": v_norm_g[:, None, :],
        "e_mat": jnp.where(head_of[:, None] == head_of[None, :], 1.0 / HEAD_DIM, 0.0).astype(BF16),
        "rel": _rel_vector(rel_bias),
        "wsp": w_spatial,
        "bsp_b": jnp.broadcast_to(b_spatial[:, :, :, None], (depth, N_GROUPS, MLP_CHUNK, GROUP_DIM)),
        "wsp_s": jnp.broadcast_to(ws_small[..., None], (depth, N_GROUPS, NT, NT, GROUP_DIM)),
        "bsp_s": jnp.broadcast_to(b_spatial[:, :, :NT, None], (depth, N_GROUPS, NT, GROUP_DIM)),
        "cw": ffn_conv_w, "cb": ffn_conv_b[:, None, :],
    }
    kc_all = cache_attn_k.transpose(0, 1, 3, 4, 2)
    vc_all = cache_attn_v.transpose(0, 1, 3, 4, 2)

    xp = x_prompt
    xs = x_sample.reshape(NB * NT, D_MODEL)
    conv_p, conv_s = [], []
    prompt_state, sample_state = (), ()
    for l in range(depth):
        xp, *outs = _mixer_prompt_call(l, depth, xp, mod_all, NB, params, w, ffn_f32, tuple(prompt_state))
        prompt_state, w_ffn = outs[:2], dict(zip(ffn_names, outs[2:]))
        nxt = mixer_f32 if l + 1 < depth else ()
        xp, cp, *outs = _ffn_prompt_call(l, l + 1, xp, mod_all, NB, params, w_ffn, nxt)
        xs, *sample_state = _mixer_sample_call(l, depth, xs, mod_all, kc_all, vc_all, params, w, NB, NT,
                                               tuple(sample_state))
        xs, cs = _ffn_sample_call(l, xs, mod_all, cache_ffn_conv, params, w_ffn, NB, NT)
        w = dict(zip(mixer_names, outs))
        conv_p.append(cp[:, SUBLANES - (CONV_W - 1):])
        conv_s.append(cs[:, SUBLANES - (CONV_W - 1):])
    new_k_prompt, new_v_prompt = (
        s.reshape(depth, B, N_HEADS, HEAD_DIM, ATTN_REACH).transpose(0, 1, 4, 2, 3) for s in prompt_state)
    ks, vs, sv = sample_state
    return (xp, xs.reshape(NB, NT, D_MODEL), new_k_prompt, new_v_prompt, jnp.stack(conv_p),
            ks.reshape(depth, NB, NT, N_HEADS, HEAD_DIM), vs.reshape(depth, NB, NT, N_HEADS, HEAD_DIM),
            sv.reshape(depth, NB, NT, N_GROUPS, GROUP_DIM), jnp.stack(conv_s))
```

```python
import functools

import jax
import jax.numpy as jnp
from jax import lax
from jax.experimental import pallas as pl
from jax.experimental.pallas import tpu as pltpu

F32 = jnp.float32
BF16 = jnp.bfloat16

D_MODEL = 1024
CHUNK = 64
ATTN_REACH = 512
N_HEADS = 8
HEAD_DIM = 64
D_A = N_HEADS * HEAD_DIM
REL_CLIP = 128
MLP_CHUNK = 128
N_GROUPS = 4
D_B = 512
GROUP_DIM = D_B // N_GROUPS
D_FF = 2816
CONV_W = 3
EPS = 1e-6

LANES = 128
SUBLANES = 8
BF16_ROWS = 16
MXU_DIM = 256
N_PAIRS = N_HEADS // 2
Q_BLOCK = 2 * CHUNK
KV_WINDOW = ATTN_REACH + Q_BLOCK
REL_SPAN = KV_WINDOW + Q_BLOCK
TILE_T = 512
FFN_TILE_T = 512
HIST = ATTN_REACH
FF_CHUNKS = ((0, 1280), (1280, 2560), (2560, D_FF))
ADA_BLOCK_N = 1536
SAMPLE_BG = 4
SAMPLE_MIN_STEPS = 8
VMEM_LIMIT = 56 * 1024 * 1024
_Q0, _K0, _V0, _B0, _G0 = 0, D_A, 2 * D_A, 3 * D_A, 3 * D_A + 2 * D_B
D_IN = _G0 + 2 * D_MODEL


def _dot(a, b):
    return jnp.dot(a, b, preferred_element_type=F32)


def _dot_nt(a, b):
    return lax.dot_general(a, b, (((1,), (1,)), ((), ())), preferred_element_type=F32)


def _dot_tn(a, b):
    return lax.dot_general(a, b, (((0,), (0,)), ((), ())), preferred_element_type=F32)


_GELU_C = 0.7978845608028654
_GELU_K = 0.044715


def _gelu(x):
    half = 0.5 * x
    return half + half * jnp.tanh(x * (_GELU_C + (_GELU_C * _GELU_K) * (x * x)))


def _sigmoid(x):
    return 0.5 * jnp.tanh(0.5 * x) + 0.5


def _rms(x, g):
    return (x * lax.rsqrt(jnp.mean(x * x, axis=-1, keepdims=True) + EPS)) * g


def _head_rms(a, g, e_ref):
    sq = a * a
    hi = sq.astype(BF16)
    lo = (sq - hi.astype(F32)).astype(BF16)
    e = e_ref[...]
    w = e.shape[0]
    ms = jnp.concatenate([_dot(hi[:, c:c + w], e) + _dot(lo[:, c:c + w], e) for c in range(0, a.shape[1], w)],
                         axis=1)
    return (a * lax.rsqrt(ms + EPS)) * g


def _rel_rows(rel_ref, h, n_rows):
    wb = jnp.broadcast_to(rel_ref[h:h + 1, :], (n_rows, REL_SPAN))
    return pltpu.roll(wb, 0, axis=1, stride=1, stride_axis=0)


def _const_spec(shape):
    n = len(shape)
    return pl.BlockSpec(shape, lambda *_: (0,) * n, pipeline_mode=pl.Buffered(1))


def _cast_specs(l, stacked, n_steps, step_of):
    in_specs, out_specs, out_shapes = [], [], []
    for a in stacked:
        rows, cols = a.shape[1:]
        blk = next(r for r in range(BF16_ROWS, rows + 1, BF16_ROWS) if rows % r == 0 and rows // r <= n_steps)
        last = rows // blk - 1
        in_specs.append(pl.BlockSpec((None, blk, cols), lambda *ids, last=last: (l, jnp.minimum(step_of(*ids), last), 0)))
        out_specs.append(pl.BlockSpec((blk, cols), lambda *ids, last=last: (jnp.minimum(step_of(*ids), last), 0)))
        out_shapes.append(jax.ShapeDtypeStruct((rows, cols), BF16))
    return in_specs, out_specs, out_shapes


def _layer_spec(l, shape):
    n = len(shape)
    return pl.BlockSpec((None, *shape), lambda *_: (l,) + (0,) * n, pipeline_mode=pl.Buffered(1))


def _ada_kernel(c_ref, w_ref, b_ref, *rest, n_cast):
    cast_in, o_ref, cast_out = rest[:n_cast], rest[n_cast], rest[n_cast + 1:]
    s = jax.nn.silu(c_ref[...]).astype(BF16)
    o_ref[...] = _dot(s, w_ref[...].astype(BF16)) + b_ref[...]
    for src, dst in zip(cast_in, cast_out):
        dst[...] = src[...].astype(BF16)


def _ada_call(c_all, w_ada, b_ada, to_cast):
    depth = w_ada.shape[0]
    rows = c_all.shape[0]
    n_out = w_ada.shape[2]
    n_blk = n_out // ADA_BLOCK_N
    cast_in, cast_out, cast_shapes = _cast_specs(0, to_cast, depth * n_blk, lambda l, n: l * n_blk + n)
    return pl.pallas_call(
        functools.partial(_ada_kernel, n_cast=len(to_cast)),
        grid=(depth, n_blk),
        in_specs=[
            pl.BlockSpec((rows, D_MODEL), lambda l, n: (0, 0)),
            pl.BlockSpec((None, D_MODEL, ADA_BLOCK_N), lambda l, n: (l, 0, n)),
            pl.BlockSpec((None, 1, ADA_BLOCK_N), lambda l, n: (l, 0, n)),
        ] + cast_in,
        out_specs=[pl.BlockSpec((None, rows, ADA_BLOCK_N), lambda l, n: (l, 0, n))] + cast_out,
        out_shape=[jax.ShapeDtypeStruct((depth, rows, n_out), F32)] + cast_shapes,
        compiler_params=pltpu.CompilerParams(
            dimension_semantics=("arbitrary", "arbitrary"), vmem_limit_bytes=VMEM_LIMIT),
        name="ada_mod",
    )(c_all, w_ada, b_ada.reshape(depth, 1, n_out), *to_cast)


def _mixer_prompt_kernel(x_ref, mod_ref, n1g_ref, win_ref, qg_ref, kg_ref, e_ref, vg_ref, rel_ref,
                         wsp_ref, bsp_ref, woa_ref, wob_ref, wo_ref, *rest, n_cast, n_carried):
    cast_in, rest = rest[:n_cast], rest[n_cast + n_carried:]
    xo_ref, kst_ref, vst_ref = rest[:3]
    cast_out = rest[3:3 + n_cast]
    kbuf, vtbuf, oat_scr, ob_scr, bias_scr, st_scr, e_scr = rest[3 + n_cast:]
    b = pl.program_id(0)
    t = pl.program_id(1)
    T = TILE_T
    for src, dst in zip(cast_in, cast_out):
        dst[...] = src[...].astype(BF16)

    @pl.when((b == 0) & (t == 0))
    def _():
        qi = lax.broadcasted_iota(jnp.int32, (Q_BLOCK, KV_WINDOW), 0) // CHUNK
        ki = lax.broadcasted_iota(jnp.int32, (Q_BLOCK, KV_WINDOW), 1) // CHUNK
        band = (ki >= qi) & (ki <= qi + ATTN_REACH // CHUNK)
        for h in range(N_HEADS):
            tab = jnp.where(band, _rel_rows(rel_ref, h, Q_BLOCK)[:, 0:KV_WINDOW], -jnp.inf)
            bias_scr[h // 2, 0:KV_WINDOW, (h % 2) * Q_BLOCK:(h % 2 + 1) * Q_BLOCK] = tab.T
        bias_scr[:, KV_WINDOW:, :] = jnp.full((N_PAIRS, Q_BLOCK, 2 * Q_BLOCK), -jnp.inf, F32)

    @pl.when(t == 0)
    def _():
        kbuf[0:HIST, :] = jnp.zeros((HIST, D_A), BF16)
        vtbuf[:, 0:HIST] = jnp.zeros((D_A, HIST), BF16)

    @pl.when(t > 0)
    def _():
        for r in range(0, HIST, T):
            kbuf[r:r + T, :] = kbuf[r + T:r + 2 * T, :]
            vtbuf[:, r:r + T] = vtbuf[:, r + T:r + 2 * T]

    mod = mod_ref[...]
    sh1, sc1, gt1 = mod[0:1], mod[1:2], mod[2:3]

    x = x_ref[...]
    h = (_rms(x, n1g_ref[...]) * (1.0 + sc1) + sh1).astype(BF16)

    q = _head_rms(_dot(h, win_ref[:, _Q0:_Q0 + D_A]), qg_ref[...], e_ref)
    k = _head_rms(_dot(h, win_ref[:, _K0:_K0 + D_A]), kg_ref[...], e_ref)
    v = _dot(h, win_ref[:, _V0:_V0 + D_A])
    vt = v.T
    kbuf[HIST:HIST + T, :] = k.astype(BF16)
    vtbuf[:, HIST:HIST + T] = vt.astype(BF16)

    kst_ref[...] = k.T
    vst_ref[...] = vt

    qb = (q * (HEAD_DIM ** -0.5)).astype(BF16)

    lane = lax.broadcasted_iota(jnp.int32, (1, LANES), 1)
    low_half = lane < HEAD_DIM

    n_qb = T // Q_BLOCK

    def scores(p, j, slot):
        r0, c0 = j * Q_BLOCK, p * LANES
        qp = qb[r0:r0 + Q_BLOCK, c0:c0 + LANES]
        q2 = jnp.concatenate([jnp.where(low_half, qp, jnp.zeros_like(qp)),
                              jnp.where(low_half, jnp.zeros_like(qp), qp)], axis=0)
        st = _dot_nt(kbuf[r0:r0 + KV_WINDOW, c0:c0 + LANES], q2)
        first_valid = HIST - t * T - r0
        m = None
        for r in range(0, KV_WINDOW, Q_BLOCK):
            src = jnp.where(first_valid > r, KV_WINDOW, r) if r < HIST else r
            blk = st[r:r + Q_BLOCK] + bias_scr[p, pl.ds(pl.multiple_of(src, Q_BLOCK), Q_BLOCK), :]
            st_scr[slot, j, r:r + Q_BLOCK, :] = blk
            bm = jnp.max(blk, axis=0, keepdims=True)
            m = bm if m is None else jnp.maximum(m, bm)
        return m

    def weights(j, slot, m):
        e = jnp.exp(st_scr[slot, j] - m)
        e_scr[slot, j] = e.astype(BF16)
        return jnp.sum(e, axis=0, keepdims=True)

    def outputs(p, j, slot, total):
        r0, c0 = j * Q_BLOCK, p * LANES
        ot = _dot(vtbuf[c0:c0 + LANES, r0:r0 + KV_WINDOW], e_scr[slot, j]) / total
        oat_scr[c0:c0 + HEAD_DIM, r0:r0 + Q_BLOCK] = ot[0:HEAD_DIM, 0:Q_BLOCK]
        oat_scr[c0 + HEAD_DIM:c0 + LANES, r0:r0 + Q_BLOCK] = ot[HEAD_DIM:, Q_BLOCK:]

    side = {}

    def project(name, act, c0):
        side[name] = act(_dot(h, win_ref[:, c0:c0 + MXU_DIM]))

    side_work = ([functools.partial(project, ("b", i), _gelu, _B0 + i * MXU_DIM)
                  for i in range(2 * D_B // MXU_DIM)]
                 + [functools.partial(project, ("g", i), _sigmoid, _G0 + i * MXU_DIM)
                    for i in range(2 * D_MODEL // MXU_DIM)])
    maxima = [scores(0, j, 0) for j in range(n_qb)]
    for p in range(N_PAIRS):
        slot = p % 2
        if p + 1 < N_PAIRS:
            next_maxima = [scores(p + 1, j, 1 - slot) for j in range(n_qb)]
        for work in side_work[len(side_work) * p // N_PAIRS:len(side_work) * (p + 1) // N_PAIRS]:
            work()
        totals = [weights(j, slot, maxima[j]) for j in range(n_qb)]
        for j in range(n_qb):
            outputs(p, j, slot, totals[j])
        maxima = next_maxima
    n_b, n_g = D_B // MXU_DIM, D_MODEL // MXU_DIM
    ub = jnp.concatenate([side["b", i] for i in range(n_b)], axis=1)
    vbn = _rms(jnp.concatenate([side["b", n_b + i] for i in range(n_b)], axis=1), vg_ref[...]).astype(BF16)
    ga = jnp.concatenate([side["g", i] for i in range(n_g)], axis=1)
    gb = jnp.concatenate([side["g", n_g + i] for i in range(n_g)], axis=1)
    row_i = lax.broadcasted_iota(jnp.int32, (MLP_CHUNK, MLP_CHUNK), 0)
    col_i = lax.broadcasted_iota(jnp.int32, (MLP_CHUNK, MLP_CHUNK), 1)
    for g in range(N_GROUPS):
        wc = jnp.where(row_i >= col_i, wsp_ref[g], 0.0).astype(BF16)
        g0 = g * GROUP_DIM
        for c in range(T // MLP_CHUNK):
            r0 = c * MLP_CHUNK
            mix = _dot(wc, vbn[r0:r0 + MLP_CHUNK, g0:g0 + GROUP_DIM]) + bsp_ref[g]
            ob_scr[r0:r0 + MLP_CHUNK, g0:g0 + GROUP_DIM] = ub[r0:r0 + MLP_CHUNK, g0:g0 + GROUP_DIM] * mix

    merged = (ga * _dot_tn(oat_scr[...].astype(BF16), woa_ref[...])
              + gb * _dot(ob_scr[...].astype(BF16), wob_ref[...]))
    xo_ref[...] = x + gt1 * _dot(merged.astype(BF16), wo_ref[...])


def _mixer_prompt_call(l, depth, x, mod_all, mod_row0, p, w, to_cast, carried):
    B, S, _ = x.shape
    T = TILE_T
    n_t = S // T
    tok = pl.BlockSpec((None, T, D_MODEL), lambda b, t: (b, t, 0))
    state = pl.BlockSpec((None, None, D_A, ATTN_REACH), lambda b, t: (l, b, 0, 0))
    cast_in, cast_out, cast_shapes = _cast_specs(l, to_cast, B * n_t, lambda b, t: b * n_t + t)
    in_specs = [
        tok,
        pl.BlockSpec((None, None, 6, D_MODEL), lambda b, t: (l, mod_row0 + b, 0, 0)),
        _layer_spec(l, (1, D_MODEL)),
        _const_spec((D_MODEL, D_IN)),
        _layer_spec(l, (1, D_A)),
        _layer_spec(l, (1, D_A)),
        _const_spec((MXU_DIM, MXU_DIM)),
        _layer_spec(l, (1, D_B)),
        _layer_spec(l, (N_HEADS, REL_SPAN)),
        _layer_spec(l, (N_GROUPS, MLP_CHUNK, MLP_CHUNK)),
        _layer_spec(l, (N_GROUPS, MLP_CHUNK, GROUP_DIM)),
        _const_spec((D_A, D_MODEL)),
        _const_spec((D_B, D_MODEL)),
        _const_spec((D_MODEL, D_MODEL)),
    ]
    in_specs += cast_in
    return pl.pallas_call(
        functools.partial(_mixer_prompt_kernel, n_cast=len(to_cast), n_carried=len(carried)),
        grid=(B, n_t),
        input_output_aliases={len(in_specs) + i: 1 + i for i in range(len(carried))},
        in_specs=in_specs + [pl.BlockSpec(memory_space=pl.ANY)] * len(carried),
        out_specs=[tok, state, state] + cast_out,
        out_shape=[
            jax.ShapeDtypeStruct((B, S, D_MODEL), F32),
            jax.ShapeDtypeStruct((depth, B, D_A, ATTN_REACH), F32),
            jax.ShapeDtypeStruct((depth, B, D_A, ATTN_REACH), F32),
        ] + cast_shapes,
        scratch_shapes=[
            pltpu.VMEM((HIST + T, D_A), BF16),
            pltpu.VMEM((D_A, HIST + T), BF16),
            pltpu.VMEM((D_A, T), F32),
            pltpu.VMEM((T, D_B), F32),
            pltpu.VMEM((N_PAIRS, KV_WINDOW + Q_BLOCK, 2 * Q_BLOCK), F32),
            pltpu.VMEM((2, T // Q_BLOCK, KV_WINDOW, 2 * Q_BLOCK), F32),
            pltpu.VMEM((2, T // Q_BLOCK, KV_WINDOW, 2 * Q_BLOCK), BF16),
        ],
        compiler_params=pltpu.CompilerParams(
            dimension_semantics=("arbitrary", "arbitrary"), vmem_limit_bytes=VMEM_LIMIT),
        name="mixer_prompt",
    )(x, mod_all, p["n1g"], w["win"], p["qg"], p["kg"], p["e_mat"], p["vg"], p["rel"],
      p["wsp"], p["bsp_b"], w["woa"], w["wob"], w["wo"], *to_cast, *carried)


def _conv_gate(g, u, prev1, prev2, row, cw_ref, cb_ref, c0, c1):
    gm1 = jnp.where(row == 0, prev1, pltpu.roll(g, 1, axis=0))
    gm2 = jnp.where(row == 0, prev2, jnp.where(row == 1, prev1, pltpu.roll(g, 2, axis=0)))
    gc = cb_ref[:, c0:c1] + cw_ref[0:1, c0:c1] * gm2
    gc = gc + cw_ref[1:2, c0:c1] * gm1
    gc = gc + cw_ref[2:3, c0:c1] * g
    return _gelu(gc) * u


def _conv_ffn(x, sh2, sc2, gt2, prev1, prev2, row, n2g_ref, wfi_ref, cw_ref, cb_ref, wfo_ref):
    if sc2.shape[0] == 1:
        h2 = (_rms(x, n2g_ref[...] * (1.0 + sc2)) + sh2).astype(BF16)
    else:
        h2 = (_rms(x, n2g_ref[...]) * (1.0 + sc2) + sh2).astype(BF16)

    def up_project(c0, c1):
        return _dot(h2, wfi_ref[:, c0:c1]), _dot(h2, wfi_ref[:, D_FF + c0:D_FF + c1])

    f = jnp.zeros(x.shape, F32)
    gs = []
    ahead = up_project(*FF_CHUNKS[0])
    for i, (c0, c1) in enumerate(FF_CHUNKS):
        g, u = ahead
        if i + 1 < len(FF_CHUNKS):
            ahead = up_project(*FF_CHUNKS[i + 1])
        act = _conv_gate(g, u, prev1[:, c0:c1], prev2[:, c0:c1], row, cw_ref, cb_ref, c0, c1)
        f = f + _dot(act.astype(BF16), wfo_ref[c0:c1, :])
        gs.append(g)
    return x + gt2 * f, gs


def _ffn_prompt_kernel(x_ref, mod_ref, n2g_ref, wfi_ref, cw_ref, cb_ref, wfo_ref, *rest, n_cast):
    cast_in, (xo_ref, cs_ref), cast_out, carry = rest[:n_cast], rest[n_cast:n_cast + 2], rest[n_cast + 2:-1], rest[-1]
    t = pl.program_id(1)
    T = FFN_TILE_T
    mod = mod_ref[...]
    for src, dst in zip(cast_in, cast_out):
        dst[...] = src[...].astype(BF16)

    @pl.when(t == 0)
    def _():
        carry[...] = jnp.zeros(carry.shape, F32)

    row = lax.broadcasted_iota(jnp.int32, (T, 1), 0)
    prev = carry[...]
    xo, gs = _conv_ffn(x_ref[...], mod[3:4], mod[4:5], mod[5:6], prev[SUBLANES - 1:], prev[SUBLANES - 2:SUBLANES - 1],
                       row, n2g_ref, wfi_ref, cw_ref, cb_ref, wfo_ref)
    xo_ref[...] = xo
    tail = jnp.concatenate([g[T - SUBLANES:T, :] for g in gs], axis=1)
    carry[...] = tail
    cs_ref[...] = tail


def _ffn_prompt_call(l, cast_l, x, mod_all, mod_row0, p, w, to_cast):
    B, S, _ = x.shape
    T = FFN_TILE_T
    n_t = S // T
    tok = pl.BlockSpec((None, T, D_MODEL), lambda b, t: (b, t, 0))
    cast_in, cast_out, cast_shapes = _cast_specs(cast_l, to_cast, B * n_t, lambda b, t: b * n_t + t)
    return pl.pallas_call(
        functools.partial(_ffn_prompt_kernel, n_cast=len(to_cast)),
        grid=(B, n_t),
        in_specs=[
            tok,
            pl.BlockSpec((None, None, 6, D_MODEL), lambda b, t: (l, mod_row0 + b, 0, 0)),
            _layer_spec(l, (1, D_MODEL)),
            _const_spec((D_MODEL, 2 * D_FF)),
            _layer_spec(l, (CONV_W, D_FF)),
            _layer_spec(l, (1, D_FF)),
            _const_spec((D_FF, D_MODEL)),
        ] + cast_in,
        out_specs=[tok, pl.BlockSpec((None, SUBLANES, D_FF), lambda b, t: (b, 0, 0))] + cast_out,
        out_shape=[
            jax.ShapeDtypeStruct((B, S, D_MODEL), F32),
            jax.ShapeDtypeStruct((B, SUBLANES, D_FF), F32),
        ] + cast_shapes,
        scratch_shapes=[pltpu.VMEM((SUBLANES, D_FF), F32)],
        compiler_params=pltpu.CompilerParams(
            dimension_semantics=("arbitrary", "arbitrary"), vmem_limit_bytes=VMEM_LIMIT),
        name="ffn_prompt",
    )(x, mod_all, p["n2g"], w["wfi"], p["cw"], p["cb"], w["wfo"], *to_cast)


def _mixer_sample_kernel(x_ref, mod_ref, n1g_ref, win_ref, qg_ref, kg_ref, e_ref, vg_ref, rel_ref,
                         kc_ref, vc_ref, wsp_ref, bsp_ref, woa_ref, wob_ref, wo_ref, *rest,
                         n_carried, n_batch, n_tok, cache_len):
    (xo_ref, ks_ref, vs_ref, sv_ref,
     h_scr, q_scr, kn_scr, vn_scr, oa_scr, ub_scr, vb_scr, ob_scr, gate_scr, bias_scr) = rest[n_carried:]
    step = pl.program_id(0)
    n_rows = n_batch * n_tok

    def per_token(m):
        return jnp.broadcast_to(m, (n_batch, n_tok, m.shape[-1])).reshape(n_rows, m.shape[-1])

    @pl.when(step == 0)
    def _():
        mod = mod_ref[...]
        sh1, sc1 = per_token(mod[:, 0:1, :]), per_token(mod[:, 1:2, :])
        h = (_rms(x_ref[...], n1g_ref[...]) * (1.0 + sc1) + sh1).astype(BF16)
        h_scr[...] = h
        q = _head_rms(_dot(h, win_ref[:, _Q0:_Q0 + D_A]), qg_ref[...], e_ref)
        k = _head_rms(_dot(h, win_ref[:, _K0:_K0 + D_A]), kg_ref[...], e_ref)
        v = _dot(h, win_ref[:, _V0:_V0 + D_A])
        ks_ref[...] = k
        vs_ref[...] = v
        kn_scr[...] = k.astype(BF16)
        vn_scr[...] = v.astype(BF16)
        q_scr[...] = (q * (HEAD_DIM ** -0.5)).astype(BF16)
        for hh in range(N_HEADS):
            bias_scr[hh * n_tok:(hh + 1) * n_tok, :] = _rel_rows(rel_ref, hh, n_tok)[:, 0:cache_len + LANES]

    lane_head = lax.broadcasted_iota(jnp.int32, (n_tok, D_A), 1) // HEAD_DIM
    for i in range(SAMPLE_BG):
        r0 = pl.multiple_of((step * SAMPLE_BG + i) * n_tok, n_tok)
        qb = q_scr[pl.ds(r0, n_tok), :]
        q_bd = jnp.concatenate([jnp.where(lane_head == hh, qb, jnp.zeros_like(qb)) for hh in range(N_HEADS)],
                               axis=0)
        kn = kn_scr[pl.ds(r0, n_tok), :]
        vn = vn_scr[pl.ds(r0, n_tok), :]
        kt = kc_ref[i].reshape(D_A, cache_len).astype(BF16)
        vt = vc_ref[i].reshape(D_A, cache_len).astype(BF16)
        s_c = _dot(q_bd, kt) + bias_scr[:, 0:cache_len]
        s_n = _dot_nt(q_bd, kn) + bias_scr[:, cache_len:cache_len + n_tok]
        m = jnp.maximum(jnp.max(s_c, axis=-1, keepdims=True), jnp.max(s_n, axis=-1, keepdims=True))
        e_c = jnp.exp(s_c - m)
        e_n = jnp.exp(s_n - m)
        l = jnp.sum(e_c, axis=-1, keepdims=True) + jnp.sum(e_n, axis=-1, keepdims=True)
        o_all = (_dot_nt(e_c.astype(BF16), vt) + _dot(e_n.astype(BF16), vn)) / l
        o = jnp.zeros((n_tok, D_A), F32)
        for hh in range(N_HEADS):
            o = jnp.where(lane_head == hh, o_all[hh * n_tok:(hh + 1) * n_tok, :], o)
        oa_scr[pl.ds(r0, n_tok), :] = o

    def spatial_half(half_scr, c0):
        z = _gelu(_dot(h_scr[...], win_ref[:, c0:c0 + D_B]))
        for g in range(N_GROUPS):
            half_scr[g] = z[:, g * GROUP_DIM:(g + 1) * GROUP_DIM]

    def spatial_mix():
        vbn = _rms(jnp.concatenate([vb_scr[g] for g in range(N_GROUPS)], axis=1), vg_ref[...])
        sv_ref[...] = vbn
        for g in range(N_GROUPS):
            vb_scr[g] = vbn[:, g * GROUP_DIM:(g + 1) * GROUP_DIM]
        for g in range(N_GROUPS):
            at_pos = [vb_scr[g, pl.ds(s, n_batch, stride=n_tok), :] for s in range(n_tok)]
            for t in range(n_tok):
                mix = bsp_ref[g, t:t + 1, :]
                for s in range(t + 1):
                    mix = mix + wsp_ref[g, t, s:s + 1, :] * at_pos[s]
                ob_scr[g, pl.ds(t, n_batch, stride=n_tok), :] = ub_scr[g, pl.ds(t, n_batch, stride=n_tok), :] * mix

    def gates(c0, width):
        gate_scr[:, c0:c0 + width] = _sigmoid(_dot(h_scr[...], win_ref[:, _G0 + c0:_G0 + c0 + width]))

    stages = [functools.partial(spatial_half, ub_scr, _B0), functools.partial(spatial_half, vb_scr, _B0 + D_B),
              spatial_mix] + [functools.partial(gates, c0, D_B) for c0 in range(0, 2 * D_MODEL, D_B)]
    for i, stage in enumerate(stages):
        pl.when(step == i + 1)(stage)

    @pl.when(step == pl.num_programs(0) - 1)
    def _():
        ob = jnp.concatenate([ob_scr[g] for g in range(N_GROUPS)], axis=1)
        merged = (gate_scr[:, 0:D_MODEL] * _dot(oa_scr[...].astype(BF16), woa_ref[...])
                  + gate_scr[:, D_MODEL:] * _dot(ob.astype(BF16), wob_ref[...]))
        gt1 = per_token(mod_ref[...][:, 2:3, :])
        xo_ref[...] = x_ref[...] + gt1 * _dot(merged.astype(BF16), wo_ref[...])


def _mixer_sample_call(l, depth, x2, mod_all, kc_all, vc_all, p, w, n_batch, n_tok, carried):
    n_rows = n_batch * n_tok
    cache_len = kc_all.shape[-1]
    cache = pl.BlockSpec((None, SAMPLE_BG, N_HEADS, HEAD_DIM, cache_len), lambda s: (l, s, 0, 0, 0))
    full = lambda shape: pl.BlockSpec(shape, lambda s: (0,) * len(shape))
    state = lambda width: pl.BlockSpec((None, n_rows, width), lambda s: (l, 0, 0))
    in_specs = [
        _const_spec((n_rows, D_MODEL)),
        _layer_spec(l, (n_batch, 6, D_MODEL)),
        _layer_spec(l, (1, D_MODEL)),
        _const_spec((D_MODEL, D_IN)),
        _layer_spec(l, (1, D_A)),
        _layer_spec(l, (1, D_A)),
        _const_spec((MXU_DIM, MXU_DIM)),
        _layer_spec(l, (1, D_B)),
        _layer_spec(l, (N_HEADS, REL_SPAN)),
        cache,
        cache,
        _layer_spec(l, (N_GROUPS, n_tok, n_tok, GROUP_DIM)),
        _layer_spec(l, (N_GROUPS, n_tok, GROUP_DIM)),
        _const_spec((D_A, D_MODEL)),
        _const_spec((D_B, D_MODEL)),
        _const_spec((D_MODEL, D_MODEL)),
    ]
    return pl.pallas_call(
        functools.partial(_mixer_sample_kernel, n_carried=len(carried), n_batch=n_batch, n_tok=n_tok,
                          cache_len=cache_len),
        grid=(n_batch // SAMPLE_BG,),
        input_output_aliases={len(in_specs) + i: 1 + i for i in range(len(carried))},
        in_specs=in_specs + [pl.BlockSpec(memory_space=pl.ANY)] * len(carried),
        out_specs=[full((n_rows, D_MODEL)), state(D_A), state(D_A), state(D_B)],
        out_shape=[
            jax.ShapeDtypeStruct((n_rows, D_MODEL), F32),
            jax.ShapeDtypeStruct((depth, n_rows, D_A), F32),
            jax.ShapeDtypeStruct((depth, n_rows, D_A), F32),
            jax.ShapeDtypeStruct((depth, n_rows, D_B), F32),
        ],
        scratch_shapes=[
            pltpu.VMEM((n_rows, D_MODEL), BF16),
            pltpu.VMEM((n_rows, D_A), BF16),
            pltpu.VMEM((n_rows, D_A), BF16),
            pltpu.VMEM((n_rows, D_A), BF16),
            pltpu.VMEM((n_rows, D_A), F32),
            pltpu.VMEM((N_GROUPS, n_rows, GROUP_DIM), F32),
            pltpu.VMEM((N_GROUPS, n_rows, GROUP_DIM), F32),
            pltpu.VMEM((N_GROUPS, n_rows, GROUP_DIM), F32),
            pltpu.VMEM((n_rows, 2 * D_MODEL), F32),
            pltpu.VMEM((N_HEADS * n_tok, cache_len + LANES), F32),
        ],
        compiler_params=pltpu.CompilerParams(
            dimension_semantics=("arbitrary",), vmem_limit_bytes=VMEM_LIMIT),
        name="mixer_sample",
    )(x2, mod_all, p["n1g"], w["win"], p["qg"], p["kg"], p["e_mat"], p["vg"], p["rel"],
      kc_all, vc_all, p["wsp_s"], p["bsp_s"], w["woa"], w["wob"], w["wo"], *carried)


def _ffn_sample_kernel(x_ref, mod_ref, n2g_ref, wfi_ref, cw_ref, cb_ref, cc_ref, wfo_ref,
                       xo_ref, cs_ref, *, n_batch, n_tok):
    n_rows = n_batch * n_tok

    def per_token(m):
        return jnp.broadcast_to(m, (n_batch, n_tok, m.shape[-1])).reshape(n_rows, m.shape[-1])

    mod = mod_ref[...]
    cc = cc_ref[...]
    row = lax.broadcasted_iota(jnp.int32, (n_rows, 1), 0) % n_tok
    xo, gs = _conv_ffn(x_ref[...], per_token(mod[:, 3:4, :]), per_token(mod[:, 4:5, :]), per_token(mod[:, 5:6, :]),
                       per_token(cc[:, 1:2, :]), per_token(cc[:, 0:1, :]), row,
                       n2g_ref, wfi_ref, cw_ref, cb_ref, wfo_ref)
    xo_ref[...] = xo
    for g, (c0, c1) in zip(gs, FF_CHUNKS):
        cs_ref[:, :, c0:c1] = g.reshape(n_batch, n_tok, c1 - c0)[:, n_tok - SUBLANES:, :]


def _ffn_sample_call(l, x2, mod_all, cc_all, p, w, n_batch, n_tok):
    n_rows = n_batch * n_tok
    return pl.pallas_call(
        functools.partial(_ffn_sample_kernel, n_batch=n_batch, n_tok=n_tok),
        grid=(1,),
        in_specs=[
            _const_spec((n_rows, D_MODEL)),
            _layer_spec(l, (n_batch, 6, D_MODEL)),
            _layer_spec(l, (1, D_MODEL)),
            _const_spec((D_MODEL, 2 * D_FF)),
            _layer_spec(l, (CONV_W, D_FF)),
            _layer_spec(l, (1, D_FF)),
            _layer_spec(l, (n_batch, CONV_W - 1, D_FF)),
            _const_spec((D_FF, D_MODEL)),
        ],
        out_specs=[
            pl.BlockSpec((n_rows, D_MODEL), lambda c: (0, 0)),
            pl.BlockSpec((n_batch, SUBLANES, D_FF), lambda c: (0, 0, 0)),
        ],
        out_shape=[
            jax.ShapeDtypeStruct((n_rows, D_MODEL), F32),
            jax.ShapeDtypeStruct((n_batch, SUBLANES, D_FF), F32),
        ],
        compiler_params=pltpu.CompilerParams(
            dimension_semantics=("arbitrary",), vmem_limit_bytes=VMEM_LIMIT),
        name="ffn_sample",
    )(x2, mod_all, p["n2g"], w["wfi"], p["cw"], p["cb"], cc_all, w["wfo"])


def _rel_vector(rel_bias):
    far = rel_bias[..., 2 * REL_CLIP:]
    lead = ATTN_REACH - REL_CLIP
    return jnp.concatenate([
        jnp.broadcast_to(far, (*rel_bias.shape[:-1], lead)),
        rel_bias[..., ::-1],
        jnp.broadcast_to(far, (*rel_bias.shape[:-1], REL_SPAN - lead - (2 * REL_CLIP + 1))),
    ], axis=-1)


def kernel(x_prompt, x_sample, cache_attn_k, cache_attn_v, cache_ffn_conv, c_prompt, c_sample, norm1_g, norm2_g, w_ada, b_ada, w_in, q_norm_g, k_norm_g, rel_bias, v_norm_g, w_spatial, b_spatial, w_out_a, w_out_b, w_out, w_ffn_in, ffn_conv_w, ffn_conv_b, w_ffn_out):
    depth = w_in.shape[0]
    B, S, _ = x_prompt.shape
    NB, NT, _ = x_sample.shape
    cache_len = cache_attn_k.shape[2]
    assert S % FFN_TILE_T == 0 and S % TILE_T == 0 and TILE_T % Q_BLOCK == 0 and TILE_T == ATTN_REACH and S >= ATTN_REACH
    assert NB % SAMPLE_BG == 0 and NB // SAMPLE_BG >= SAMPLE_MIN_STEPS and NT % SUBLANES == 0 and NT <= MLP_CHUNK and N_HEADS * NT == LANES
    assert cache_len == ATTN_REACH and rel_bias.shape[-1] == 2 * REL_CLIP + 1

    rows = NB + B
    rows_pad = -(-rows // SUBLANES) * SUBLANES
    c_all = jnp.concatenate([c_sample, c_prompt, jnp.zeros((rows_pad - rows, D_MODEL), F32)], axis=0)
    mixer_names, ffn_names = ("win", "woa", "wob", "wo"), ("wfi", "wfo")
    mixer_f32, ffn_f32 = (w_in, w_out_a, w_out_b, w_out), (w_ffn_in, w_ffn_out)
    mod_all, *first = _ada_call(c_all, w_ada, b_ada, mixer_f32)
    mod_all = mod_all.reshape(depth, rows_pad, 6, D_MODEL)
    w = dict(zip(mixer_names, first))

    head_of = jnp.arange(MXU_DIM) // HEAD_DIM
    tril = jnp.tril(jnp.ones((NT, NT), F32))
    ws_small = w_spatial[:, :, :NT, :NT] * tril
    params = {
        "n1g": norm1_g[:, None, :], "n2g": norm2_g[:, None, :],
        "qg": jnp.tile(q_norm_g, (1, N_HEADS))[:, None, :], "kg": jnp.tile(k_norm_g, (1, N_HEADS))[:, None, :],
        "vg": v_norm_g[:, None, :],
        "e_mat": jnp.where(head_of[:, None] == head_of[None, :], 1.0 / HEAD_DIM, 0.0).astype(BF16),
        "rel": _rel_vector(rel_bias),
        "wsp": w_spatial,
        "bsp_b": jnp.broadcast_to(b_spatial[:, :, :, None], (depth, N_GROUPS, MLP_CHUNK, GROUP_DIM)),
        "wsp_s": jnp.broadcast_to(ws_small[..., None], (depth, N_GROUPS, NT, NT, GROUP_DIM)),
        "bsp_s": jnp.broadcast_to(b_spatial[:, :, :NT, None], (depth, N_GROUPS, NT, GROUP_DIM)),
        "cw": ffn_conv_w, "cb": ffn_conv_b[:, None, :],
    }
    kc_all = cache_attn_k.transpose(0, 1, 3, 4, 2)
    vc_all = cache_attn_v.transpose(0, 1, 3, 4, 2)

    xp = x_prompt
    xs = x_sample.reshape(NB * NT, D_MODEL)
    conv_p, conv_s = [], []
    prompt_state = tuple(jnp.zeros((depth, B, D_A, ATTN_REACH), F32) for _ in range(2))
    sample_state = tuple(jnp.zeros((depth, NB * NT, width), F32) for width in (D_A, D_A, D_B))
    for l in range(depth):
        xp, *outs = _mixer_prompt_call(l, depth, xp, mod_all, NB, params, w, ffn_f32, tuple(prompt_state))
        prompt_state, w_ffn = outs[:2], dict(zip(ffn_names, outs[2:]))
        nxt = mixer_f32 if l + 1 < depth else ()
        xp, cp, *outs = _ffn_prompt_call(l, l + 1, xp, mod_all, NB, params, w_ffn, nxt)
        xs, *sample_state = _mixer_sample_call(l, depth, xs, mod_all, kc_all, vc_all, params, w, NB, NT,
                                               tuple(sample_state))
        xs, cs = _ffn_sample_call(l, xs, mod_all, cache_ffn_conv, params, w_ffn, NB, NT)
        w = dict(zip(mixer_names, outs))
        conv_p.append(cp[:, SUBLANES - (CONV_W - 1):])
        conv_s.append(cs[:, SUBLANES - (CONV_W - 1):])
    new_k_prompt, new_v_prompt = (
        s.reshape(depth, B, N_HEADS, HEAD_DIM, ATTN_REACH).transpose(0, 1, 4, 2, 3) for s in prompt_state)
    ks, vs, sv = sample_state
    return (xp, xs.reshape(NB, NT, D_MODEL), new_k_prompt, new_v_prompt, jnp.stack(conv_p),
            ks.reshape(depth, NB, NT, N_HEADS, HEAD_DIM), vs.reshape(depth, NB, NT, N_HEADS, HEAD_DIM),
            sv.reshape(depth, NB, NT, N_GROUPS, GROUP_DIM), jnp.stack(conv_s))
```

```python
import functools

import jax
import jax.numpy as jnp
from jax import lax
from jax.experimental import pallas as pl
from jax.experimental.pallas import tpu as pltpu

F32 = jnp.float32
BF16 = jnp.bfloat16

D_MODEL = 1024
CHUNK = 64
ATTN_REACH = 512
N_HEADS = 8
HEAD_DIM = 64
D_A = N_HEADS * HEAD_DIM
REL_CLIP = 128
MLP_CHUNK = 128
N_GROUPS = 4
D_B = 512
GROUP_DIM = D_B // N_GROUPS
D_FF = 2816
CONV_W = 3
EPS = 1e-6

LANES = 128
SUBLANES = 8
BF16_ROWS = 16
MXU_DIM = 256
N_PAIRS = N_HEADS // 2
Q_BLOCK = 2 * CHUNK
KV_WINDOW = ATTN_REACH + Q_BLOCK
REL_SPAN = KV_WINDOW + Q_BLOCK
TILE_T = 512
FFN_TILE_T = 512
HIST = ATTN_REACH
FF_CHUNKS = ((0, 1280), (1280, 2560), (2560, D_FF))
ADA_BLOCK_N = 1536
SAMPLE_BG = 2
CACHE_AHEAD = 3
CACHE_SLOTS = CACHE_AHEAD + 1
SAMPLE_MIN_STEPS = 8
VMEM_LIMIT = 56 * 1024 * 1024
_Q0, _K0, _V0, _B0, _G0 = 0, D_A, 2 * D_A, 3 * D_A, 3 * D_A + 2 * D_B
D_IN = _G0 + 2 * D_MODEL


def _dot(a, b):
    return jnp.dot(a, b, preferred_element_type=F32)


def _dot_nt(a, b):
    return lax.dot_general(a, b, (((1,), (1,)), ((), ())), preferred_element_type=F32)


def _dot_tn(a, b):
    return lax.dot_general(a, b, (((0,), (0,)), ((), ())), preferred_element_type=F32)


_GELU_C = 0.7978845608028654
_GELU_K = 0.044715


def _gelu(x):
    half = 0.5 * x
    return half + half * jnp.tanh(x * (_GELU_C + (_GELU_C * _GELU_K) * (x * x)))


def _sigmoid(x):
    return 0.5 * jnp.tanh(0.5 * x) + 0.5


def _rms(x, g):
    return (x * lax.rsqrt(jnp.mean(x * x, axis=-1, keepdims=True) + EPS)) * g


def _head_rms(a, g, e_ref):
    sq = a * a
    hi = sq.astype(BF16)
    lo = (sq - hi.astype(F32)).astype(BF16)
    e = e_ref[...]
    w = e.shape[0]
    ms = jnp.concatenate([_dot(hi[:, c:c + w], e) + _dot(lo[:, c:c + w], e) for c in range(0, a.shape[1], w)],
                         axis=1)
    return (a * lax.rsqrt(ms + EPS)) * g


def _rel_rows(rel_ref, h, n_rows):
    wb = jnp.broadcast_to(rel_ref[h:h + 1, :], (n_rows, REL_SPAN))
    return pltpu.roll(wb, 0, axis=1, stride=1, stride_axis=0)


def _const_spec(shape):
    n = len(shape)
    return pl.BlockSpec(shape, lambda *_: (0,) * n, pipeline_mode=pl.Buffered(1))


def _cast_specs(l, stacked, n_steps, step_of):
    in_specs, out_specs, out_shapes = [], [], []
    for a in stacked:
        rows, cols = a.shape[1:]
        blk = next(r for r in range(BF16_ROWS, rows + 1, BF16_ROWS) if rows % r == 0 and rows // r <= n_steps)
        last = rows // blk - 1
        in_specs.append(pl.BlockSpec((None, blk, cols), lambda *ids, last=last: (l, jnp.minimum(step_of(*ids), last), 0)))
        out_specs.append(pl.BlockSpec((blk, cols), lambda *ids, last=last: (jnp.minimum(step_of(*ids), last), 0)))
        out_shapes.append(jax.ShapeDtypeStruct((rows, cols), BF16))
    return in_specs, out_specs, out_shapes


def _layer_spec(l, shape):
    n = len(shape)
    return pl.BlockSpec((None, *shape), lambda *_: (l,) + (0,) * n, pipeline_mode=pl.Buffered(1))


def _ada_kernel(c_ref, w_ref, b_ref, *rest, n_cast):
    cast_in, o_ref, cast_out = rest[:n_cast], rest[n_cast], rest[n_cast + 1:]
    s = jax.nn.silu(c_ref[...]).astype(BF16)
    o_ref[...] = _dot(s, w_ref[...].astype(BF16)) + b_ref[...]
    for src, dst in zip(cast_in, cast_out):
        dst[...] = src[...].astype(BF16)


def _ada_call(c_all, w_ada, b_ada, to_cast):
    depth = w_ada.shape[0]
    rows = c_all.shape[0]
    n_out = w_ada.shape[2]
    n_blk = n_out // ADA_BLOCK_N
    cast_in, cast_out, cast_shapes = _cast_specs(0, to_cast, depth * n_blk, lambda l, n: l * n_blk + n)
    return pl.pallas_call(
        functools.partial(_ada_kernel, n_cast=len(to_cast)),
        grid=(depth, n_blk),
        in_specs=[
            pl.BlockSpec((rows, D_MODEL), lambda l, n: (0, 0)),
            pl.BlockSpec((None, D_MODEL, ADA_BLOCK_N), lambda l, n: (l, 0, n)),
            pl.BlockSpec((None, 1, ADA_BLOCK_N), lambda l, n: (l, 0, n)),
        ] + cast_in,
        out_specs=[pl.BlockSpec((None, rows, ADA_BLOCK_N), lambda l, n: (l, 0, n))] + cast_out,
        out_shape=[jax.ShapeDtypeStruct((depth, rows, n_out), F32)] + cast_shapes,
        compiler_params=pltpu.CompilerParams(
            dimension_semantics=("arbitrary", "arbitrary"), vmem_limit_bytes=VMEM_LIMIT),
        name="ada_mod",
    )(c_all, w_ada, b_ada.reshape(depth, 1, n_out), *to_cast)


def _mixer_prompt_kernel(x_ref, mod_ref, n1g_ref, win_ref, qg_ref, kg_ref, e_ref, vg_ref, rel_ref,
                         wsp_ref, bsp_ref, woa_ref, wob_ref, wo_ref, *rest, n_cast, n_carried):
    cast_in, rest = rest[:n_cast], rest[n_cast + n_carried:]
    xo_ref, kst_ref, vst_ref = rest[:3]
    cast_out = rest[3:3 + n_cast]
    kbuf, vtbuf, oat_scr, ob_scr, bias_scr, st_scr, e_scr = rest[3 + n_cast:]
    b = pl.program_id(0)
    t = pl.program_id(1)
    T = TILE_T
    for src, dst in zip(cast_in, cast_out):
        dst[...] = src[...].astype(BF16)

    @pl.when((b == 0) & (t == 0))
    def _():
        qi = lax.broadcasted_iota(jnp.int32, (Q_BLOCK, KV_WINDOW), 0) // CHUNK
        ki = lax.broadcasted_iota(jnp.int32, (Q_BLOCK, KV_WINDOW), 1) // CHUNK
        band = (ki >= qi) & (ki <= qi + ATTN_REACH // CHUNK)
        for h in range(N_HEADS):
            tab = jnp.where(band, _rel_rows(rel_ref, h, Q_BLOCK)[:, 0:KV_WINDOW], -jnp.inf)
            bias_scr[h // 2, 0:KV_WINDOW, (h % 2) * Q_BLOCK:(h % 2 + 1) * Q_BLOCK] = tab.T
        bias_scr[:, KV_WINDOW:, :] = jnp.full((N_PAIRS, Q_BLOCK, 2 * Q_BLOCK), -jnp.inf, F32)

    @pl.when(t == 0)
    def _():
        kbuf[0:HIST, :] = jnp.zeros((HIST, D_A), BF16)
        vtbuf[:, 0:HIST] = jnp.zeros((D_A, HIST), BF16)

    @pl.when(t > 0)
    def _():
        for r in range(0, HIST, T):
            kbuf[r:r + T, :] = kbuf[r + T:r + 2 * T, :]
            vtbuf[:, r:r + T] = vtbuf[:, r + T:r + 2 * T]

    mod = mod_ref[...]
    sh1, sc1, gt1 = mod[0:1], mod[1:2], mod[2:3]

    x = x_ref[...]
    h = (_rms(x, n1g_ref[...]) * (1.0 + sc1) + sh1).astype(BF16)

    q = _head_rms(_dot(h, win_ref[:, _Q0:_Q0 + D_A]), qg_ref[...], e_ref)
    k = _head_rms(_dot(h, win_ref[:, _K0:_K0 + D_A]), kg_ref[...], e_ref)
    v = _dot(h, win_ref[:, _V0:_V0 + D_A])
    vt = v.T
    kbuf[HIST:HIST + T, :] = k.astype(BF16)
    vtbuf[:, HIST:HIST + T] = vt.astype(BF16)

    kst_ref[...] = k.T
    vst_ref[...] = vt

    qb = (q * (HEAD_DIM ** -0.5)).astype(BF16)

    lane = lax.broadcasted_iota(jnp.int32, (1, LANES), 1)
    low_half = lane < HEAD_DIM

    n_qb = T // Q_BLOCK

    def scores(p, j, slot):
        r0, c0 = j * Q_BLOCK, p * LANES
        qp = qb[r0:r0 + Q_BLOCK, c0:c0 + LANES]
        q2 = jnp.concatenate([jnp.where(low_half, qp, jnp.zeros_like(qp)),
                              jnp.where(low_half, jnp.zeros_like(qp), qp)], axis=0)
        st = _dot_nt(kbuf[r0:r0 + KV_WINDOW, c0:c0 + LANES], q2)
        first_valid = HIST - t * T - r0
        m = None
        for r in range(0, KV_WINDOW, Q_BLOCK):
            src = jnp.where(first_valid > r, KV_WINDOW, r) if r < HIST else r
            blk = st[r:r + Q_BLOCK] + bias_scr[p, pl.ds(pl.multiple_of(src, Q_BLOCK), Q_BLOCK), :]
            st_scr[slot, j, r:r + Q_BLOCK, :] = blk
            bm = jnp.max(blk, axis=0, keepdims=True)
            m = bm if m is None else jnp.maximum(m, bm)
        return m

    def weights(j, slot, m):
        e = jnp.exp(st_scr[slot, j] - m)
        e_scr[slot, j] = e.astype(BF16)
        return jnp.sum(e, axis=0, keepdims=True)

    def outputs(p, j, slot, total):
        r0, c0 = j * Q_BLOCK, p * LANES
        ot = _dot(vtbuf[c0:c0 + LANES, r0:r0 + KV_WINDOW], e_scr[slot, j]) / total
        oat_scr[c0:c0 + HEAD_DIM, r0:r0 + Q_BLOCK] = ot[0:HEAD_DIM, 0:Q_BLOCK]
        oat_scr[c0 + HEAD_DIM:c0 + LANES, r0:r0 + Q_BLOCK] = ot[HEAD_DIM:, Q_BLOCK:]

    side = {}

    def project(name, act, c0):
        side[name] = act(_dot(h, win_ref[:, c0:c0 + MXU_DIM]))

    side_work = ([functools.partial(project, ("b", i), _gelu, _B0 + i * MXU_DIM)
                  for i in range(2 * D_B // MXU_DIM)]
                 + [functools.partial(project, ("g", i), _sigmoid, _G0 + i * MXU_DIM)
                    for i in range(2 * D_MODEL // MXU_DIM)])
    maxima = [scores(0, j, 0) for j in range(n_qb)]
    for p in range(N_PAIRS):
        slot = p % 2
        if p + 1 < N_PAIRS:
            next_maxima = [scores(p + 1, j, 1 - slot) for j in range(n_qb)]
        for work in side_work[len(side_work) * p // N_PAIRS:len(side_work) * (p + 1) // N_PAIRS]:
            work()
        totals = [weights(j, slot, maxima[j]) for j in range(n_qb)]
        for j in range(n_qb):
            outputs(p, j, slot, totals[j])
        maxima = next_maxima
    n_b, n_g = D_B // MXU_DIM, D_MODEL // MXU_DIM
    ub = jnp.concatenate([side["b", i] for i in range(n_b)], axis=1)
    vbn = _rms(jnp.concatenate([side["b", n_b + i] for i in range(n_b)], axis=1), vg_ref[...]).astype(BF16)
    ga = jnp.concatenate([side["g", i] for i in range(n_g)], axis=1)
    gb = jnp.concatenate([side["g", n_g + i] for i in range(n_g)], axis=1)
    row_i = lax.broadcasted_iota(jnp.int32, (MLP_CHUNK, MLP_CHUNK), 0)
    col_i = lax.broadcasted_iota(jnp.int32, (MLP_CHUNK, MLP_CHUNK), 1)
    for g in range(N_GROUPS):
        wc = jnp.where(row_i >= col_i, wsp_ref[g], 0.0).astype(BF16)
        g0 = g * GROUP_DIM
        for c in range(T // MLP_CHUNK):
            r0 = c * MLP_CHUNK
            mix = _dot(wc, vbn[r0:r0 + MLP_CHUNK, g0:g0 + GROUP_DIM]) + bsp_ref[g]
            ob_scr[r0:r0 + MLP_CHUNK, g0:g0 + GROUP_DIM] = ub[r0:r0 + MLP_CHUNK, g0:g0 + GROUP_DIM] * mix

    merged = (ga * _dot_tn(oat_scr[...].astype(BF16), woa_ref[...])
              + gb * _dot(ob_scr[...].astype(BF16), wob_ref[...]))
    xo_ref[...] = x + gt1 * _dot(merged.astype(BF16), wo_ref[...])


def _mixer_prompt_call(l, depth, x, mod_all, mod_row0, p, w, to_cast, carried):
    B, S, _ = x.shape
    T = TILE_T
    n_t = S // T
    tok = pl.BlockSpec((None, T, D_MODEL), lambda b, t: (b, t, 0))
    state = pl.BlockSpec((None, None, D_A, ATTN_REACH), lambda b, t: (l, b, 0, 0))
    cast_in, cast_out, cast_shapes = _cast_specs(l, to_cast, B * n_t, lambda b, t: b * n_t + t)
    in_specs = [
        tok,
        pl.BlockSpec((None, None, 6, D_MODEL), lambda b, t: (l, mod_row0 + b, 0, 0)),
        _layer_spec(l, (1, D_MODEL)),
        _const_spec((D_MODEL, D_IN)),
        _layer_spec(l, (1, D_A)),
        _layer_spec(l, (1, D_A)),
        _const_spec((MXU_DIM, MXU_DIM)),
        _layer_spec(l, (1, D_B)),
        _layer_spec(l, (N_HEADS, REL_SPAN)),
        _layer_spec(l, (N_GROUPS, MLP_CHUNK, MLP_CHUNK)),
        _layer_spec(l, (N_GROUPS, MLP_CHUNK, GROUP_DIM)),
        _const_spec((D_A, D_MODEL)),
        _const_spec((D_B, D_MODEL)),
        _const_spec((D_MODEL, D_MODEL)),
    ]
    in_specs += cast_in
    return pl.pallas_call(
        functools.partial(_mixer_prompt_kernel, n_cast=len(to_cast), n_carried=len(carried)),
        grid=(B, n_t),
        input_output_aliases={len(in_specs) + i: 1 + i for i in range(len(carried))},
        in_specs=in_specs + [pl.BlockSpec(memory_space=pl.ANY)] * len(carried),
        out_specs=[tok, state, state] + cast_out,
        out_shape=[
            jax.ShapeDtypeStruct((B, S, D_MODEL), F32),
            jax.ShapeDtypeStruct((depth, B, D_A, ATTN_REACH), F32),
            jax.ShapeDtypeStruct((depth, B, D_A, ATTN_REACH), F32),
        ] + cast_shapes,
        scratch_shapes=[
            pltpu.VMEM((HIST + T, D_A), BF16),
            pltpu.VMEM((D_A, HIST + T), BF16),
            pltpu.VMEM((D_A, T), F32),
            pltpu.VMEM((T, D_B), F32),
            pltpu.VMEM((N_PAIRS, KV_WINDOW + Q_BLOCK, 2 * Q_BLOCK), F32),
            pltpu.VMEM((2, T // Q_BLOCK, KV_WINDOW, 2 * Q_BLOCK), F32),
            pltpu.VMEM((2, T // Q_BLOCK, KV_WINDOW, 2 * Q_BLOCK), BF16),
        ],
        compiler_params=pltpu.CompilerParams(
            dimension_semantics=("arbitrary", "arbitrary"), vmem_limit_bytes=VMEM_LIMIT),
        name="mixer_prompt",
    )(x, mod_all, p["n1g"], w["win"], p["qg"], p["kg"], p["e_mat"], p["vg"], p["rel"],
      p["wsp"], p["bsp_b"], w["woa"], w["wob"], w["wo"], *to_cast, *carried)


def _conv_gate(g, u, prev1, prev2, row, cw_ref, cb_ref, c0, c1):
    gm1 = jnp.where(row == 0, prev1, pltpu.roll(g, 1, axis=0))
    gm2 = jnp.where(row == 0, prev2, jnp.where(row == 1, prev1, pltpu.roll(g, 2, axis=0)))
    gc = cb_ref[:, c0:c1] + cw_ref[0:1, c0:c1] * gm2
    gc = gc + cw_ref[1:2, c0:c1] * gm1
    gc = gc + cw_ref[2:3, c0:c1] * g
    return _gelu(gc) * u


def _conv_ffn(x, sh2, sc2, gt2, prev1, prev2, row, n2g_ref, wfi_ref, cw_ref, cb_ref, wfo_ref):
    if sc2.shape[0] == 1:
        h2 = (_rms(x, n2g_ref[...] * (1.0 + sc2)) + sh2).astype(BF16)
    else:
        h2 = (_rms(x, n2g_ref[...]) * (1.0 + sc2) + sh2).astype(BF16)

    def up_project(c0, c1):
        return _dot(h2, wfi_ref[:, c0:c1]), _dot(h2, wfi_ref[:, D_FF + c0:D_FF + c1])

    f = jnp.zeros(x.shape, F32)
    gs = []
    ahead = up_project(*FF_CHUNKS[0])
    for i, (c0, c1) in enumerate(FF_CHUNKS):
        g, u = ahead
        if i + 1 < len(FF_CHUNKS):
            ahead = up_project(*FF_CHUNKS[i + 1])
        act = _conv_gate(g, u, prev1[:, c0:c1], prev2[:, c0:c1], row, cw_ref, cb_ref, c0, c1)
        f = f + _dot(act.astype(BF16), wfo_ref[c0:c1, :])
        gs.append(g)
    return x + gt2 * f, gs


def _ffn_prompt_kernel(x_ref, mod_ref, n2g_ref, wfi_ref, cw_ref, cb_ref, wfo_ref, *rest, n_cast):
    cast_in, (xo_ref, cs_ref), cast_out, carry = rest[:n_cast], rest[n_cast:n_cast + 2], rest[n_cast + 2:-1], rest[-1]
    t = pl.program_id(1)
    T = FFN_TILE_T
    mod = mod_ref[...]
    for src, dst in zip(cast_in, cast_out):
        dst[...] = src[...].astype(BF16)

    @pl.when(t == 0)
    def _():
        carry[...] = jnp.zeros(carry.shape, F32)

    row = lax.broadcasted_iota(jnp.int32, (T, 1), 0)
    prev = carry[...]
    xo, gs = _conv_ffn(x_ref[...], mod[3:4], mod[4:5], mod[5:6], prev[SUBLANES - 1:], prev[SUBLANES - 2:SUBLANES - 1],
                       row, n2g_ref, wfi_ref, cw_ref, cb_ref, wfo_ref)
    xo_ref[...] = xo
    tail = jnp.concatenate([g[T - SUBLANES:T, :] for g in gs], axis=1)
    carry[...] = tail
    cs_ref[...] = tail


def _ffn_prompt_call(l, cast_l, x, mod_all, mod_row0, p, w, to_cast):
    B, S, _ = x.shape
    T = FFN_TILE_T
    n_t = S // T
    tok = pl.BlockSpec((None, T, D_MODEL), lambda b, t: (b, t, 0))
    cast_in, cast_out, cast_shapes = _cast_specs(cast_l, to_cast, B * n_t, lambda b, t: b * n_t + t)
    return pl.pallas_call(
        functools.partial(_ffn_prompt_kernel, n_cast=len(to_cast)),
        grid=(B, n_t),
        in_specs=[
            tok,
            pl.BlockSpec((None, None, 6, D_MODEL), lambda b, t: (l, mod_row0 + b, 0, 0)),
            _layer_spec(l, (1, D_MODEL)),
            _const_spec((D_MODEL, 2 * D_FF)),
            _layer_spec(l, (CONV_W, D_FF)),
            _layer_spec(l, (1, D_FF)),
            _const_spec((D_FF, D_MODEL)),
        ] + cast_in,
        out_specs=[tok, pl.BlockSpec((None, SUBLANES, D_FF), lambda b, t: (b, 0, 0))] + cast_out,
        out_shape=[
            jax.ShapeDtypeStruct((B, S, D_MODEL), F32),
            jax.ShapeDtypeStruct((B, SUBLANES, D_FF), F32),
        ] + cast_shapes,
        scratch_shapes=[pltpu.VMEM((SUBLANES, D_FF), F32)],
        compiler_params=pltpu.CompilerParams(
            dimension_semantics=("arbitrary", "arbitrary"), vmem_limit_bytes=VMEM_LIMIT),
        name="ffn_prompt",
    )(x, mod_all, p["n2g"], w["wfi"], p["cw"], p["cb"], w["wfo"], *to_cast)


def _mixer_sample_kernel(x_ref, mod_ref, n1g_ref, win_ref, qg_ref, kg_ref, e_ref, vg_ref, rel_ref,
                         kc_ref, vc_ref, wsp_ref, bsp_ref, woa_ref, wob_ref, wo_ref, *rest,
                         layer, n_carried, n_batch, n_tok, cache_len):
    (xo_ref, ks_ref, vs_ref, sv_ref,
     h_scr, q_scr, kn_scr, vn_scr, oa_scr, ub_scr, vb_scr, ob_scr, gate_scr, bias_scr,
     kc_buf, vc_buf, cache_sem) = rest[n_carried:]
    step = pl.program_id(0)
    n_steps = pl.num_programs(0)
    n_rows = n_batch * n_tok

    def cache_copies(blk):
        slot = blk % CACHE_SLOTS
        rows = pl.ds(blk * SAMPLE_BG, SAMPLE_BG)
        return (pltpu.make_async_copy(kc_ref.at[layer, rows], kc_buf.at[slot], cache_sem.at[0, slot]),
                pltpu.make_async_copy(vc_ref.at[layer, rows], vc_buf.at[slot], cache_sem.at[1, slot]))

    @pl.when(step == 0)
    def _():
        for blk in range(CACHE_AHEAD):
            for cp in cache_copies(blk):
                cp.start()

    @pl.when(step + CACHE_AHEAD < n_steps)
    def _():
        for cp in cache_copies(step + CACHE_AHEAD):
            cp.start()

    def per_token(m):
        return jnp.broadcast_to(m, (n_batch, n_tok, m.shape[-1])).reshape(n_rows, m.shape[-1])

    @pl.when(step == 0)
    def _():
        mod = mod_ref[...]
        sh1, sc1 = per_token(mod[:, 0:1, :]), per_token(mod[:, 1:2, :])
        h = (_rms(x_ref[...], n1g_ref[...]) * (1.0 + sc1) + sh1).astype(BF16)
        h_scr[...] = h
        q = _head_rms(_dot(h, win_ref[:, _Q0:_Q0 + D_A]), qg_ref[...], e_ref)
        k = _head_rms(_dot(h, win_ref[:, _K0:_K0 + D_A]), kg_ref[...], e_ref)
        v = _dot(h, win_ref[:, _V0:_V0 + D_A])
        ks_ref[...] = k
        vs_ref[...] = v
        kn_scr[...] = k.astype(BF16)
        vn_scr[...] = v.astype(BF16)
        q_scr[...] = (q * (HEAD_DIM ** -0.5)).astype(BF16)
        for hh in range(N_HEADS):
            bias_scr[hh * n_tok:(hh + 1) * n_tok, :] = _rel_rows(rel_ref, hh, n_tok)[:, 0:cache_len + LANES]

    lane_head = lax.broadcasted_iota(jnp.int32, (n_tok, D_A), 1) // HEAD_DIM
    for cp in cache_copies(step):
        cp.wait()
    slot = step % CACHE_SLOTS
    for i in range(SAMPLE_BG):
        r0 = pl.multiple_of((step * SAMPLE_BG + i) * n_tok, n_tok)
        qb = q_scr[pl.ds(r0, n_tok), :]
        q_bd = jnp.concatenate([jnp.where(lane_head == hh, qb, jnp.zeros_like(qb)) for hh in range(N_HEADS)],
                               axis=0)
        kn = kn_scr[pl.ds(r0, n_tok), :]
        vn = vn_scr[pl.ds(r0, n_tok), :]
        kt = kc_buf[slot, i].reshape(D_A, cache_len).astype(BF16)
        vt = vc_buf[slot, i].reshape(D_A, cache_len).astype(BF16)
        s_c = _dot(q_bd, kt) + bias_scr[:, 0:cache_len]
        s_n = _dot_nt(q_bd, kn) + bias_scr[:, cache_len:cache_len + n_tok]
        m = jnp.maximum(jnp.max(s_c, axis=-1, keepdims=True), jnp.max(s_n, axis=-1, keepdims=True))
        e_c = jnp.exp(s_c - m)
        e_n = jnp.exp(s_n - m)
        l = jnp.sum(e_c, axis=-1, keepdims=True) + jnp.sum(e_n, axis=-1, keepdims=True)
        o_all = (_dot_nt(e_c.astype(BF16), vt) + _dot(e_n.astype(BF16), vn)) / l
        o = jnp.zeros((n_tok, D_A), F32)
        for hh in range(N_HEADS):
            o = jnp.where(lane_head == hh, o_all[hh * n_tok:(hh + 1) * n_tok, :], o)
        oa_scr[pl.ds(r0, n_tok), :] = o

    def spatial_half(half_scr, c0):
        z = _gelu(_dot(h_scr[...], win_ref[:, c0:c0 + D_B]))
        for g in range(N_GROUPS):
            half_scr[g] = z[:, g * GROUP_DIM:(g + 1) * GROUP_DIM]

    def spatial_mix():
        vbn = _rms(jnp.concatenate([vb_scr[g] for g in range(N_GROUPS)], axis=1), vg_ref[...])
        sv_ref[...] = vbn
        for g in range(N_GROUPS):
            vb_scr[g] = vbn[:, g * GROUP_DIM:(g + 1) * GROUP_DIM]
        for g in range(N_GROUPS):
            at_pos = [vb_scr[g, pl.ds(s, n_batch, stride=n_tok), :] for s in range(n_tok)]
            for t in range(n_tok):
                mix = bsp_ref[g, t:t + 1, :]
                for s in range(t + 1):
                    mix = mix + wsp_ref[g, t, s:s + 1, :] * at_pos[s]
                ob_scr[g, pl.ds(t, n_batch, stride=n_tok), :] = ub_scr[g, pl.ds(t, n_batch, stride=n_tok), :] * mix

    def gates(c0, width):
        gate_scr[:, c0:c0 + width] = _sigmoid(_dot(h_scr[...], win_ref[:, _G0 + c0:_G0 + c0 + width]))

    stages = [functools.partial(spatial_half, ub_scr, _B0), functools.partial(spatial_half, vb_scr, _B0 + D_B),
              spatial_mix] + [functools.partial(gates, c0, D_B) for c0 in range(0, 2 * D_MODEL, D_B)]
    for i, stage in enumerate(stages):
        pl.when(step == i + 1)(stage)

    @pl.when(step == pl.num_programs(0) - 1)
    def _():
        ob = jnp.concatenate([ob_scr[g] for g in range(N_GROUPS)], axis=1)
        merged = (gate_scr[:, 0:D_MODEL] * _dot(oa_scr[...].astype(BF16), woa_ref[...])
                  + gate_scr[:, D_MODEL:] * _dot(ob.astype(BF16), wob_ref[...]))
        gt1 = per_token(mod_ref[...][:, 2:3, :])
        xo_ref[...] = x_ref[...] + gt1 * _dot(merged.astype(BF16), wo_ref[...])


def _mixer_sample_call(l, depth, x2, mod_all, kc_all, vc_all, p, w, n_batch, n_tok, carried):
    n_rows = n_batch * n_tok
    cache_len = kc_all.shape[-1]
    cache = pl.BlockSpec(memory_space=pl.ANY)
    n_steps = n_batch // SAMPLE_BG
    assert n_steps >= CACHE_AHEAD and CACHE_SLOTS == CACHE_AHEAD + 1
    full = lambda shape: pl.BlockSpec(shape, lambda s: (0,) * len(shape))
    state = lambda width: pl.BlockSpec((None, n_rows, width), lambda s: (l, 0, 0))
    in_specs = [
        _const_spec((n_rows, D_MODEL)),
        _layer_spec(l, (n_batch, 6, D_MODEL)),
        _layer_spec(l, (1, D_MODEL)),
        _const_spec((D_MODEL, D_IN)),
        _layer_spec(l, (1, D_A)),
        _layer_spec(l, (1, D_A)),
        _const_spec((MXU_DIM, MXU_DIM)),
        _layer_spec(l, (1, D_B)),
        _layer_spec(l, (N_HEADS, REL_SPAN)),
        cache,
        cache,
        _layer_spec(l, (N_GROUPS, n_tok, n_tok, GROUP_DIM)),
        _layer_spec(l, (N_GROUPS, n_tok, GROUP_DIM)),
        _const_spec((D_A, D_MODEL)),
        _const_spec((D_B, D_MODEL)),
        _const_spec((D_MODEL, D_MODEL)),
    ]
    return pl.pallas_call(
        functools.partial(_mixer_sample_kernel, layer=l, n_carried=len(carried), n_batch=n_batch, n_tok=n_tok,
                          cache_len=cache_len),
        grid=(n_steps,),
        input_output_aliases={len(in_specs) + i: 1 + i for i in range(len(carried))},
        in_specs=in_specs + [pl.BlockSpec(memory_space=pl.ANY)] * len(carried),
        out_specs=[full((n_rows, D_MODEL)), state(D_A), state(D_A), state(D_B)],
        out_shape=[
            jax.ShapeDtypeStruct((n_rows, D_MODEL), F32),
            jax.ShapeDtypeStruct((depth, n_rows, D_A), F32),
            jax.ShapeDtypeStruct((depth, n_rows, D_A), F32),
            jax.ShapeDtypeStruct((depth, n_rows, D_B), F32),
        ],
        scratch_shapes=[
            pltpu.VMEM((n_rows, D_MODEL), BF16),
            pltpu.VMEM((n_rows, D_A), BF16),
            pltpu.VMEM((n_rows, D_A), BF16),
            pltpu.VMEM((n_rows, D_A), BF16),
            pltpu.VMEM((n_rows, D_A), F32),
            pltpu.VMEM((N_GROUPS, n_rows, GROUP_DIM), F32),
            pltpu.VMEM((N_GROUPS, n_rows, GROUP_DIM), F32),
            pltpu.VMEM((N_GROUPS, n_rows, GROUP_DIM), F32),
            pltpu.VMEM((n_rows, 2 * D_MODEL), F32),
            pltpu.VMEM((N_HEADS * n_tok, cache_len + LANES), F32),
            pltpu.VMEM((CACHE_SLOTS, SAMPLE_BG, N_HEADS, HEAD_DIM, cache_len), F32),
            pltpu.VMEM((CACHE_SLOTS, SAMPLE_BG, N_HEADS, HEAD_DIM, cache_len), F32),
            pltpu.SemaphoreType.DMA((2, CACHE_SLOTS)),
        ],
        compiler_params=pltpu.CompilerParams(
            dimension_semantics=("arbitrary",), vmem_limit_bytes=VMEM_LIMIT),
        name="mixer_sample",
    )(x2, mod_all, p["n1g"], w["win"], p["qg"], p["kg"], p["e_mat"], p["vg"], p["rel"],
      kc_all, vc_all, p["wsp_s"], p["bsp_s"], w["woa"], w["wob"], w["wo"], *carried)


def _ffn_sample_kernel(x_ref, mod_ref, n2g_ref, wfi_ref, cw_ref, cb_ref, cc_ref, wfo_ref,
                       xo_ref, cs_ref, *, n_batch, n_tok):
    n_rows = n_batch * n_tok

    def per_token(m):
        return jnp.broadcast_to(m, (n_batch, n_tok, m.shape[-1])).reshape(n_rows, m.shape[-1])

    mod = mod_ref[...]
    cc = cc_ref[...]
    row = lax.broadcasted_iota(jnp.int32, (n_rows, 1), 0) % n_tok
    xo, gs = _conv_ffn(x_ref[...], per_token(mod[:, 3:4, :]), per_token(mod[:, 4:5, :]), per_token(mod[:, 5:6, :]),
                       per_token(cc[:, 1:2, :]), per_token(cc[:, 0:1, :]), row,
                       n2g_ref, wfi_ref, cw_ref, cb_ref, wfo_ref)
    xo_ref[...] = xo
    for g, (c0, c1) in zip(gs, FF_CHUNKS):
        cs_ref[:, :, c0:c1] = g.reshape(n_batch, n_tok, c1 - c0)[:, n_tok - SUBLANES:, :]


def _ffn_sample_call(l, x2, mod_all, cc_all, p, w, n_batch, n_tok):
    n_rows = n_batch * n_tok
    return pl.pallas_call(
        functools.partial(_ffn_sample_kernel, n_batch=n_batch, n_tok=n_tok),
        grid=(1,),
        in_specs=[
            _const_spec((n_rows, D_MODEL)),
            _layer_spec(l, (n_batch, 6, D_MODEL)),
            _layer_spec(l, (1, D_MODEL)),
            _const_spec((D_MODEL, 2 * D_FF)),
            _layer_spec(l, (CONV_W, D_FF)),
            _layer_spec(l, (1, D_FF)),
            _layer_spec(l, (n_batch, CONV_W - 1, D_FF)),
            _const_spec((D_FF, D_MODEL)),
        ],
        out_specs=[
            pl.BlockSpec((n_rows, D_MODEL), lambda c: (0, 0)),
            pl.BlockSpec((n_batch, SUBLANES, D_FF), lambda c: (0, 0, 0)),
        ],
        out_shape=[
            jax.ShapeDtypeStruct((n_rows, D_MODEL), F32),
            jax.ShapeDtypeStruct((n_batch, SUBLANES, D_FF), F32),
        ],
        compiler_params=pltpu.CompilerParams(
            dimension_semantics=("arbitrary",), vmem_limit_bytes=VMEM_LIMIT),
        name="ffn_sample",
    )(x2, mod_all, p["n2g"], w["wfi"], p["cw"], p["cb"], cc_all, w["wfo"])


def _rel_vector(rel_bias):
    far = rel_bias[..., 2 * REL_CLIP:]
    lead = ATTN_REACH - REL_CLIP
    return jnp.concatenate([
        jnp.broadcast_to(far, (*rel_bias.shape[:-1], lead)),
        rel_bias[..., ::-1],
        jnp.broadcast_to(far, (*rel_bias.shape[:-1], REL_SPAN - lead - (2 * REL_CLIP + 1))),
    ], axis=-1)


def kernel(x_prompt, x_sample, cache_attn_k, cache_attn_v, cache_ffn_conv, c_prompt, c_sample, norm1_g, norm2_g, w_ada, b_ada, w_in, q_norm_g, k_norm_g, rel_bias, v_norm_g, w_spatial, b_spatial, w_out_a, w_out_b, w_out, w_ffn_in, ffn_conv_w, ffn_conv_b, w_ffn_out):
    depth = w_in.shape[0]
    B, S, _ = x_prompt.shape
    NB, NT, _ = x_sample.shape
    cache_len = cache_attn_k.shape[2]
    assert S % FFN_TILE_T == 0 and S % TILE_T == 0 and TILE_T % Q_BLOCK == 0 and TILE_T == ATTN_REACH and S >= ATTN_REACH
    assert NB % SAMPLE_BG == 0 and NB // SAMPLE_BG >= SAMPLE_MIN_STEPS and NT % SUBLANES == 0 and NT <= MLP_CHUNK and N_HEADS * NT == LANES
    assert cache_len == ATTN_REACH and rel_bias.shape[-1] == 2 * REL_CLIP + 1

    rows = NB + B
    rows_pad = -(-rows // SUBLANES) * SUBLANES
    c_all = jnp.concatenate([c_sample, c_prompt, jnp.zeros((rows_pad - rows, D_MODEL), F32)], axis=0)
    mixer_names, ffn_names = ("win", "woa", "wob", "wo"), ("wfi", "wfo")
    mixer_f32, ffn_f32 = (w_in, w_out_a, w_out_b, w_out), (w_ffn_in, w_ffn_out)
    mod_all, *first = _ada_call(c_all, w_ada, b_ada, mixer_f32)
    mod_all = mod_all.reshape(depth, rows_pad, 6, D_MODEL)
    w = dict(zip(mixer_names, first))

    head_of = jnp.arange(MXU_DIM) // HEAD_DIM
    tril = jnp.tril(jnp.ones((NT, NT), F32))
    ws_small = w_spatial[:, :, :NT, :NT] * tril
    params = {
        "n1g": norm1_g[:, None, :], "n2g": norm2_g[:, None, :],
        "qg": jnp.tile(q_norm_g, (1, N_HEADS))[:, None, :], "kg": jnp.tile(k_norm_g, (1, N_HEADS))[:, None, :],
        "vg": v_norm_g[:, None, :],
        "e_mat": jnp.where(head_of[:, None] == head_of[None, :], 1.0 / HEAD_DIM, 0.0).astype(BF16),
        "rel": _rel_vector(rel_bias),
        "wsp": w_spatial,
        "bsp_b": jnp.broadcast_to(b_spatial[:, :, :, None], (depth, N_GROUPS, MLP_CHUNK, GROUP_DIM)),
        "wsp_s": jnp.broadcast_to(ws_small[..., None], (depth, N_GROUPS, NT, NT, GROUP_DIM)),
        "bsp_s": jnp.broadcast_to(b_spatial[:, :, :NT, None], (depth, N_GROUPS, NT, GROUP_DIM)),
        "cw": ffn_conv_w, "cb": ffn_conv_b[:, None, :],
    }
    kc_all = cache_attn_k.transpose(0, 1, 3, 4, 2)
    vc_all = cache_attn_v.transpose(0, 1, 3, 4, 2)

    xp = x_prompt
    xs = x_sample.reshape(NB * NT, D_MODEL)
    conv_p, conv_s = [], []
    prompt_state, sample_state = (), ()
    for l in range(depth):
        xp, *outs = _mixer_prompt_call(l, depth, xp, mod_all, NB, params, w, ffn_f32, tuple(prompt_state))
        prompt_state, w_ffn = outs[:2], dict(zip(ffn_names, outs[2:]))
        nxt = mixer_f32 if l + 1 < depth else ()
        xp, cp, *outs = _ffn_prompt_call(l, l + 1, xp, mod_all, NB, params, w_ffn, nxt)
        xs, *sample_state = _mixer_sample_call(l, depth, xs, mod_all, kc_all, vc_all, params, w, NB, NT,
                                               tuple(sample_state))
        xs, cs = _ffn_sample_call(l, xs, mod_all, cache_ffn_conv, params, w_ffn, NB, NT)
        w = dict(zip(mixer_names, outs))
        conv_p.append(cp[:, SUBLANES - (CONV_W - 1):])
        conv_s.append(cs[:, SUBLANES - (CONV_W - 1):])
    new_k_prompt, new_v_prompt = (
        s.reshape(depth, B, N_HEADS, HEAD_DIM, ATTN_REACH).transpose(0, 1, 4, 2, 3) for s in prompt_state)
    ks, vs, sv = sample_state
    return (xp, xs.reshape(NB, NT, D_MODEL), new_k_prompt, new_v_prompt, jnp.stack(conv_p),
            ks.reshape(depth, NB, NT, N_HEADS, HEAD_DIM), vs.reshape(depth, NB, NT, N_HEADS, HEAD_DIM),
            sv.reshape(depth, NB, NT, N_GROUPS, GROUP_DIM), jnp.stack(conv_s))
```

```python
import functools

import jax
import jax.numpy as jnp
from jax import lax
from jax.experimental import pallas as pl
from jax.experimental.pallas import tpu as pltpu

F32 = jnp.float32
BF16 = jnp.bfloat16

D_MODEL = 1024
CHUNK = 64
ATTN_REACH = 512
N_HEADS = 8
HEAD_DIM = 64
D_A = N_HEADS * HEAD_DIM
REL_CLIP = 128
MLP_CHUNK = 128
N_GROUPS = 4
D_B = 512
GROUP_DIM = D_B // N_GROUPS
D_FF = 2816
CONV_W = 3
EPS = 1e-6

LANES = 128
SUBLANES = 8
BF16_ROWS = 16
MXU_DIM = 256
N_PAIRS = N_HEADS // 2
Q_BLOCK = 2 * CHUNK
KV_WINDOW = ATTN_REACH + Q_BLOCK
REL_SPAN = KV_WINDOW + Q_BLOCK
TILE_T = 512
FFN_TILE_T = 512
HIST = ATTN_REACH
FF_CHUNKS = ((0, 1280), (1280, 2560), (2560, D_FF))
ADA_BLOCK_N = 1536
SAMPLE_BG = 2
CACHE_AHEAD = 3
CACHE_SLOTS = CACHE_AHEAD + 1
SAMPLE_STAGE_STRIDE = 2
SAMPLE_MIN_STEPS = 6 * SAMPLE_STAGE_STRIDE + 2
VMEM_LIMIT = 56 * 1024 * 1024
_Q0, _K0, _V0, _B0, _G0 = 0, D_A, 2 * D_A, 3 * D_A, 3 * D_A + 2 * D_B
D_IN = _G0 + 2 * D_MODEL


def _dot(a, b):
    return jnp.dot(a, b, preferred_element_type=F32)


def _dot_nt(a, b):
    return lax.dot_general(a, b, (((1,), (1,)), ((), ())), preferred_element_type=F32)


def _dot_tn(a, b):
    return lax.dot_general(a, b, (((0,), (0,)), ((), ())), preferred_element_type=F32)


_GELU_C = 0.7978845608028654
_GELU_K = 0.044715


def _gelu(x):
    half = 0.5 * x
    return half + half * jnp.tanh(x * (_GELU_C + (_GELU_C * _GELU_K) * (x * x)))


def _sigmoid(x):
    return 0.5 * jnp.tanh(0.5 * x) + 0.5


def _rms(x, g):
    return (x * lax.rsqrt(jnp.mean(x * x, axis=-1, keepdims=True) + EPS)) * g


def _head_rms(a, g, e_ref):
    sq = a * a
    hi = sq.astype(BF16)
    lo = (sq - hi.astype(F32)).astype(BF16)
    e = e_ref[...]
    w = e.shape[0]
    ms = jnp.concatenate([_dot(hi[:, c:c + w], e) + _dot(lo[:, c:c + w], e) for c in range(0, a.shape[1], w)],
                         axis=1)
    return (a * lax.rsqrt(ms + EPS)) * g


def _rel_rows(rel_ref, h, n_rows):
    wb = jnp.broadcast_to(rel_ref[h:h + 1, :], (n_rows, REL_SPAN))
    return pltpu.roll(wb, 0, axis=1, stride=1, stride_axis=0)


def _const_spec(shape):
    n = len(shape)
    return pl.BlockSpec(shape, lambda *_: (0,) * n, pipeline_mode=pl.Buffered(1))


def _cast_specs(l, stacked, n_steps, step_of):
    in_specs, out_specs, out_shapes = [], [], []
    for a in stacked:
        rows, cols = a.shape[1:]
        blk = next(r for r in range(BF16_ROWS, rows + 1, BF16_ROWS) if rows % r == 0 and rows // r <= n_steps)
        last = rows // blk - 1
        in_specs.append(pl.BlockSpec((None, blk, cols), lambda *ids, last=last: (l, jnp.minimum(step_of(*ids), last), 0)))
        out_specs.append(pl.BlockSpec((blk, cols), lambda *ids, last=last: (jnp.minimum(step_of(*ids), last), 0)))
        out_shapes.append(jax.ShapeDtypeStruct((rows, cols), BF16))
    return in_specs, out_specs, out_shapes


def _layer_spec(l, shape):
    n = len(shape)
    return pl.BlockSpec((None, *shape), lambda *_: (l,) + (0,) * n, pipeline_mode=pl.Buffered(1))


def _ada_kernel(c_ref, w_ref, b_ref, *rest, n_cast):
    cast_in, o_ref, cast_out = rest[:n_cast], rest[n_cast], rest[n_cast + 1:]
    s = jax.nn.silu(c_ref[...]).astype(BF16)
    o_ref[...] = _dot(s, w_ref[...].astype(BF16)) + b_ref[...]
    for src, dst in zip(cast_in, cast_out):
        dst[...] = src[...].astype(BF16)


def _ada_call(c_all, w_ada, b_ada, to_cast):
    depth = w_ada.shape[0]
    rows = c_all.shape[0]
    n_out = w_ada.shape[2]
    n_blk = n_out // ADA_BLOCK_N
    cast_in, cast_out, cast_shapes = _cast_specs(0, to_cast, depth * n_blk, lambda l, n: l * n_blk + n)
    return pl.pallas_call(
        functools.partial(_ada_kernel, n_cast=len(to_cast)),
        grid=(depth, n_blk),
        in_specs=[
            pl.BlockSpec((rows, D_MODEL), lambda l, n: (0, 0)),
            pl.BlockSpec((None, D_MODEL, ADA_BLOCK_N), lambda l, n: (l, 0, n)),
            pl.BlockSpec((None, 1, ADA_BLOCK_N), lambda l, n: (l, 0, n)),
        ] + cast_in,
        out_specs=[pl.BlockSpec((None, rows, ADA_BLOCK_N), lambda l, n: (l, 0, n))] + cast_out,
        out_shape=[jax.ShapeDtypeStruct((depth, rows, n_out), F32)] + cast_shapes,
        compiler_params=pltpu.CompilerParams(
            dimension_semantics=("arbitrary", "arbitrary"), vmem_limit_bytes=VMEM_LIMIT),
        name="ada_mod",
    )(c_all, w_ada, b_ada.reshape(depth, 1, n_out), *to_cast)


def _mixer_prompt_kernel(x_ref, mod_ref, n1g_ref, win_ref, qg_ref, kg_ref, e_ref, vg_ref, rel_ref,
                         wsp_ref, bsp_ref, woa_ref, wob_ref, wo_ref, *rest, n_cast, n_carried):
    cast_in, rest = rest[:n_cast], rest[n_cast + n_carried:]
    xo_ref, kst_ref, vst_ref = rest[:3]
    cast_out = rest[3:3 + n_cast]
    kbuf, vtbuf, oat_scr, ob_scr, bias_scr, st_scr, e_scr = rest[3 + n_cast:]
    b = pl.program_id(0)
    t = pl.program_id(1)
    T = TILE_T
    for src, dst in zip(cast_in, cast_out):
        dst[...] = src[...].astype(BF16)

    @pl.when((b == 0) & (t == 0))
    def _():
        qi = lax.broadcasted_iota(jnp.int32, (Q_BLOCK, KV_WINDOW), 0) // CHUNK
        ki = lax.broadcasted_iota(jnp.int32, (Q_BLOCK, KV_WINDOW), 1) // CHUNK
        band = (ki >= qi) & (ki <= qi + ATTN_REACH // CHUNK)
        for h in range(N_HEADS):
            tab = jnp.where(band, _rel_rows(rel_ref, h, Q_BLOCK)[:, 0:KV_WINDOW], -jnp.inf)
            bias_scr[h // 2, 0:KV_WINDOW, (h % 2) * Q_BLOCK:(h % 2 + 1) * Q_BLOCK] = tab.T
        bias_scr[:, KV_WINDOW:, :] = jnp.full((N_PAIRS, Q_BLOCK, 2 * Q_BLOCK), -jnp.inf, F32)

    @pl.when(t == 0)
    def _():
        kbuf[0:HIST, :] = jnp.zeros((HIST, D_A), BF16)
        vtbuf[:, 0:HIST] = jnp.zeros((D_A, HIST), BF16)

    @pl.when(t > 0)
    def _():
        for r in range(0, HIST, T):
            kbuf[r:r + T, :] = kbuf[r + T:r + 2 * T, :]
            vtbuf[:, r:r + T] = vtbuf[:, r + T:r + 2 * T]

    mod = mod_ref[...]
    sh1, sc1, gt1 = mod[0:1], mod[1:2], mod[2:3]

    x = x_ref[...]
    h = (_rms(x, n1g_ref[...]) * (1.0 + sc1) + sh1).astype(BF16)

    q = _head_rms(_dot(h, win_ref[:, _Q0:_Q0 + D_A]), qg_ref[...], e_ref)
    k = _head_rms(_dot(h, win_ref[:, _K0:_K0 + D_A]), kg_ref[...], e_ref)
    v = _dot(h, win_ref[:, _V0:_V0 + D_A])
    vt = v.T
    kbuf[HIST:HIST + T, :] = k.astype(BF16)
    vtbuf[:, HIST:HIST + T] = vt.astype(BF16)

    kst_ref[...] = k.T
    vst_ref[...] = vt

    qb = (q * (HEAD_DIM ** -0.5)).astype(BF16)

    lane = lax.broadcasted_iota(jnp.int32, (1, LANES), 1)
    low_half = lane < HEAD_DIM

    n_qb = T // Q_BLOCK

    def scores(p, j, slot):
        r0, c0 = j * Q_BLOCK, p * LANES
        qp = qb[r0:r0 + Q_BLOCK, c0:c0 + LANES]
        q2 = jnp.concatenate([jnp.where(low_half, qp, jnp.zeros_like(qp)),
                              jnp.where(low_half, jnp.zeros_like(qp), qp)], axis=0)
        st = _dot_nt(kbuf[r0:r0 + KV_WINDOW, c0:c0 + LANES], q2)
        first_valid = HIST - t * T - r0
        m = None
        for r in range(0, KV_WINDOW, Q_BLOCK):
            src = jnp.where(first_valid > r, KV_WINDOW, r) if r < HIST else r
            blk = st[r:r + Q_BLOCK] + bias_scr[p, pl.ds(pl.multiple_of(src, Q_BLOCK), Q_BLOCK), :]
            st_scr[slot, j, r:r + Q_BLOCK, :] = blk
            bm = jnp.max(blk, axis=0, keepdims=True)
            m = bm if m is None else jnp.maximum(m, bm)
        return m

    def weights(j, slot, m):
        e = jnp.exp(st_scr[slot, j] - m)
        e_scr[slot, j] = e.astype(BF16)
        return jnp.sum(e, axis=0, keepdims=True)

    def outputs(p, j, slot, total):
        r0, c0 = j * Q_BLOCK, p * LANES
        ot = _dot(vtbuf[c0:c0 + LANES, r0:r0 + KV_WINDOW], e_scr[slot, j]) / total
        oat_scr[c0:c0 + HEAD_DIM, r0:r0 + Q_BLOCK] = ot[0:HEAD_DIM, 0:Q_BLOCK]
        oat_scr[c0 + HEAD_DIM:c0 + LANES, r0:r0 + Q_BLOCK] = ot[HEAD_DIM:, Q_BLOCK:]

    side = {}

    def project(name, act, c0):
        side[name] = act(_dot(h, win_ref[:, c0:c0 + MXU_DIM]))

    side_work = ([functools.partial(project, ("b", i), _gelu, _B0 + i * MXU_DIM)
                  for i in range(2 * D_B // MXU_DIM)]
                 + [functools.partial(project, ("g", i), _sigmoid, _G0 + i * MXU_DIM)
                    for i in range(2 * D_MODEL // MXU_DIM)])
    maxima = [scores(0, j, 0) for j in range(n_qb)]
    for p in range(N_PAIRS):
        slot = p % 2
        if p + 1 < N_PAIRS:
            next_maxima = [scores(p + 1, j, 1 - slot) for j in range(n_qb)]
        for work in side_work[len(side_work) * p // N_PAIRS:len(side_work) * (p + 1) // N_PAIRS]:
            work()
        totals = [weights(j, slot, maxima[j]) for j in range(n_qb)]
        for j in range(n_qb):
            outputs(p, j, slot, totals[j])
        maxima = next_maxima
    n_b, n_g = D_B // MXU_DIM, D_MODEL // MXU_DIM
    ub = jnp.concatenate([side["b", i] for i in range(n_b)], axis=1)
    vbn = _rms(jnp.concatenate([side["b", n_b + i] for i in range(n_b)], axis=1), vg_ref[...]).astype(BF16)
    ga = jnp.concatenate([side["g", i] for i in range(n_g)], axis=1)
    gb = jnp.concatenate([side["g", n_g + i] for i in range(n_g)], axis=1)
    row_i = lax.broadcasted_iota(jnp.int32, (MLP_CHUNK, MLP_CHUNK), 0)
    col_i = lax.broadcasted_iota(jnp.int32, (MLP_CHUNK, MLP_CHUNK), 1)
    for g in range(N_GROUPS):
        wc = jnp.where(row_i >= col_i, wsp_ref[g], 0.0).astype(BF16)
        g0 = g * GROUP_DIM
        for c in range(T // MLP_CHUNK):
            r0 = c * MLP_CHUNK
            mix = _dot(wc, vbn[r0:r0 + MLP_CHUNK, g0:g0 + GROUP_DIM]) + bsp_ref[g]
            ob_scr[r0:r0 + MLP_CHUNK, g0:g0 + GROUP_DIM] = ub[r0:r0 + MLP_CHUNK, g0:g0 + GROUP_DIM] * mix

    merged = (ga * _dot_tn(oat_scr[...].astype(BF16), woa_ref[...])
              + gb * _dot(ob_scr[...].astype(BF16), wob_ref[...]))
    xo_ref[...] = x + gt1 * _dot(merged.astype(BF16), wo_ref[...])


def _mixer_prompt_call(l, depth, x, mod_all, mod_row0, p, w, to_cast, carried):
    B, S, _ = x.shape
    T = TILE_T
    n_t = S // T
    tok = pl.BlockSpec((None, T, D_MODEL), lambda b, t: (b, t, 0))
    state = pl.BlockSpec((None, None, D_A, ATTN_REACH), lambda b, t: (l, b, 0, 0))
    cast_in, cast_out, cast_shapes = _cast_specs(l, to_cast, B * n_t, lambda b, t: b * n_t + t)
    in_specs = [
        tok,
        pl.BlockSpec((None, None, 6, D_MODEL), lambda b, t: (l, mod_row0 + b, 0, 0)),
        _layer_spec(l, (1, D_MODEL)),
        _const_spec((D_MODEL, D_IN)),
        _layer_spec(l, (1, D_A)),
        _layer_spec(l, (1, D_A)),
        _const_spec((MXU_DIM, MXU_DIM)),
        _layer_spec(l, (1, D_B)),
        _layer_spec(l, (N_HEADS, REL_SPAN)),
        _layer_spec(l, (N_GROUPS, MLP_CHUNK, MLP_CHUNK)),
        _layer_spec(l, (N_GROUPS, MLP_CHUNK, GROUP_DIM)),
        _const_spec((D_A, D_MODEL)),
        _const_spec((D_B, D_MODEL)),
        _const_spec((D_MODEL, D_MODEL)),
    ]
    in_specs += cast_in
    return pl.pallas_call(
        functools.partial(_mixer_prompt_kernel, n_cast=len(to_cast), n_carried=len(carried)),
        grid=(B, n_t),
        input_output_aliases={len(in_specs) + i: 1 + i for i in range(len(carried))},
        in_specs=in_specs + [pl.BlockSpec(memory_space=pl.ANY)] * len(carried),
        out_specs=[tok, state, state] + cast_out,
        out_shape=[
            jax.ShapeDtypeStruct((B, S, D_MODEL), F32),
            jax.ShapeDtypeStruct((depth, B, D_A, ATTN_REACH), F32),
            jax.ShapeDtypeStruct((depth, B, D_A, ATTN_REACH), F32),
        ] + cast_shapes,
        scratch_shapes=[
            pltpu.VMEM((HIST + T, D_A), BF16),
            pltpu.VMEM((D_A, HIST + T), BF16),
            pltpu.VMEM((D_A, T), F32),
            pltpu.VMEM((T, D_B), F32),
            pltpu.VMEM((N_PAIRS, KV_WINDOW + Q_BLOCK, 2 * Q_BLOCK), F32),
            pltpu.VMEM((2, T // Q_BLOCK, KV_WINDOW, 2 * Q_BLOCK), F32),
            pltpu.VMEM((2, T // Q_BLOCK, KV_WINDOW, 2 * Q_BLOCK), BF16),
        ],
        compiler_params=pltpu.CompilerParams(
            dimension_semantics=("arbitrary", "arbitrary"), vmem_limit_bytes=VMEM_LIMIT),
        name="mixer_prompt",
    )(x, mod_all, p["n1g"], w["win"], p["qg"], p["kg"], p["e_mat"], p["vg"], p["rel"],
      p["wsp"], p["bsp_b"], w["woa"], w["wob"], w["wo"], *to_cast, *carried)


def _conv_gate(g, u, prev1, prev2, row, cw_ref, cb_ref, c0, c1):
    gm1 = jnp.where(row == 0, prev1, pltpu.roll(g, 1, axis=0))
    gm2 = jnp.where(row == 0, prev2, jnp.where(row == 1, prev1, pltpu.roll(g, 2, axis=0)))
    gc = cb_ref[:, c0:c1] + cw_ref[0:1, c0:c1] * gm2
    gc = gc + cw_ref[1:2, c0:c1] * gm1
    gc = gc + cw_ref[2:3, c0:c1] * g
    return _gelu(gc) * u


def _conv_ffn(x, sh2, sc2, gt2, prev1, prev2, row, n2g_ref, wfi_ref, cw_ref, cb_ref, wfo_ref):
    if sc2.shape[0] == 1:
        h2 = (_rms(x, n2g_ref[...] * (1.0 + sc2)) + sh2).astype(BF16)
    else:
        h2 = (_rms(x, n2g_ref[...]) * (1.0 + sc2) + sh2).astype(BF16)

    def up_project(c0, c1):
        return _dot(h2, wfi_ref[:, c0:c1]), _dot(h2, wfi_ref[:, D_FF + c0:D_FF + c1])

    f = jnp.zeros(x.shape, F32)
    gs = []
    ahead = up_project(*FF_CHUNKS[0])
    for i, (c0, c1) in enumerate(FF_CHUNKS):
        g, u = ahead
        if i + 1 < len(FF_CHUNKS):
            ahead = up_project(*FF_CHUNKS[i + 1])
        act = _conv_gate(g, u, prev1[:, c0:c1], prev2[:, c0:c1], row, cw_ref, cb_ref, c0, c1)
        f = f + _dot(act.astype(BF16), wfo_ref[c0:c1, :])
        gs.append(g)
    return x + gt2 * f, gs


def _ffn_prompt_kernel(x_ref, mod_ref, n2g_ref, wfi_ref, cw_ref, cb_ref, wfo_ref, *rest, n_cast):
    cast_in, (xo_ref, cs_ref), cast_out, carry = rest[:n_cast], rest[n_cast:n_cast + 2], rest[n_cast + 2:-1], rest[-1]
    t = pl.program_id(1)
    T = FFN_TILE_T
    mod = mod_ref[...]
    for src, dst in zip(cast_in, cast_out):
        dst[...] = src[...].astype(BF16)

    @pl.when(t == 0)
    def _():
        carry[...] = jnp.zeros(carry.shape, F32)

    row = lax.broadcasted_iota(jnp.int32, (T, 1), 0)
    prev = carry[...]
    xo, gs = _conv_ffn(x_ref[...], mod[3:4], mod[4:5], mod[5:6], prev[SUBLANES - 1:], prev[SUBLANES - 2:SUBLANES - 1],
                       row, n2g_ref, wfi_ref, cw_ref, cb_ref, wfo_ref)
    xo_ref[...] = xo
    tail = jnp.concatenate([g[T - SUBLANES:T, :] for g in gs], axis=1)
    carry[...] = tail
    cs_ref[...] = tail


def _ffn_prompt_call(l, cast_l, x, mod_all, mod_row0, p, w, to_cast):
    B, S, _ = x.shape
    T = FFN_TILE_T
    n_t = S // T
    tok = pl.BlockSpec((None, T, D_MODEL), lambda b, t: (b, t, 0))
    cast_in, cast_out, cast_shapes = _cast_specs(cast_l, to_cast, B * n_t, lambda b, t: b * n_t + t)
    return pl.pallas_call(
        functools.partial(_ffn_prompt_kernel, n_cast=len(to_cast)),
        grid=(B, n_t),
        in_specs=[
            tok,
            pl.BlockSpec((None, None, 6, D_MODEL), lambda b, t: (l, mod_row0 + b, 0, 0)),
            _layer_spec(l, (1, D_MODEL)),
            _const_spec((D_MODEL, 2 * D_FF)),
            _layer_spec(l, (CONV_W, D_FF)),
            _layer_spec(l, (1, D_FF)),
            _const_spec((D_FF, D_MODEL)),
        ] + cast_in,
        out_specs=[tok, pl.BlockSpec((None, SUBLANES, D_FF), lambda b, t: (b, 0, 0))] + cast_out,
        out_shape=[
            jax.ShapeDtypeStruct((B, S, D_MODEL), F32),
            jax.ShapeDtypeStruct((B, SUBLANES, D_FF), F32),
        ] + cast_shapes,
        scratch_shapes=[pltpu.VMEM((SUBLANES, D_FF), F32)],
        compiler_params=pltpu.CompilerParams(
            dimension_semantics=("arbitrary", "arbitrary"), vmem_limit_bytes=VMEM_LIMIT),
        name="ffn_prompt",
    )(x, mod_all, p["n2g"], w["wfi"], p["cw"], p["cb"], w["wfo"], *to_cast)


def _mixer_sample_kernel(x_ref, mod_ref, n1g_ref, win_ref, qg_ref, kg_ref, e_ref, vg_ref, rel_ref,
                         kc_ref, vc_ref, wsp_ref, bsp_ref, woa_ref, wob_ref, wo_ref, *rest,
                         layer, n_carried, n_batch, n_tok, cache_len):
    (xo_ref, ks_ref, vs_ref, sv_ref,
     h_scr, q_scr, kn_scr, vn_scr, oa_scr, ub_scr, vb_scr, ob_scr, gate_scr, bias_scr,
     kc_buf, vc_buf, cache_sem) = rest[n_carried:]
    step = pl.program_id(0)
    n_steps = pl.num_programs(0)
    n_rows = n_batch * n_tok

    def cache_copies(blk):
        slot = blk % CACHE_SLOTS
        rows = pl.ds(blk * SAMPLE_BG, SAMPLE_BG)
        return (pltpu.make_async_copy(kc_ref.at[layer, rows], kc_buf.at[slot], cache_sem.at[0, slot]),
                pltpu.make_async_copy(vc_ref.at[layer, rows], vc_buf.at[slot], cache_sem.at[1, slot]))

    @pl.when(step == 0)
    def _():
        for blk in range(CACHE_AHEAD):
            for cp in cache_copies(blk):
                cp.start()

    @pl.when(step + CACHE_AHEAD < n_steps)
    def _():
        for cp in cache_copies(step + CACHE_AHEAD):
            cp.start()

    def per_token(m):
        return jnp.broadcast_to(m, (n_batch, n_tok, m.shape[-1])).reshape(n_rows, m.shape[-1])

    @pl.when(step == 0)
    def _():
        mod = mod_ref[...]
        sh1, sc1 = per_token(mod[:, 0:1, :]), per_token(mod[:, 1:2, :])
        h = (_rms(x_ref[...], n1g_ref[...]) * (1.0 + sc1) + sh1).astype(BF16)
        h_scr[...] = h
        q = _head_rms(_dot(h, win_ref[:, _Q0:_Q0 + D_A]), qg_ref[...], e_ref)
        k = _head_rms(_dot(h, win_ref[:, _K0:_K0 + D_A]), kg_ref[...], e_ref)
        v = _dot(h, win_ref[:, _V0:_V0 + D_A])
        ks_ref[...] = k
        vs_ref[...] = v
        kn_scr[...] = k.astype(BF16)
        vn_scr[...] = v.astype(BF16)
        q_scr[...] = (q * (HEAD_DIM ** -0.5)).astype(BF16)
        for hh in range(N_HEADS):
            bias_scr[hh * n_tok:(hh + 1) * n_tok, :] = _rel_rows(rel_ref, hh, n_tok)[:, 0:cache_len + LANES]

    lane_head = lax.broadcasted_iota(jnp.int32, (n_tok, D_A), 1) // HEAD_DIM
    for cp in cache_copies(step):
        cp.wait()
    slot = step % CACHE_SLOTS
    for i in range(SAMPLE_BG):
        r0 = pl.multiple_of((step * SAMPLE_BG + i) * n_tok, n_tok)
        qb = q_scr[pl.ds(r0, n_tok), :]
        q_bd = jnp.concatenate([jnp.where(lane_head == hh, qb, jnp.zeros_like(qb)) for hh in range(N_HEADS)],
                               axis=0)
        kn = kn_scr[pl.ds(r0, n_tok), :]
        vn = vn_scr[pl.ds(r0, n_tok), :]
        kt = kc_buf[slot, i].reshape(D_A, cache_len).astype(BF16)
        vt = vc_buf[slot, i].reshape(D_A, cache_len).astype(BF16)
        s_c = _dot(q_bd, kt) + bias_scr[:, 0:cache_len]
        s_n = _dot_nt(q_bd, kn) + bias_scr[:, cache_len:cache_len + n_tok]
        m = jnp.maximum(jnp.max(s_c, axis=-1, keepdims=True), jnp.max(s_n, axis=-1, keepdims=True))
        e_c = jnp.exp(s_c - m)
        e_n = jnp.exp(s_n - m)
        l = jnp.sum(e_c, axis=-1, keepdims=True) + jnp.sum(e_n, axis=-1, keepdims=True)
        o_all = (_dot_nt(e_c.astype(BF16), vt) + _dot(e_n.astype(BF16), vn)) / l
        o = jnp.zeros((n_tok, D_A), F32)
        for hh in range(N_HEADS):
            o = jnp.where(lane_head == hh, o_all[hh * n_tok:(hh + 1) * n_tok, :], o)
        oa_scr[pl.ds(r0, n_tok), :] = o

    def spatial_half(half_scr, c0):
        z = _gelu(_dot(h_scr[...], win_ref[:, c0:c0 + D_B]))
        for g in range(N_GROUPS):
            half_scr[g] = z[:, g * GROUP_DIM:(g + 1) * GROUP_DIM]

    def spatial_mix():
        vbn = _rms(jnp.concatenate([vb_scr[g] for g in range(N_GROUPS)], axis=1), vg_ref[...])
        sv_ref[...] = vbn
        for g in range(N_GROUPS):
            vb_scr[g] = vbn[:, g * GROUP_DIM:(g + 1) * GROUP_DIM]
        for g in range(N_GROUPS):
            at_pos = [vb_scr[g, pl.ds(s, n_batch, stride=n_tok), :] for s in range(n_tok)]
            for t in range(n_tok):
                mix = bsp_ref[g, t:t + 1, :]
                for s in range(t + 1):
                    mix = mix + wsp_ref[g, t, s:s + 1, :] * at_pos[s]
                ob_scr[g, pl.ds(t, n_batch, stride=n_tok), :] = ub_scr[g, pl.ds(t, n_batch, stride=n_tok), :] * mix

    def gates(c0, width):
        gate_scr[:, c0:c0 + width] = _sigmoid(_dot(h_scr[...], win_ref[:, _G0 + c0:_G0 + c0 + width]))

    stages = [functools.partial(spatial_half, ub_scr, _B0), functools.partial(spatial_half, vb_scr, _B0 + D_B),
              spatial_mix] + [functools.partial(gates, c0, D_B) for c0 in range(0, 2 * D_MODEL, D_B)]
    for i, stage in enumerate(stages):
        pl.when(step == SAMPLE_STAGE_STRIDE * i + 1)(stage)

    @pl.when(step == pl.num_programs(0) - 1)
    def _():
        ob = jnp.concatenate([ob_scr[g] for g in range(N_GROUPS)], axis=1)
        merged = (gate_scr[:, 0:D_MODEL] * _dot(oa_scr[...].astype(BF16), woa_ref[...])
                  + gate_scr[:, D_MODEL:] * _dot(ob.astype(BF16), wob_ref[...]))
        gt1 = per_token(mod_ref[...][:, 2:3, :])
        xo_ref[...] = x_ref[...] + gt1 * _dot(merged.astype(BF16), wo_ref[...])


def _mixer_sample_call(l, depth, x2, mod_all, kc_all, vc_all, p, w, n_batch, n_tok, carried):
    n_rows = n_batch * n_tok
    cache_len = kc_all.shape[-1]
    cache = pl.BlockSpec(memory_space=pl.ANY)
    n_steps = n_batch // SAMPLE_BG
    assert n_steps >= CACHE_AHEAD and CACHE_SLOTS == CACHE_AHEAD + 1
    full = lambda shape: pl.BlockSpec(shape, lambda s: (0,) * len(shape))
    state = lambda width: pl.BlockSpec((None, n_rows, width), lambda s: (l, 0, 0))
    in_specs = [
        _const_spec((n_rows, D_MODEL)),
        _layer_spec(l, (n_batch, 6, D_MODEL)),
        _layer_spec(l, (1, D_MODEL)),
        _const_spec((D_MODEL, D_IN)),
        _layer_spec(l, (1, D_A)),
        _layer_spec(l, (1, D_A)),
        _const_spec((MXU_DIM, MXU_DIM)),
        _layer_spec(l, (1, D_B)),
        _layer_spec(l, (N_HEADS, REL_SPAN)),
        cache,
        cache,
        _layer_spec(l, (N_GROUPS, n_tok, n_tok, GROUP_DIM)),
        _layer_spec(l, (N_GROUPS, n_tok, GROUP_DIM)),
        _const_spec((D_A, D_MODEL)),
        _const_spec((D_B, D_MODEL)),
        _const_spec((D_MODEL, D_MODEL)),
    ]
    return pl.pallas_call(
        functools.partial(_mixer_sample_kernel, layer=l, n_carried=len(carried), n_batch=n_batch, n_tok=n_tok,
                          cache_len=cache_len),
        grid=(n_steps,),
        input_output_aliases={len(in_specs) + i: 1 + i for i in range(len(carried))},
        in_specs=in_specs + [pl.BlockSpec(memory_space=pl.ANY)] * len(carried),
        out_specs=[full((n_rows, D_MODEL)), state(D_A), state(D_A), state(D_B)],
        out_shape=[
            jax.ShapeDtypeStruct((n_rows, D_MODEL), F32),
            jax.ShapeDtypeStruct((depth, n_rows, D_A), F32),
            jax.ShapeDtypeStruct((depth, n_rows, D_A), F32),
            jax.ShapeDtypeStruct((depth, n_rows, D_B), F32),
        ],
        scratch_shapes=[
            pltpu.VMEM((n_rows, D_MODEL), BF16),
            pltpu.VMEM((n_rows, D_A), BF16),
            pltpu.VMEM((n_rows, D_A), BF16),
            pltpu.VMEM((n_rows, D_A), BF16),
            pltpu.VMEM((n_rows, D_A), F32),
            pltpu.VMEM((N_GROUPS, n_rows, GROUP_DIM), F32),
            pltpu.VMEM((N_GROUPS, n_rows, GROUP_DIM), F32),
            pltpu.VMEM((N_GROUPS, n_rows, GROUP_DIM), F32),
            pltpu.VMEM((n_rows, 2 * D_MODEL), F32),
            pltpu.VMEM((N_HEADS * n_tok, cache_len + LANES), F32),
            pltpu.VMEM((CACHE_SLOTS, SAMPLE_BG, N_HEADS, HEAD_DIM, cache_len), F32),
            pltpu.VMEM((CACHE_SLOTS, SAMPLE_BG, N_HEADS, HEAD_DIM, cache_len), F32),
            pltpu.SemaphoreType.DMA((2, CACHE_SLOTS)),
        ],
        compiler_params=pltpu.CompilerParams(
            dimension_semantics=("arbitrary",), vmem_limit_bytes=VMEM_LIMIT),
        name="mixer_sample",
    )(x2, mod_all, p["n1g"], w["win"], p["qg"], p["kg"], p["e_mat"], p["vg"], p["rel"],
      kc_all, vc_all, p["wsp_s"], p["bsp_s"], w["woa"], w["wob"], w["wo"], *carried)


def _ffn_sample_kernel(x_ref, mod_ref, n2g_ref, wfi_ref, cw_ref, cb_ref, cc_ref, wfo_ref,
                       xo_ref, cs_ref, *, n_batch, n_tok):
    n_rows = n_batch * n_tok

    def per_token(m):
        return jnp.broadcast_to(m, (n_batch, n_tok, m.shape[-1])).reshape(n_rows, m.shape[-1])

    mod = mod_ref[...]
    cc = cc_ref[...]
    row = lax.broadcasted_iota(jnp.int32, (n_rows, 1), 0) % n_tok
    xo, gs = _conv_ffn(x_ref[...], per_token(mod[:, 3:4, :]), per_token(mod[:, 4:5, :]), per_token(mod[:, 5:6, :]),
                       per_token(cc[:, 1:2, :]), per_token(cc[:, 0:1, :]), row,
                       n2g_ref, wfi_ref, cw_ref, cb_ref, wfo_ref)
    xo_ref[...] = xo
    for g, (c0, c1) in zip(gs, FF_CHUNKS):
        cs_ref[:, :, c0:c1] = g.reshape(n_batch, n_tok, c1 - c0)[:, n_tok - SUBLANES:, :]


def _ffn_sample_call(l, x2, mod_all, cc_all, p, w, n_batch, n_tok):
    n_rows = n_batch * n_tok
    return pl.pallas_call(
        functools.partial(_ffn_sample_kernel, n_batch=n_batch, n_tok=n_tok),
        grid=(1,),
        in_specs=[
            _const_spec((n_rows, D_MODEL)),
            _layer_spec(l, (n_batch, 6, D_MODEL)),
            _layer_spec(l, (1, D_MODEL)),
            _const_spec((D_MODEL, 2 * D_FF)),
            _layer_spec(l, (CONV_W, D_FF)),
            _layer_spec(l, (1, D_FF)),
            _layer_spec(l, (n_batch, CONV_W - 1, D_FF)),
            _const_spec((D_FF, D_MODEL)),
        ],
        out_specs=[
            pl.BlockSpec((n_rows, D_MODEL), lambda c: (0, 0)),
            pl.BlockSpec((n_batch, SUBLANES, D_FF), lambda c: (0, 0, 0)),
        ],
        out_shape=[
            jax.ShapeDtypeStruct((n_rows, D_MODEL), F32),
            jax.ShapeDtypeStruct((n_batch, SUBLANES, D_FF), F32),
        ],
        compiler_params=pltpu.CompilerParams(
            dimension_semantics=("arbitrary",), vmem_limit_bytes=VMEM_LIMIT),
        name="ffn_sample",
    )(x2, mod_all, p["n2g"], w["wfi"], p["cw"], p["cb"], cc_all, w["wfo"])


def _rel_vector(rel_bias):
    far = rel_bias[..., 2 * REL_CLIP:]
    lead = ATTN_REACH - REL_CLIP
    return jnp.concatenate([
        jnp.broadcast_to(far, (*rel_bias.shape[:-1], lead)),
        rel_bias[..., ::-1],
        jnp.broadcast_to(far, (*rel_bias.shape[:-1], REL_SPAN - lead - (2 * REL_CLIP + 1))),
    ], axis=-1)


def kernel(x_prompt, x_sample, cache_attn_k, cache_attn_v, cache_ffn_conv, c_prompt, c_sample, norm1_g, norm2_g, w_ada, b_ada, w_in, q_norm_g, k_norm_g, rel_bias, v_norm_g, w_spatial, b_spatial, w_out_a, w_out_b, w_out, w_ffn_in, ffn_conv_w, ffn_conv_b, w_ffn_out):
    depth = w_in.shape[0]
    B, S, _ = x_prompt.shape
    NB, NT, _ = x_sample.shape
    cache_len = cache_attn_k.shape[2]
    assert S % FFN_TILE_T == 0 and S % TILE_T == 0 and TILE_T % Q_BLOCK == 0 and TILE_T == ATTN_REACH and S >= ATTN_REACH
    assert NB % SAMPLE_BG == 0 and NB // SAMPLE_BG >= SAMPLE_MIN_STEPS and NT % SUBLANES == 0 and NT <= MLP_CHUNK and N_HEADS * NT == LANES
    assert cache_len == ATTN_REACH and rel_bias.shape[-1] == 2 * REL_CLIP + 1

    rows = NB + B
    rows_pad = -(-rows // SUBLANES) * SUBLANES
    c_all = jnp.concatenate([c_sample, c_prompt, jnp.zeros((rows_pad - rows, D_MODEL), F32)], axis=0)
    mixer_names, ffn_names = ("win", "woa", "wob", "wo"), ("wfi", "wfo")
    mixer_f32, ffn_f32 = (w_in, w_out_a, w_out_b, w_out), (w_ffn_in, w_ffn_out)
    mod_all, *first = _ada_call(c_all, w_ada, b_ada, mixer_f32)
    mod_all = mod_all.reshape(depth, rows_pad, 6, D_MODEL)
    w = dict(zip(mixer_names, first))

    head_of = jnp.arange(MXU_DIM) // HEAD_DIM
    tril = jnp.tril(jnp.ones((NT, NT), F32))
    ws_small = w_spatial[:, :, :NT, :NT] * tril
    params = {
        "n1g": norm1_g[:, None, :], "n2g": norm2_g[:, None, :],
        "qg": jnp.tile(q_norm_g, (1, N_HEADS))[:, None, :], "kg": jnp.tile(k_norm_g, (1, N_HEADS))[:, None, :],
        "vg": v_norm_g[:, None, :],
        "e_mat": jnp.where(head_of[:, None] == head_of[None, :], 1.0 / HEAD_DIM, 0.0).astype(BF16),
        "rel": _rel_vector(rel_bias),
        "wsp": w_spatial,
        "bsp_b": jnp.broadcast_to(b_spatial[:, :, :, None], (depth, N_GROUPS, MLP_CHUNK, GROUP_DIM)),
        "wsp_s": jnp.broadcast_to(ws_small[..., None], (depth, N_GROUPS, NT, NT, GROUP_DIM)),
        "bsp_s": jnp.broadcast_to(b_spatial[:, :, :NT, None], (depth, N_GROUPS, NT, GROUP_DIM)),
        "cw": ffn_conv_w, "cb": ffn_conv_b[:, None, :],
    }
    kc_all = cache_attn_k.transpose(0, 1, 3, 4, 2)
    vc_all = cache_attn_v.transpose(0, 1, 3, 4, 2)

    xp = x_prompt
    xs = x_sample.reshape(NB * NT, D_MODEL)
    conv_p, conv_s = [], []
    prompt_state, sample_state = (), ()
    for l in range(depth):
        xp, *outs = _mixer_prompt_call(l, depth, xp, mod_all, NB, params, w, ffn_f32, tuple(prompt_state))
        prompt_state, w_ffn = outs[:2], dict(zip(ffn_names, outs[2:]))
        nxt = mixer_f32 if l + 1 < depth else ()
        xp, cp, *outs = _ffn_prompt_call(l, l + 1, xp, mod_all, NB, params, w_ffn, nxt)
        xs, *sample_state = _mixer_sample_call(l, depth, xs, mod_all, kc_all, vc_all, params, w, NB, NT,
                                               tuple(sample_state))
        xs, cs = _ffn_sample_call(l, xs, mod_all, cache_ffn_conv, params, w_ffn, NB, NT)
        w = dict(zip(mixer_names, outs))
        conv_p.append(cp[:, SUBLANES - (CONV_W - 1):])
        conv_s.append(cs[:, SUBLANES - (CONV_W - 1):])
    new_k_prompt, new_v_prompt = (
        s.reshape(depth, B, N_HEADS, HEAD_DIM, ATTN_REACH).transpose(0, 1, 4, 2, 3) for s in prompt_state)
    ks, vs, sv = sample_state
    return (xp, xs.reshape(NB, NT, D_MODEL), new_k_prompt, new_v_prompt, jnp.stack(conv_p),
            ks.reshape(depth, NB, NT, N_HEADS, HEAD_DIM), vs.reshape(depth, NB, NT, N_HEADS, HEAD_DIM),
            sv.reshape(depth, NB, NT, N_GROUPS, GROUP_DIM), jnp.stack(conv_s))
```
